```python
import jax
import jax.numpy as jnp
from jax import lax
import numpy as np

D_MODEL = 4096
BATCH = 4
SEQ = 2048
DEPTH = 2
DEC_BATCH = 32
DEC_SEQ = 4
PAST_LEN = 16384
PAGE_SIZE = 128

F32 = jnp.float32
RMS_EPS = 1e-6
N_A_LAYERS = (DEPTH + 1) // 2
N_C_LAYERS = DEPTH // 2
RET_HEADS = 8
RET_DK = 256
RET_DV = 256
RET_Q = RET_HEADS * RET_DK
RET_V = RET_HEADS * RET_DV
RET_CHUNK = 128
RWKV_HEADS = 32
RWKV_HD = 64
RWKV_W = RWKV_HEADS * RWKV_HD
LORA_W = 128
LORA_A = 128
LORA_G = 96
RWKV_LN_EPS = 64e-5
N_SHIFT = 3 * RWKV_W + LORA_W + LORA_A + LORA_G
N_IN_A = 2 * RET_Q + 2 * RET_V + N_SHIFT
MIX_A = RET_V + RWKV_W
SWA_HEADS = 64
SWA_KV = 8
SWA_G = SWA_HEADS // SWA_KV
SWA_HD = 64
WINDOW = 128
MIX_C = SWA_HEADS * SWA_HD
KV_W = SWA_KV * SWA_HD
N_IN_C = MIX_C + 2 * KV_W
N_GROUPS = 4
EXP_PER_GROUP = 8
N_EXPERTS = N_GROUPS * EXP_PER_GROUP
EXP_TOPK = 2
EXP_HIDDEN = D_MODEL // 8

kernel_name = 'hybrid_retention_rwkv7_swa_hmoe_step'


def rms_norm(x, g, eps=RMS_EPS):
    xf = x.astype(F32)
    y = xf * lax.rsqrt(jnp.mean(xf * xf, axis=-1, keepdims=True) + eps)
    return (y * g.astype(F32)).astype(x.dtype)


def retention(q, k, v, s0, chunk):
    B, T, H, DK = q.shape
    DV = v.shape[-1]
    n = T // chunk
    log_g = jnp.log(1.0 - 2.0 ** (-5.0 - jnp.arange(H, dtype=F32)))
    pos = jnp.arange(chunk, dtype=F32)
    diff = pos[:, None] - pos[None, :]
    causal = diff >= 0
    decay_mask = jnp.where(causal, jnp.exp(jnp.where(causal, diff, 0.0) * log_g[:, None, None]), 0.0)
    q_decay = jnp.exp((pos + 1.0)[None, :] * log_g[:, None])
    k_decay = jnp.exp((chunk - 1.0 - pos)[None, :] * log_g[:, None])
    chunk_decay = jnp.exp(chunk * log_g)

    def to_chunks(a):
        return a.astype(F32).reshape(B, n, chunk, H, a.shape[-1]).swapaxes(0, 1)

    def step(s, inp):
        qi, ki, vi = inp
        scores = jnp.einsum('bihd,bjhd->bhij', qi, ki) * decay_mask
        inner = jnp.einsum('bhij,bjhv->bihv', scores, vi)
        cross = jnp.einsum('bihd,hi,bhdv->bihv', qi, q_decay, s)
        s_new = chunk_decay[None, :, None, None] * s + jnp.einsum('bjhd,hj,bjhv->bhdv', ki, k_decay, vi)
        return s_new, inner + cross

    s_T, o = lax.scan(step, s0.astype(F32), (to_chunks(q), to_chunks(k), to_chunks(v)))
    return o.swapaxes(0, 1).reshape(B, T, H, DV), s_T


def rwkv7_scan(r, w, k, v, kk, a, s0):
    def step(s, inp):
        rt, wt, kt, vt, kkt, at = inp
        s = (s * wt[:, :, None, :]
             - jnp.einsum('bhvk,bhk,bhj->bhvj', s, kkt, kkt * at)
             + vt[..., :, None] * kt[..., None, :])
        return s, jnp.einsum('bhvk,bhk->bhv', s, rt)

    xs = tuple(t.astype(F32).swapaxes(0, 1) for t in (r, w, k, v, kk, a))
    s_T, y = lax.scan(step, s0.astype(F32), xs)
    return y.swapaxes(0, 1), s_T


def sink_softmax(s, sink):
    m = jnp.maximum(jnp.max(s, axis=-1, keepdims=True), sink)
    e = jnp.exp(s - m)
    return e / (jnp.sum(e, axis=-1, keepdims=True) + jnp.exp(sink - m))


def alibi_slopes(n_heads):
    return 2.0 ** (-8.0 * jnp.arange(1, n_heads + 1, dtype=F32) / n_heads)


def band_attend(q, k, v, diff, mask, sinks):
    s = jnp.einsum('bnqkgd,bnskd->bnkgqs', q.astype(F32), k.astype(F32)) * (SWA_HD ** -0.5)
    s = s - alibi_slopes(SWA_HEADS).reshape(SWA_KV, SWA_G)[:, :, None, None] * diff
    s = jnp.where(mask[None, :, None, None], s, -jnp.inf)
    p = sink_softmax(s, sinks.astype(F32).reshape(SWA_KV, SWA_G)[:, :, None, None])
    return jnp.einsum('bnkgqs,bnskd->bnqkgd', p, v.astype(F32))


def hier_moe(h, w_group, b_group, w_expert, b_expert, w_gate, w_up, w_down):
    M = h.shape[0]
    g_logit = (h @ w_group).astype(F32) + b_group.astype(F32)
    g_prob = jax.nn.softmax(g_logit, axis=-1)
    g_idx = jnp.argmax(g_logit, axis=-1)
    g_gate = jnp.take_along_axis(g_prob, g_idx[:, None], axis=-1)
    e_logit = ((h @ w_expert).astype(F32) + b_expert.astype(F32)).reshape(M, N_GROUPS, EXP_PER_GROUP)
    e_in = jnp.take_along_axis(e_logit, g_idx[:, None, None], axis=1)[:, 0]
    top_v, top_i = lax.top_k(e_in, EXP_TOPK)
    top_w = jax.nn.softmax(top_v, axis=-1) * g_gate
    eid = g_idx[:, None] * EXP_PER_GROUP + top_i
    gates = jnp.einsum('mk,mke->me', top_w, jax.nn.one_hot(eid, N_EXPERTS, dtype=F32))
    hid = jax.nn.silu(h @ w_gate) * (h @ w_up)
    hid = (hid.reshape(M, N_EXPERTS, EXP_HIDDEN).astype(F32) * gates[:, :, None]).astype(h.dtype)
    return hid.reshape(M, N_EXPERTS * EXP_HIDDEN) @ w_down


def setup_inputs(seed: int = 0) -> dict:
    key = jax.random.key(seed)
    ks = iter(jax.random.split(key, 48))

    def nrm(shape, scale=1.0):
        return jax.random.normal(next(ks), shape, F32) * scale

    def uni(shape, lo, hi):
        return jax.random.uniform(next(ks), shape, F32, lo, hi)

    LA, LC = N_A_LAYERS, N_C_LAYERS
    return {
        'x_prompt': nrm((BATCH, SEQ, D_MODEL)),
        'x_sample': nrm((DEC_BATCH, DEC_SEQ, D_MODEL)),
        'state_ret': nrm((LA, DEC_BATCH, RET_HEADS, RET_DK, RET_DV), 0.05),
        'state_rwkv': nrm((LA, DEC_BATCH, RWKV_HEADS, RWKV_HD, RWKV_HD), 0.1),
        'state_rwkv_shift': nrm((LA, DEC_BATCH, N_SHIFT)),
        'cache_swa_k': nrm((LC, DEC_BATCH, WINDOW, SWA_KV, SWA_HD)),
        'cache_swa_v': nrm((LC, DEC_BATCH, WINDOW, SWA_KV, SWA_HD)),
        'norm_mix_g': 1.0 + nrm((DEPTH, D_MODEL), 0.02),
        'norm_ffn_g': 1.0 + nrm((DEPTH, D_MODEL), 0.02),
        'a_w_in': nrm((LA, D_MODEL, N_IN_A), D_MODEL ** -0.5),
        'a_w_out': nrm((LA, MIX_A, D_MODEL), MIX_A ** -0.5),
        'ret_norm_g': 1.0 + nrm((LA, RET_V), 0.02),
        'rwkv_mu': uni((LA, N_SHIFT), 0.0, 1.0),
        'rwkv_w0': uni((LA, RWKV_W), -6.0, -1.0),
        'rwkv_w2': nrm((LA, LORA_W, RWKV_W), 0.5 * LORA_W ** -0.5),
        'rwkv_a0': nrm((LA, RWKV_W), 0.5),
        'rwkv_a2': nrm((LA, LORA_A, RWKV_W), LORA_A ** -0.5),
        'rwkv_g2': nrm((LA, LORA_G, RWKV_W), LORA_G ** -0.5),
        'rwkv_kk_scale': 0.85 + nrm((LA, RWKV_W), 0.02),
        'rwkv_ka': 1.0 + nrm((LA, RWKV_W), 0.02),
        'rwkv_rk': nrm((LA, RWKV_HEADS, RWKV_HD), 0.1),
        'rwkv_lnx_g': 1.0 + nrm((LA, RWKV_W), 0.02),
        'rwkv_lnx_b': nrm((LA, RWKV_W), 0.01),
        'c_w_in': nrm((LC, D_MODEL, N_IN_C), D_MODEL ** -0.5),
        'c_w_out': nrm((LC, MIX_C, D_MODEL), MIX_C ** -0.5),
        'c_q_norm_g': 1.0 + nrm((LC, SWA_HD), 0.02),
        'c_k_norm_g': 1.0 + nrm((LC, SWA_HD), 0.02),
        'c_sinks': nrm((LC, SWA_HEADS), 0.5),
        'moe_w_group': nrm((DEPTH, D_MODEL, N_GROUPS), D_MODEL ** -0.5),
        'moe_b_group': nrm((DEPTH, N_GROUPS), 0.01),
        'moe_w_expert': nrm((DEPTH, D_MODEL, N_EXPERTS), D_MODEL ** -0.5),
        'moe_b_expert': nrm((DEPTH, N_EXPERTS), 0.01),
        'moe_w_gate': nrm((DEPTH, D_MODEL, N_EXPERTS * EXP_HIDDEN), D_MODEL ** -0.5),
        'moe_w_up': nrm((DEPTH, D_MODEL, N_EXPERTS * EXP_HIDDEN), D_MODEL ** -0.5),
        'moe_w_down': nrm((DEPTH, N_EXPERTS * EXP_HIDDEN, D_MODEL), EXP_HIDDEN ** -0.5),
    }


def reference(x_prompt, x_sample, state_ret, state_rwkv, state_rwkv_shift, cache_swa_k, cache_swa_v,
              norm_mix_g, norm_ffn_g, a_w_in, a_w_out, ret_norm_g, rwkv_mu, rwkv_w0, rwkv_w2,
              rwkv_a0, rwkv_a2, rwkv_g2, rwkv_kk_scale, rwkv_ka, rwkv_rk, rwkv_lnx_g, rwkv_lnx_b,
              c_w_in, c_w_out, c_q_norm_g, c_k_norm_g, c_sinks,
              moe_w_group, moe_b_group, moe_w_expert, moe_b_expert, moe_w_gate, moe_w_up, moe_w_down):

    def even_mixer(h, i, s_ret0, s_rwkv0, shift0, chunk):
        Bx, T = h.shape[:2]
        p = h @ a_w_in[i]
        q, k, v, g, ps = jnp.split(p, [RET_Q, 2 * RET_Q, 2 * RET_Q + RET_V, 2 * RET_Q + 2 * RET_V], axis=-1)
        q = q.reshape(Bx, T, RET_HEADS, RET_DK)
        k = k.reshape(Bx, T, RET_HEADS, RET_DK) * (RET_DK ** -0.5)
        v = v.reshape(Bx, T, RET_HEADS, RET_DV)
        o, s_ret = retention(q, k, v, s_ret0, chunk)
        o = rms_norm(o, ret_norm_g[i].reshape(RET_HEADS, RET_DV))
        o_ret = (jax.nn.silu(g.astype(F32)) * o.reshape(Bx, T, RET_V)).astype(h.dtype)
        ps_prev = jnp.concatenate([shift0[:, None].astype(ps.dtype), ps[:, :-1]], axis=1)
        xs = ps + (ps_prev - ps) * rwkv_mu[i]
        r, kr, vr, lw, la, lg = jnp.split(
            xs, [RWKV_W, 2 * RWKV_W, 3 * RWKV_W, 3 * RWKV_W + LORA_W, 3 * RWKV_W + LORA_W + LORA_A], axis=-1)
        w_log = -jax.nn.softplus(-(rwkv_w0[i].astype(F32) + (jnp.tanh(lw) @ rwkv_w2[i]).astype(F32))) - 0.5
        decay = jnp.exp(-jnp.exp(w_log))
        a = jax.nn.sigmoid(rwkv_a0[i].astype(F32) + (la @ rwkv_a2[i]).astype(F32))
        gate = (jax.nn.sigmoid(lg) @ rwkv_g2[i]).astype(F32)

        def heads(t):
            return t.astype(F32).reshape(Bx, T, RWKV_HEADS, RWKV_HD)

        rf, vf, dh, ah = heads(r), heads(vr), heads(decay), heads(a)
        kk = heads(kr * rwkv_kk_scale[i])
        kk = kk / jnp.maximum(jnp.sqrt(jnp.sum(kk * kk, axis=-1, keepdims=True)), 1e-12)
        kf = heads(kr) * (1.0 + (ah - 1.0) * rwkv_ka[i].astype(F32).reshape(RWKV_HEADS, RWKV_HD))
        y, s_rwkv = rwkv7_scan(rf, dh, kf, vf, kk, ah, s_rwkv0)
        mu_y = jnp.mean(y, axis=-1, keepdims=True)
        var_y = jnp.mean(jnp.square(y - mu_y), axis=-1, keepdims=True)
        y = ((y - mu_y) * lax.rsqrt(var_y + RWKV_LN_EPS)).reshape(Bx, T, RWKV_W)
        y = y * rwkv_lnx_g[i].astype(F32) + rwkv_lnx_b[i].astype(F32)
        bonus = jnp.sum(rf * kf * rwkv_rk[i].astype(F32), axis=-1, keepdims=True) * vf
        y = (y + bonus.reshape(Bx, T, RWKV_W)) * gate
        out = jnp.concatenate([o_ret, y.astype(h.dtype)], axis=-1) @ a_w_out[i]
        return out, s_ret, s_rwkv, ps[:, -1]

    def odd_mixer(h, i, kbuf, vbuf):
        Bx, T = h.shape[:2]
        p = h @ c_w_in[i]
        q = rms_norm(p[..., :MIX_C].reshape(Bx, T, SWA_KV, SWA_G, SWA_HD), c_q_norm_g[i])
        k = rms_norm(p[..., MIX_C:MIX_C + KV_W].reshape(Bx, T, SWA_KV, SWA_HD), c_k_norm_g[i])
        v = p[..., MIX_C + KV_W:].reshape(Bx, T, SWA_KV, SWA_HD)
        if kbuf is None:
            nb = T // WINDOW
            qb = q.reshape(Bx, nb, WINDOW, SWA_KV, SWA_G, SWA_HD)
            kb = k.reshape(Bx, nb, WINDOW, SWA_KV, SWA_HD)
            vb = v.reshape(Bx, nb, WINDOW, SWA_KV, SWA_HD)
            pad = ((0, 0), (1, 0), (0, 0), (0, 0), (0, 0))
            kk = jnp.concatenate([jnp.pad(kb[:, :-1], pad), kb], axis=2)
            vv = jnp.concatenate([jnp.pad(vb[:, :-1], pad), vb], axis=2)
            qpos = jnp.arange(WINDOW)
            kpos = jnp.arange(2 * WINDOW) - WINDOW
            diff = qpos[:, None] - kpos[None, :]
            valid = (diff >= 0) & (diff < WINDOW)
            has_prev = jnp.arange(nb) > 0
            mask = valid[None] & (has_prev[:, None, None] | (kpos >= 0)[None, None, :])
            o = band_attend(qb, kk, vv, diff.astype(F32), mask, c_sinks[i])
            new_k, new_v = k[:, -WINDOW:], v[:, -WINDOW:]
        else:
            kk = jnp.concatenate([kbuf.astype(k.dtype), k], axis=1)
            vv = jnp.concatenate([vbuf.astype(v.dtype), v], axis=1)
            qpos = jnp.arange(T)
            kpos = jnp.arange(WINDOW + T) - WINDOW
            diff = qpos[:, None] - kpos[None, :]
            mask = ((diff >= 0) & (diff < WINDOW))[None]
            o = band_attend(q[:, None], kk[:, None], vv[:, None], diff.astype(F32), mask, c_sinks[i])
            new_k, new_v = kk[:, -WINDOW:], vv[:, -WINDOW:]
        out = o.reshape(Bx, T, MIX_C).astype(h.dtype) @ c_w_out[i]
        return out, new_k, new_v

    def run(x, is_prompt):
        Bx, T = x.shape[:2]
        rets, rwkvs, shifts, kbufs, vbufs = [], [], [], [], []
        for l in range(DEPTH):
            i = l // 2
            h = rms_norm(x, norm_mix_g[l])
            if l % 2 == 0:
                if is_prompt:
                    s_r0 = jnp.zeros((Bx, RET_HEADS, RET_DK, RET_DV), x.dtype)
                    s_w0 = jnp.zeros((Bx, RWKV_HEADS, RWKV_HD, RWKV_HD), x.dtype)
                    sh0 = jnp.zeros((Bx, N_SHIFT), x.dtype)
                    chunk = RET_CHUNK if T % RET_CHUNK == 0 else T
                else:
                    s_r0, s_w0, sh0, chunk = state_ret[i], state_rwkv[i], state_rwkv_shift[i], T
                out, s_r, s_w, sh = even_mixer(h, i, s_r0, s_w0, sh0, chunk)
                rets.append(s_r.astype(x.dtype))
                rwkvs.append(s_w.astype(x.dtype))
                shifts.append(sh.astype(x.dtype))
            else:
                kb0 = None if is_prompt else cache_swa_k[i]
                vb0 = None if is_prompt else cache_swa_v[i]
                out, nk, nv = odd_mixer(h, i, kb0, vb0)
                kbufs.append(nk.astype(x.dtype))
                vbufs.append(nv.astype(x.dtype))
            x = x + out
            h = rms_norm(x, norm_ffn_g[l])
            y = hier_moe(h.reshape(Bx * T, D_MODEL), moe_w_group[l], moe_b_group[l], moe_w_expert[l],
                         moe_b_expert[l], moe_w_gate[l], moe_w_up[l], moe_w_down[l])
            x = x + y.reshape(Bx, T, D_MODEL)
        return x, jnp.stack(rets), jnp.stack(rwkvs), jnp.stack(shifts), jnp.stack(kbufs), jnp.stack(vbufs)

    y_prompt, p_ret, p_rwkv, p_shift, p_k, p_v = run(x_prompt, True)
    y_sample, s_ret, s_rwkv, s_shift, s_k, s_v = run(x_sample, False)
    return (y_prompt, y_sample, p_ret, p_rwkv, p_shift, p_k, p_v, s_ret, s_rwkv, s_shift, s_k, s_v)
```

```python
import functools

import jax
import jax.numpy as jnp
from jax import lax
from jax.experimental import pallas as pl
from jax.experimental.pallas import tpu as pltpu

F32 = jnp.float32
BF16 = jnp.bfloat16

RMS_EPS = 1e-6
RWKV_LN_EPS = 64e-5
RET_CHUNK = 128

LANE = 128
VMEM_LIMIT = 56 * 1024 * 1024

MM_TM = 640
MM_TN = 512
NORM_TM = 320
MOE_TM = 256
SCAN_TC = 32


def _cparams(n_axes):
    return pltpu.CompilerParams(dimension_semantics=("arbitrary",) * n_axes,
                                vmem_limit_bytes=VMEM_LIMIT)


def _norm_kernel(x_ref, g_ref, h_ref):
    x = x_ref[...]
    y = x * lax.rsqrt(jnp.mean(x * x, axis=-1, keepdims=True) + RMS_EPS) * g_ref[...]
    h_ref[...] = y.astype(BF16)


def _norm_router_kernel(x_ref, g_ref, wr_ref, h_ref, logit_ref):
    x = x_ref[...]
    y = x * lax.rsqrt(jnp.mean(x * x, axis=-1, keepdims=True) + RMS_EPS) * g_ref[...]
    h_ref[...] = y.astype(BF16)
    logit_ref[...] = jnp.dot(y, wr_ref[...], precision=lax.Precision.HIGHEST,
                             preferred_element_type=F32)


def _rms_norm(x, g, w_router=None):
    m, d = x.shape
    tm = NORM_TM if m % NORM_TM == 0 else m
    grid = (m // tm,)
    x_spec = pl.BlockSpec((tm, d), lambda i: (i, 0))
    g_spec = pl.BlockSpec((1, d), lambda i: (0, 0))
    if w_router is None:
        return pl.pallas_call(
            _norm_kernel, grid=grid, in_specs=[x_spec, g_spec], out_specs=x_spec,
            out_shape=jax.ShapeDtypeStruct((m, d), BF16), compiler_params=_cparams(1),
            name="rms_norm")(x, g.reshape(1, d))
    nr = w_router.shape[1]
    return pl.pallas_call(
        _norm_router_kernel, grid=grid,
        in_specs=[x_spec, g_spec, pl.BlockSpec((d, nr), lambda i: (0, 0))],
        out_specs=[x_spec, pl.BlockSpec((tm, nr), lambda i: (i, 0))],
        out_shape=[jax.ShapeDtypeStruct((m, d), BF16), jax.ShapeDtypeStruct((m, nr), F32)],
        compiler_params=_cparams(1), name="rms_norm_router")(x, g.reshape(1, d), w_router)


def _mm_kernel(*refs, n_parts, has_res):
    x_refs = refs[:n_parts]
    w_refs = refs[n_parts:2 * n_parts]
    pos = 2 * n_parts
    res_ref = refs[pos] if has_res else None
    pos += int(has_res)
    o_ref = refs[pos]
    wb_refs = refs[pos + 1:]

    @pl.when(pl.program_id(1) == 0)
    def _():
        for w_ref, wb_ref in zip(w_refs, wb_refs):
            wb_ref[...] = w_ref[...].astype(BF16)

    acc = None
    for x_ref, wb_ref in zip(x_refs, wb_refs):
        d = jnp.dot(x_ref[...], wb_ref[...], preferred_element_type=F32)
        acc = d if acc is None else acc + d
    if has_res:
        acc = acc + res_ref[...]
    o_ref[...] = acc


def _matmul(xs, w, n_out, col_block0=0, tn=MM_TN, res=None, name="proj"):
    m = xs[0].shape[0]
    tm = MM_TM if m % MM_TM == 0 else m
    assert n_out % tn == 0
    grid = (n_out // tn, m // tm)
    in_specs, w_specs, scratch = [], [], []
    row = 0
    for x in xs:
        kp = x.shape[1]
        in_specs.append(pl.BlockSpec((tm, kp), lambda j, i: (i, 0)))
        assert row % kp == 0
        rb = row // kp
        w_specs.append(pl.BlockSpec((kp, tn), lambda j, i, rb=rb: (rb, j + col_block0)))
        scratch.append(pltpu.VMEM((kp, tn), BF16))
        row += kp
    assert row == w.shape[0]
    args = list(xs) + [w] * len(xs)
    in_specs = in_specs + w_specs
    if res is not None:
        in_specs.append(pl.BlockSpec((tm, tn), lambda j, i: (i, j)))
        args.append(res)
    return pl.pallas_call(
        functools.partial(_mm_kernel, n_parts=len(xs), has_res=res is not None),
        grid=grid, in_specs=in_specs,
        out_specs=pl.BlockSpec((tm, tn), lambda j, i: (i, j)),
        out_shape=jax.ShapeDtypeStruct((m, n_out), F32),
        scratch_shapes=scratch, compiler_params=_cparams(2), name=name)(*args)


def _ret_finish(o, g, gain):
    o = o * lax.rsqrt(jnp.mean(o * o, axis=-1, keepdims=True) + RMS_EPS) * gain
    return ((g / (1.0 + jnp.exp(-g))) * o).astype(BF16)


def _ret_prompt_kernel(q_ref, k_ref, v_ref, g_ref, gain_ref, lg_ref, o_ref, s_ref, *, dk):
    c = pl.program_id(2)
    chunk = q_ref.shape[0]

    @pl.when(c == 0)
    def _():
        s_ref[...] = jnp.zeros_like(s_ref)

    lg = lg_ref[:, :1]
    row = lax.broadcasted_iota(jnp.int32, (chunk, chunk), 0)
    col = lax.broadcasted_iota(jnp.int32, (chunk, chunk), 1)
    diff = (row - col).astype(F32)
    causal = diff >= 0
    decay_mask = jnp.where(causal, jnp.exp(jnp.where(causal, diff, 0.0) * lg), 0.0)
    pos = lax.broadcasted_iota(jnp.int32, (chunk, 1), 0).astype(F32)
    q_decay = jnp.exp((pos + 1.0) * lg)
    k_decay = jnp.exp((chunk - 1.0 - pos) * lg)
    chunk_decay = jnp.exp(chunk * lg)

    q = q_ref[...]
    k = k_ref[...] * (dk ** -0.5)
    vb = v_ref[...].astype(BF16)
    s = s_ref[...]
    scores = lax.dot_general(q.astype(BF16), k.astype(BF16), (((1,), (1,)), ((), ())),
                             preferred_element_type=F32) * decay_mask
    inner = jnp.dot(scores.astype(BF16), vb, preferred_element_type=F32)
    cross = jnp.dot((q * q_decay).astype(BF16), s.astype(BF16), preferred_element_type=F32)
    s_ref[...] = chunk_decay * s + lax.dot_general(
        (k * k_decay).astype(BF16), vb, (((0,), (0,)), ((), ())), preferred_element_type=F32)
    o_ref[...] = _ret_finish(inner + cross, g_ref[...], gain_ref[...])


def _ret_sample_kernel(q_ref, k_ref, v_ref, g_ref, gain_ref, lg_ref, s0_ref, o_ref, s_ref, acc_ref,
                       *, dk, t_len):
    b = pl.program_id(1)
    rows = q_ref.shape[0]
    lg = lg_ref[:, :1]
    r_b = lax.broadcasted_iota(jnp.int32, (rows, 1), 0) // t_len
    r_t = (lax.broadcasted_iota(jnp.int32, (rows, 1), 0) % t_len).astype(F32)
    q = q_ref[...]
    k = k_ref[...] * (dk ** -0.5)
    vb = v_ref[...].astype(BF16)

    @pl.when(b == 0)
    def _():
        row = lax.broadcasted_iota(jnp.int32, (rows, rows), 0)
        col = lax.broadcasted_iota(jnp.int32, (rows, rows), 1)
        diff = (row % t_len - col % t_len).astype(F32)
        ok = (row // t_len == col // t_len) & (diff >= 0)
        mask = jnp.where(ok, jnp.exp(jnp.where(ok, diff, 0.0) * lg), 0.0)
        scores = lax.dot_general(q.astype(BF16), k.astype(BF16), (((1,), (1,)), ((), ())),
                                 preferred_element_type=F32) * mask
        acc_ref[...] = jnp.dot(scores.astype(BF16), vb, preferred_element_type=F32)

    sel = r_b == b
    s0 = s0_ref[...]
    cross = jnp.dot((q * jnp.exp((r_t + 1.0) * lg)).astype(BF16), s0.astype(BF16),
                    preferred_element_type=F32)
    acc_ref[...] += jnp.where(sel, cross, 0.0)
    k_decay = jnp.where(sel, jnp.exp((t_len - 1.0 - r_t) * lg), 0.0)
    s_ref[...] = jnp.exp(t_len * lg) * s0 + lax.dot_general(
        (k * k_decay).astype(BF16), vb, (((0,), (0,)), ((), ())), preferred_element_type=F32)

    @pl.when(b == pl.num_programs(1) - 1)
    def _():
        o_ref[...] = _ret_finish(acc_ref[...], g_ref[...], gain_ref[...])


def _ret_log_decay(n_heads):
    lg = jnp.log(1.0 - 2.0 ** (-5.0 - jnp.arange(n_heads, dtype=F32)))
    return jnp.broadcast_to(lg[:, None, None], (n_heads, 1, LANE))


def _retention_prompt(p, gain, n_batch, seq, n_heads, dk):
    nc = seq // RET_CHUNK
    col = lambda grp: (lambda b, h, c: (b * nc + c, grp * n_heads + h))
    blk = lambda grp: pl.BlockSpec((RET_CHUNK, dk), col(grp))
    return pl.pallas_call(
        functools.partial(_ret_prompt_kernel, dk=dk),
        grid=(n_batch, n_heads, nc),
        in_specs=[blk(0), blk(1), blk(2), blk(3),
                  pl.BlockSpec((None, 1, dk), lambda b, h, c: (h, 0, 0)),
                  pl.BlockSpec((None, 1, LANE), lambda b, h, c: (h, 0, 0))],
        out_specs=[pl.BlockSpec((RET_CHUNK, dk), lambda b, h, c: (b * nc + c, h)),
                   pl.BlockSpec((None, None, dk, dk), lambda b, h, c: (b, h, 0, 0))],
        out_shape=[jax.ShapeDtypeStruct((n_batch * seq, n_heads * dk), BF16),
                   jax.ShapeDtypeStruct((n_batch, n_heads, dk, dk), F32)],
        compiler_params=_cparams(3), name="retention_prompt",
    )(p, p, p, p, gain.reshape(n_heads, 1, dk), _ret_log_decay(n_heads))


def _retention_sample(p, gain, s0, row0, n_batch, t_len, n_heads, dk):
    rows = n_batch * t_len
    rb = row0 // rows
    blk = lambda grp: pl.BlockSpec((rows, dk), lambda h, b, grp=grp: (rb, grp * n_heads + h))
    return pl.pallas_call(
        functools.partial(_ret_sample_kernel, dk=dk, t_len=t_len),
        grid=(n_heads, n_batch),
        in_specs=[blk(0), blk(1), blk(2), blk(3),
                  pl.BlockSpec((None, 1, dk), lambda h, b: (h, 0, 0)),
                  pl.BlockSpec((None, 1, LANE), lambda h, b: (h, 0, 0)),
                  pl.BlockSpec((None, None, dk, dk), lambda h, b: (b, h, 0, 0))],
        out_specs=[pl.BlockSpec((rows, dk), lambda h, b: (0, h)),
                   pl.BlockSpec((None, None, dk, dk), lambda h, b: (b, h, 0, 0))],
        out_shape=[jax.ShapeDtypeStruct((rows, n_heads * dk), BF16),
                   jax.ShapeDtypeStruct((n_batch, n_heads, dk, dk), F32)],
        scratch_shapes=[pltpu.VMEM((rows, dk), F32)],
        compiler_params=_cparams(2), name="retention_sample",
    )(p, p, p, p, gain.reshape(n_heads, 1, dk), _ret_log_decay(n_heads), s0)


def _sigmoid(x):
    return 1.0 / (1.0 + jnp.exp(-x))


def _rwkv_prep_kernel(*refs, sample, seq, t_len):
    if sample:
        (r_c, k_c, v_c, t_c, r_s, k_s, v_s, t_s,
         mu_r, mu_k, mu_v, mu_t, w0, w2, a0, a2, g2,
         r_o, w_o, k_o, v_o, a_o, g_o) = refs
    else:
        (r_c, k_c, v_c, t_c, r_p, k_p, v_p, t_p,
         mu_r, mu_k, mu_v, mu_t, w0, w2, a0, a2, g2,
         r_o, w_o, k_o, v_o, a_o, g_o) = refs
    tm = r_c.shape[0]
    row = lax.broadcasted_iota(jnp.int32, (tm, 1), 0)
    seq_start = (pl.program_id(0) * tm) % seq == 0

    def mix(cur_ref, other_ref, mu_ref):
        cur = cur_ref[...]
        rolled = pltpu.roll(cur, 1, 0)
        if sample:
            prev = jnp.where(row % t_len == 0, other_ref[...], rolled)
        else:
            last = other_ref[7:8, :]
            first = jnp.where(seq_start, jnp.zeros_like(last), last)
            prev = jnp.where(row == 0, first, rolled)
        return cur + (prev - cur) * mu_ref[...]

    other = (r_s, k_s, v_s, t_s) if sample else (r_p, k_p, v_p, t_p)
    r_o[...] = mix(r_c, other[0], mu_r)
    k_o[...] = mix(k_c, other[1], mu_k)
    v_o[...] = mix(v_c, other[2], mu_v)
    tail = mix(t_c, other[3], mu_t)
    lw, la, lgt = tail[:, :LANE], tail[:, LANE:2 * LANE], tail[:, 2 * LANE:]

    z = -(w0[...] + jnp.dot(jnp.tanh(lw).astype(BF16), w2[...].astype(BF16),
                            preferred_element_type=F32))
    softplus = jnp.maximum(z, 0.0) + jnp.log1p(jnp.exp(-jnp.abs(z)))
    w_o[...] = jnp.exp(-jnp.exp(-softplus - 0.5))
    a_o[...] = _sigmoid(a0[...] + jnp.dot(la.astype(BF16), a2[...].astype(BF16),
                                          preferred_element_type=F32))
    g_o[...] = jnp.dot(_sigmoid(lgt).astype(BF16), g2[...].astype(BF16),
                       preferred_element_type=F32)


def _rwkv_prep(p, tail, row0, n_rows, col_block0, width, params, shift_rows, seq, t_len):
    sample = shift_rows is not None
    tm = n_rows if sample else 256
    rb0 = row0 // tm
    grid = (n_rows // tm,)
    cur = lambda grp: pl.BlockSpec((tm, width), lambda i, grp=grp: (rb0 + i, col_block0 + grp))
    tail_w = tail.shape[1]
    in_specs = [cur(0), cur(1), cur(2), pl.BlockSpec((tm, tail_w), lambda i: (rb0 + i, 0))]
    args = [p, p, p, tail]
    if sample:
        sh_main, sh_tail = shift_rows
        in_specs += [pl.BlockSpec((tm, width), lambda i, grp=grp: (0, grp)) for grp in range(3)]
        in_specs += [pl.BlockSpec((tm, tail_w), lambda i: (0, 0))]
        args += [sh_main, sh_main, sh_main, sh_tail]
    else:
        per = tm // 8
        prev_rb = lambda i: jnp.maximum((rb0 + i) * per - 1, 0)
        in_specs += [pl.BlockSpec((8, width), lambda i, grp=grp: (prev_rb(i), col_block0 + grp))
                     for grp in range(3)]
        in_specs += [pl.BlockSpec((8, tail_w), lambda i: (prev_rb(i), 0))]
        args += [p, p, p, tail]
    full = lambda a: pl.BlockSpec(a.shape, lambda i: (0,) * a.ndim)
    in_specs += [full(a) for a in params]
    args += list(params)
    out_spec = pl.BlockSpec((tm, width), lambda i: (i, 0))
    return pl.pallas_call(
        functools.partial(_rwkv_prep_kernel, sample=sample, seq=seq, t_len=t_len),
        grid=grid, in_specs=in_specs, out_specs=[out_spec] * 6,
        out_shape=[jax.ShapeDtypeStruct((n_rows, width), F32)] * 6,
        compiler_params=_cparams(1), name="rwkv_prep_sample" if sample else "rwkv_prep_prompt",
    )(*args)


def _rwkv_scan_kernel(r_ref, w_ref, k_ref, v_ref, a_ref, g_ref,
                      kks_ref, ka_ref, rk_ref, lng_ref, lnb_ref, s0_ref,
                      y_ref, s_ref, ys_ref):
    n_v = s_ref.shape[0]

    @pl.when(pl.program_id(1) == 0)
    def _():
        s_ref[...] = s0_ref[...]

    def step(t, carry):
        r = r_ref[t]
        w = w_ref[t]
        kr = k_ref[t]
        v = v_ref[t]
        a = a_ref[t]
        kks = kr * kks_ref[...]
        kk = kks / jnp.maximum(jnp.sqrt(jnp.sum(kks * kks, axis=0, keepdims=True)), 1e-12)
        kf = kr * (1.0 + (a - 1.0) * ka_ref[...])
        kka = kk * a
        for vi in range(n_v):
            s = s_ref[vi]
            sa = jnp.sum(s * kk, axis=0, keepdims=True)
            s = s * w - sa * kka + v[vi:vi + 1, :] * kf
            s_ref[vi] = s
            ys_ref[vi:vi + 1, :] = jnp.sum(s * r, axis=0, keepdims=True)
        y = ys_ref[...]
        mu = jnp.mean(y, axis=0, keepdims=True)
        var = jnp.mean(jnp.square(y - mu), axis=0, keepdims=True)
        y = (y - mu) * lax.rsqrt(var + RWKV_LN_EPS) * lng_ref[...] + lnb_ref[...]
        bonus = jnp.sum(r * kf * rk_ref[...], axis=0, keepdims=True) * v
        y_ref[t] = (y + bonus) * g_ref[t]
        return carry

    lax.fori_loop(0, r_ref.shape[0], step, 0)


def _rwkv_scan(seqs, params, s0):
    t_len, n, pairs = seqs[0].shape
    tc = SCAN_TC if t_len % SCAN_TC == 0 else t_len
    grid = (pairs // LANE, t_len // tc)
    seq_spec = pl.BlockSpec((tc, n, LANE), lambda g, t: (t, 0, g))
    par_spec = pl.BlockSpec((n, LANE), lambda g, t: (0, g))
    st_spec = pl.BlockSpec((n, n, LANE), lambda g, t: (0, 0, g))
    return pl.pallas_call(
        _rwkv_scan_kernel, grid=grid,
        in_specs=[seq_spec] * 6 + [par_spec] * 5 + [st_spec],
        out_specs=[seq_spec, st_spec],
        out_shape=[jax.ShapeDtypeStruct((t_len, n, pairs), F32),
                   jax.ShapeDtypeStruct((n, n, pairs), F32)],
        scratch_shapes=[pltpu.VMEM((n, LANE), F32)],
        compiler_params=_cparams(2), name="rwkv_scan",
    )(*seqs, *params, s0)


def _swa_kernel(sink_ref, q_ref, kc_ref, vc_ref, kp_ref, vp_ref, qg_ref, kg_ref,
                o_ref, kn_ref, *, n_kv, group, hd, norm_prev, first_has_prev, blocks_per_seq):
    tq = q_ref.shape[0]
    win = kc_ref.shape[0]
    n_heads = n_kv * group
    has_prev = jnp.logical_or(first_has_prev, pl.program_id(0) % blocks_per_seq > 0)

    def rms(x, g):
        return x * lax.rsqrt(jnp.mean(x * x, axis=-1, keepdims=True) + RMS_EPS) * g

    qpos = lax.broadcasted_iota(jnp.int32, (tq, 2 * win), 0)
    kpos = lax.broadcasted_iota(jnp.int32, (tq, 2 * win), 1) - win
    diff = qpos - kpos
    valid = (diff >= 0) & (diff < win) & (kpos >= jnp.where(has_prev, -win, 0))
    diff_f = diff.astype(F32)
    qg = qg_ref[...]
    kg = kg_ref[...]

    for j in range(n_kv):
        sl = slice(j * hd, (j + 1) * hd)
        kc = rms(kc_ref[:, sl], kg)
        kn_ref[:, sl] = kc
        kp = kp_ref[:, sl]
        if norm_prev:
            kp = rms(kp, kg)
        kb = jnp.concatenate([kp, kc], axis=0).astype(BF16)
        vb = jnp.concatenate([vp_ref[:, sl], vc_ref[:, sl]], axis=0).astype(BF16)
        for gi in range(group):
            h = j * group + gi
            qh = rms(q_ref[:, h * hd:(h + 1) * hd], qg).astype(BF16)
            s = lax.dot_general(qh, kb, (((1,), (1,)), ((), ())), preferred_element_type=F32)
            slope = 2.0 ** (-8.0 * (h + 1) / n_heads)
            s = s * (hd ** -0.5) - slope * diff_f
            s = jnp.where(valid, s, -jnp.inf)
            sink = sink_ref[h]
            m = jnp.maximum(jnp.max(s, axis=-1, keepdims=True), sink)
            e = jnp.exp(s - m)
            p = e / (jnp.sum(e, axis=-1, keepdims=True) + jnp.exp(sink - m))
            o_ref[:, h * hd:(h + 1) * hd] = jnp.dot(p.astype(BF16), vb,
                                                    preferred_element_type=F32).astype(BF16)


def _swa(q_arr, kc_arr, vc_arr, kp_arr, vp_arr, maps, n_blocks, tq, win, qg, kg, sinks,
         n_kv, group, hd, norm_prev, first_has_prev, blocks_per_seq):
    q_map, kc_map, vc_map, kp_map, vp_map = maps
    n_heads = n_kv * group
    return pl.pallas_call(
        functools.partial(_swa_kernel, n_kv=n_kv, group=group, hd=hd, norm_prev=norm_prev,
                          first_has_prev=first_has_prev, blocks_per_seq=blocks_per_seq),
        grid=(n_blocks,),
        in_specs=[pl.BlockSpec(memory_space=pltpu.SMEM),
                  pl.BlockSpec((tq, n_heads * hd), q_map),
                  pl.BlockSpec((win, n_kv * hd), kc_map),
                  pl.BlockSpec((win, n_kv * hd), vc_map),
                  pl.BlockSpec((win, n_kv * hd), kp_map),
                  pl.BlockSpec((win, n_kv * hd), vp_map),
                  pl.BlockSpec((1, hd), lambda i: (0, 0)),
                  pl.BlockSpec((1, hd), lambda i: (0, 0))],
        out_specs=[pl.BlockSpec((tq, n_heads * hd), lambda i: (i, 0)),
                   pl.BlockSpec((win, n_kv * hd), lambda i: (i, 0))],
        out_shape=[jax.ShapeDtypeStruct((n_blocks * tq, n_heads * hd), BF16),
                   jax.ShapeDtypeStruct((n_blocks * win, n_kv * hd), F32)],
        compiler_params=_cparams(1), name="swa_prev%d" % int(norm_prev),
    )(sinks, q_arr, kc_arr, vc_arr, kp_arr, vp_arr, qg.reshape(1, hd), kg.reshape(1, hd))


def _moe_up_kernel(te_ref, nu_ref, xs_ref, wg_ref, wu_ref, rw_ref, hid_ref, wgb_ref, wub_ref):
    i = pl.program_id(1)
    changed = jnp.logical_or(i == 0, te_ref[i] != te_ref[jnp.maximum(i - 1, 0)])

    @pl.when(changed)
    def _():
        wgb_ref[...] = wg_ref[...].astype(BF16)
        wub_ref[...] = wu_ref[...].astype(BF16)

    @pl.when(i < nu_ref[0])
    def _():
        x = xs_ref[...]
        g = jnp.dot(x, wgb_ref[...], preferred_element_type=F32)
        u = jnp.dot(x, wub_ref[...], preferred_element_type=F32)
        hid_ref[...] = ((g / (1.0 + jnp.exp(-g))) * u * rw_ref[...]).astype(BF16)

    @pl.when(i >= nu_ref[0])
    def _():
        hid_ref[...] = jnp.zeros_like(hid_ref)


def _moe_down_kernel(te_ref, nu_ref, hid_ref, wd_ref, out_ref, wdb_ref):
    i = pl.program_id(0)
    changed = jnp.logical_or(i == 0, te_ref[i] != te_ref[jnp.maximum(i - 1, 0)])

    @pl.when(changed)
    def _():
        wdb_ref[...] = wd_ref[...].astype(BF16)

    @pl.when(i < nu_ref[0])
    def _():
        out_ref[...] = jnp.dot(hid_ref[...], wdb_ref[...], preferred_element_type=F32)

    @pl.when(i >= nu_ref[0])
    def _():
        out_ref[...] = jnp.zeros_like(out_ref)


def _moe(x, h, logits, b_group, b_expert, w_gate, w_up, w_down, layer):
    m, d = h.shape
    n_groups = b_group.shape[0]
    n_experts = b_expert.shape[0]
    per_group = n_experts // n_groups
    f = w_gate.shape[2] // n_experts
    top_k = 2

    g_logit = logits[:, :n_groups] + b_group
    g_prob = jax.nn.softmax(g_logit, axis=-1)
    g_idx = jnp.argmax(g_logit, axis=-1)
    g_gate = jnp.take_along_axis(g_prob, g_idx[:, None], axis=-1)
    e_logit = (logits[:, n_groups:n_groups + n_experts] + b_expert).reshape(m, n_groups, per_group)
    e_in = jnp.take_along_axis(e_logit, g_idx[:, None, None], axis=1)[:, 0]
    top_v, top_i = lax.top_k(e_in, top_k)
    top_w = jax.nn.softmax(top_v, axis=-1) * g_gate
    eid = (g_idx[:, None] * per_group + top_i).astype(jnp.int32)

    n_pairs = m * top_k
    n_tiles = (n_pairs + n_experts * (MOE_TM - 1)) // MOE_TM + 1
    n_rows = n_tiles * MOE_TM
    flat_e = eid.reshape(-1)
    order = jnp.argsort(flat_e, stable=True).astype(jnp.int32)
    sorted_e = flat_e[order]
    counts = jnp.sum(flat_e[:, None] == jnp.arange(n_experts, dtype=jnp.int32)[None, :], axis=0,
                     dtype=jnp.int32)
    padded = ((counts + MOE_TM - 1) // MOE_TM) * MOE_TM
    pad_end = jnp.cumsum(padded)
    pad_start = pad_end - padded
    start = jnp.cumsum(counts) - counts
    dest = pad_start[sorted_e] + jnp.arange(n_pairs, dtype=jnp.int32) - start[sorted_e]
    row_token = jnp.zeros((n_rows,), jnp.int32).at[dest].set(order // top_k)
    row_w = jnp.zeros((n_rows,), F32).at[dest].set(top_w.reshape(-1)[order])
    pos = jnp.zeros((n_pairs,), jnp.int32).at[order].set(dest).reshape(m, top_k)
    n_used = (pad_end[-1] // MOE_TM).astype(jnp.int32)
    tile_row = jnp.minimum(jnp.arange(n_tiles, dtype=jnp.int32), n_used - 1) * MOE_TM
    tile_e = jnp.minimum(jnp.searchsorted(pad_end, tile_row, side="right"),
                         n_experts - 1).astype(jnp.int32)
    n_used = n_used.reshape(1)

    xs = h[row_token]
    halves = 2
    fh = f // halves
    hid = pl.pallas_call(
        _moe_up_kernel,
        grid_spec=pltpu.PrefetchScalarGridSpec(
            num_scalar_prefetch=2, grid=(halves, n_tiles),
            in_specs=[pl.BlockSpec((MOE_TM, d), lambda j, i, te, nu: (i, 0)),
                      pl.BlockSpec((None, d, fh), lambda j, i, te, nu: (layer, 0, te[i] * halves + j)),
                      pl.BlockSpec((None, d, fh), lambda j, i, te, nu: (layer, 0, te[i] * halves + j)),
                      pl.BlockSpec((MOE_TM, 1), lambda j, i, te, nu: (i, 0))],
            out_specs=pl.BlockSpec((MOE_TM, fh), lambda j, i, te, nu: (i, j)),
            scratch_shapes=[pltpu.VMEM((d, fh), BF16), pltpu.VMEM((d, fh), BF16)]),
        out_shape=jax.ShapeDtypeStruct((n_rows, f), BF16),
        compiler_params=_cparams(2), name="moe_up",
    )(tile_e, n_used, xs, w_gate, w_up, row_w.reshape(n_rows, 1))
    rows = pl.pallas_call(
        _moe_down_kernel,
        grid_spec=pltpu.PrefetchScalarGridSpec(
            num_scalar_prefetch=2, grid=(n_tiles,),
            in_specs=[pl.BlockSpec((MOE_TM, f), lambda i, te, nu: (i, 0)),
                      pl.BlockSpec((None, f, d), lambda i, te, nu: (layer, te[i], 0))],
            out_specs=pl.BlockSpec((MOE_TM, d), lambda i, te, nu: (i, 0)),
            scratch_shapes=[pltpu.VMEM((f, d), BF16)]),
        out_shape=jax.ShapeDtypeStruct((n_rows, d), F32),
        compiler_params=_cparams(1), name="moe_down",
    )(tile_e, n_used, hid, w_down)
    return x + rows[pos[:, 0]] + rows[pos[:, 1]]


def _pairs_major(a, n_batch, t_len, n_heads, n):
    return a.reshape(n_batch, t_len, n_heads, n).transpose(1, 3, 0, 2).reshape(
        t_len, n, n_batch * n_heads)


def _token_major(a, n_batch, t_len, n_heads, n):
    return a.reshape(t_len, n, n_batch, n_heads).transpose(2, 0, 3, 1).reshape(
        n_batch * t_len, n_heads * n)


def _param_pairs(a, n_batch, n_heads, n):
    return jnp.tile(a.reshape(n_heads, n).T, (1, n_batch))


def _even_layer(x, h, i, groups, state_ret, state_rwkv, state_shift, wts):
    (a_w_in, a_w_out, ret_norm_g, rwkv_mu, rwkv_w0, rwkv_w2, rwkv_a0, rwkv_a2, rwkv_g2,
     rwkv_kk_scale, rwkv_ka, rwkv_rk, rwkv_lnx_g, rwkv_lnx_b) = [w[i] for w in wts]
    ret_heads, ret_dk = state_ret.shape[2], state_ret.shape[3]
    rw_heads, rw_n = state_rwkv.shape[2], state_rwkv.shape[3]
    ret_w = ret_heads * ret_dk
    rw_w = rw_heads * rw_n
    lora_w, lora_a, lora_g = rwkv_w2.shape[0], rwkv_a2.shape[0], rwkv_g2.shape[0]
    assert lora_w == LANE and lora_a == LANE and lora_g <= LANE and ret_w == rw_w
    n_main = 4 * ret_w + 3 * rw_w
    n_shift = 3 * rw_w + lora_w + lora_a + lora_g
    (pb, pt), (sb, st) = groups
    m_p, m_s = pb * pt, sb * st

    p = _matmul([h], a_w_in, n_main, name="a_in")
    tail_cols = 3 * LANE
    w_tail = jnp.pad(a_w_in[:, n_main:], ((0, 0), (0, tail_cols - (lora_w + lora_a + lora_g))))
    tail = _matmul([h], w_tail, tail_cols, tn=tail_cols, name="a_in_tail")

    o_ret_p, ret_p = _retention_prompt(p, ret_norm_g, pb, pt, ret_heads, ret_dk)
    o_ret_s, ret_s = _retention_sample(p, ret_norm_g, state_ret[i], m_p, sb, st, ret_heads, ret_dk)

    pad_g = lambda a: jnp.pad(a, ((0, LANE - a.shape[0]), (0, 0)))
    mu = rwkv_mu
    mu_tail = jnp.pad(mu[3 * rw_w:], (0, tail_cols - (n_shift - 3 * rw_w)))
    prep_params = [mu[:rw_w].reshape(1, -1), mu[rw_w:2 * rw_w].reshape(1, -1),
                   mu[2 * rw_w:3 * rw_w].reshape(1, -1), mu_tail.reshape(1, -1),
                   rwkv_w0.reshape(1, -1), rwkv_w2, rwkv_a0.reshape(1, -1), rwkv_a2,
                   pad_g(rwkv_g2)]
    cb0 = (4 * ret_w) // rw_w
    shift0 = state_shift[i]
    sh_main = jnp.repeat(shift0[:, :3 * rw_w], st, axis=0)
    sh_tail = jnp.repeat(jnp.pad(shift0[:, 3 * rw_w:], ((0, 0), (0, tail_cols - (n_shift - 3 * rw_w)))),
                         st, axis=0)
    seq_p = _rwkv_prep(p, tail, 0, m_p, cb0, rw_w, prep_params, None, pt, st)
    seq_s = _rwkv_prep(p, tail, m_p, m_s, cb0, rw_w, prep_params, (sh_main, sh_tail), pt, st)

    outs = []
    for seqs, nb, nt, s0 in ((seq_p, pb, pt, None), (seq_s, sb, st, state_rwkv[i])):
        scan_params = [_param_pairs(a, nb, rw_heads, rw_n)
                       for a in (rwkv_kk_scale, rwkv_ka, rwkv_rk, rwkv_lnx_g, rwkv_lnx_b)]
        if s0 is None:
            s0_l = jnp.zeros((rw_n, rw_n, nb * rw_heads), F32)
        else:
            s0_l = s0.transpose(2, 3, 0, 1).reshape(rw_n, rw_n, nb * rw_heads)
        y, s_t = _rwkv_scan([_pairs_major(a, nb, nt, rw_heads, rw_n) for a in seqs], scan_params, s0_l)
        y = _token_major(y, nb, nt, rw_heads, rw_n).astype(BF16)
        s_t = s_t.reshape(rw_n, rw_n, nb, rw_heads).transpose(2, 3, 0, 1)
        outs.append((y, s_t))
    (y_p, rwkv_p), (y_s, rwkv_s) = outs

    o_ret = jnp.concatenate([o_ret_p, o_ret_s], axis=0)
    y = jnp.concatenate([y_p, y_s], axis=0)
    x = _matmul([o_ret, y], a_w_out, a_w_out.shape[1], res=x, name="a_out")

    def last_ps(row0, nb, nt):
        rows = row0 + jnp.arange(nb) * nt + nt - 1
        return jnp.concatenate([p[rows, 4 * ret_w:], tail[rows, :n_shift - 3 * rw_w]], axis=-1)

    return x, (ret_p, rwkv_p, last_ps(0, pb, pt)), (ret_s, rwkv_s, last_ps(m_p, sb, st))


def _odd_layer(x, h, i, groups, cache_k, cache_v, wts):
    c_w_in, c_w_out, c_q_norm_g, c_k_norm_g, c_sinks = [w[i] for w in wts]
    win, n_kv, hd = cache_k.shape[2], cache_k.shape[3], cache_k.shape[4]
    n_heads = c_sinks.shape[0]
    group = n_heads // n_kv
    mix_c = n_heads * hd
    kv_w = n_kv * hd
    (pb, pt), (sb, st) = groups
    m_p = pb * pt
    assert pt % win == 0 and mix_c % kv_w == 0
    p = _matmul([h], c_w_in, mix_c + 2 * kv_w, name="c_in")
    kcb, vcb = mix_c // kv_w, mix_c // kv_w + 1

    nb_p = m_p // win
    maps = (lambda n: (n, 0), lambda n: (n, kcb), lambda n: (n, vcb),
            lambda n: (jnp.maximum(n - 1, 0), kcb), lambda n: (jnp.maximum(n - 1, 0), vcb))
    o_p, kn_p = _swa(p, p, p, p, p, maps, nb_p, win, win, c_q_norm_g, c_k_norm_g, c_sinks,
                     n_kv, group, hd, True, False, pt // win)

    tq = 16
    ps = p[m_p:].reshape(sb, st, -1)
    q_s = jnp.pad(ps[:, :, :mix_c], ((0, 0), (0, tq - st), (0, 0))).reshape(sb * tq, mix_c)
    kv_pad = lambda a: jnp.pad(a, ((0, 0), (0, win - st), (0, 0))).reshape(sb * win, kv_w)
    k_s = kv_pad(ps[:, :, mix_c:mix_c + kv_w])
    v_s = kv_pad(ps[:, :, mix_c + kv_w:])
    kc0 = cache_k[i].reshape(sb * win, kv_w)
    vc0 = cache_v[i].reshape(sb * win, kv_w)
    same = lambda n: (n, 0)
    o_s, kn_s = _swa(q_s, k_s, v_s, kc0, vc0, (same,) * 5, sb, tq, win, c_q_norm_g, c_k_norm_g,
                     c_sinks, n_kv, group, hd, False, True, 1)
    o_s = o_s.reshape(sb, tq, mix_c)[:, :st].reshape(sb * st, mix_c)

    x = _matmul([jnp.concatenate([o_p, o_s], axis=0)], c_w_out, c_w_out.shape[1], res=x, name="c_out")

    new_k_p = kn_p.reshape(pb, pt, n_kv, hd)[:, -win:]
    new_v_p = p[:m_p, mix_c + kv_w:].reshape(pb, pt, n_kv, hd)[:, -win:]
    kn_s = kn_s.reshape(sb, win, n_kv, hd)[:, :st]
    new_k_s = jnp.concatenate([cache_k[i], kn_s], axis=1)[:, -win:]
    new_v_s = jnp.concatenate([cache_v[i], ps[:, :, mix_c + kv_w:].reshape(sb, st, n_kv, hd)],
                              axis=1)[:, -win:]
    return x, (new_k_p, new_v_p), (new_k_s, new_v_s)


def kernel(x_prompt, x_sample, state_ret, state_rwkv, state_rwkv_shift, cache_swa_k, cache_swa_v, norm_mix_g, norm_ffn_g, a_w_in, a_w_out, ret_norm_g, rwkv_mu, rwkv_w0, rwkv_w2, rwkv_a0, rwkv_a2, rwkv_g2, rwkv_kk_scale, rwkv_ka, rwkv_rk, rwkv_lnx_g, rwkv_lnx_b, c_w_in, c_w_out, c_q_norm_g, c_k_norm_g, c_sinks, moe_w_group, moe_b_group, moe_w_expert, moe_b_expert, moe_w_gate, moe_w_up, moe_w_down):
    pb, pt, d = x_prompt.shape
    sb, st, _ = x_sample.shape
    groups = ((pb, pt), (sb, st))
    depth = norm_mix_g.shape[0]
    x = jnp.concatenate([x_prompt.reshape(pb * pt, d), x_sample.reshape(sb * st, d)], axis=0)
    a_wts = (a_w_in, a_w_out, ret_norm_g, rwkv_mu, rwkv_w0, rwkv_w2, rwkv_a0, rwkv_a2, rwkv_g2,
             rwkv_kk_scale, rwkv_ka, rwkv_rk, rwkv_lnx_g, rwkv_lnx_b)
    c_wts = (c_w_in, c_w_out, c_q_norm_g, c_k_norm_g, c_sinks)
    st_p = [[] for _ in range(5)]
    st_s = [[] for _ in range(5)]
    for l in range(depth):
        i = l // 2
        h = _rms_norm(x, norm_mix_g[l])
        if l % 2 == 0:
            x, sp, ss = _even_layer(x, h, i, groups, state_ret, state_rwkv, state_rwkv_shift, a_wts)
            for j in range(3):
                st_p[j].append(sp[j])
                st_s[j].append(ss[j])
        else:
            x, sp, ss = _odd_layer(x, h, i, groups, cache_swa_k, cache_swa_v, c_wts)
            for j in range(2):
                st_p[3 + j].append(sp[j])
                st_s[3 + j].append(ss[j])
        n_router = moe_w_group.shape[2] + moe_w_expert.shape[2]
        w_router = jnp.pad(jnp.concatenate([moe_w_group[l], moe_w_expert[l]], axis=1),
                           ((0, 0), (0, LANE - n_router)))
        h, logits = _rms_norm(x, norm_ffn_g[l], w_router)
        x = _moe(x, h, logits, moe_b_group[l], moe_b_expert[l], moe_w_gate, moe_w_up, moe_w_down, l)
    m_p = pb * pt
    y_prompt = x[:m_p].reshape(pb, pt, d)
    y_sample = x[m_p:].reshape(sb, st, d)
    return (y_prompt, y_sample) + tuple(jnp.stack(s) for s in st_p) + tuple(jnp.stack(s) for s in st_s)
```

```python
import functools

import jax
import jax.numpy as jnp
from jax import lax
from jax.experimental import pallas as pl
from jax.experimental.pallas import tpu as pltpu

F32 = jnp.float32
BF16 = jnp.bfloat16

RMS_EPS = 1e-6
RWKV_LN_EPS = 64e-5
RET_CHUNK = 128

LANE = 128
VMEM_LIMIT = 56 * 1024 * 1024

MM_TM = 640
MM_TN = 512
NORM_TM = 320
MOE_TM = 256
SCAN_TC = 32


def _cparams(n_axes):
    return pltpu.CompilerParams(dimension_semantics=("arbitrary",) * n_axes,
                                vmem_limit_bytes=VMEM_LIMIT)


def _norm_kernel(x_ref, g_ref, h_ref):
    x = x_ref[...]
    y = x * lax.rsqrt(jnp.mean(x * x, axis=-1, keepdims=True) + RMS_EPS) * g_ref[...]
    h_ref[...] = y.astype(BF16)


def _norm_router_kernel(x_ref, g_ref, wr_ref, h_ref, logit_ref):
    x = x_ref[...]
    y = x * lax.rsqrt(jnp.mean(x * x, axis=-1, keepdims=True) + RMS_EPS) * g_ref[...]
    h_ref[...] = y.astype(BF16)
    logit_ref[...] = jnp.dot(y, wr_ref[...], precision=lax.Precision.HIGHEST,
                             preferred_element_type=F32)


def _norm_first_kernel(xp_ref, xs_ref, g_ref, h_ref, x_ref, *, n_prompt_tiles):
    x = jnp.where(pl.program_id(0) < n_prompt_tiles, xp_ref[...], xs_ref[...])
    x_ref[...] = x
    y = x * lax.rsqrt(jnp.mean(x * x, axis=-1, keepdims=True) + RMS_EPS) * g_ref[...]
    h_ref[...] = y.astype(BF16)


def _rms_norm_first(xp, xs, g):
    (m_p, d), m_s = xp.shape, xs.shape[0]
    tm = LANE
    assert m_p % tm == 0 and m_s % tm == 0
    npt = m_p // tm
    spec = pl.BlockSpec((tm, d), lambda i: (i, 0))
    return pl.pallas_call(
        functools.partial(_norm_first_kernel, n_prompt_tiles=npt), grid=((m_p + m_s) // tm,),
        in_specs=[pl.BlockSpec((tm, d), lambda i: (jnp.minimum(i, npt - 1), 0)),
                  pl.BlockSpec((tm, d), lambda i: (jnp.maximum(i - npt, 0), 0)),
                  pl.BlockSpec((1, d), lambda i: (0, 0))],
        out_specs=[spec, spec],
        out_shape=[jax.ShapeDtypeStruct((m_p + m_s, d), BF16),
                   jax.ShapeDtypeStruct((m_p + m_s, d), F32)],
        compiler_params=_cparams(1), name="rms_norm_first")(xp, xs, g.reshape(1, d))


def _rms_norm(x, g, w_router=None):
    m, d = x.shape
    tm = NORM_TM if m % NORM_TM == 0 else m
    grid = (m // tm,)
    x_spec = pl.BlockSpec((tm, d), lambda i: (i, 0))
    g_spec = pl.BlockSpec((1, d), lambda i: (0, 0))
    if w_router is None:
        return pl.pallas_call(
            _norm_kernel, grid=grid, in_specs=[x_spec, g_spec], out_specs=x_spec,
            out_shape=jax.ShapeDtypeStruct((m, d), BF16), compiler_params=_cparams(1),
            name="rms_norm")(x, g.reshape(1, d))
    nr = w_router.shape[1]
    return pl.pallas_call(
        _norm_router_kernel, grid=grid,
        in_specs=[x_spec, g_spec, pl.BlockSpec((d, nr), lambda i: (0, 0))],
        out_specs=[x_spec, pl.BlockSpec((tm, nr), lambda i: (i, 0))],
        out_shape=[jax.ShapeDtypeStruct((m, d), BF16), jax.ShapeDtypeStruct((m, nr), F32)],
        compiler_params=_cparams(1), name="rms_norm_router")(x, g.reshape(1, d), w_router)


def _mm_kernel(*refs, n_parts, has_res):
    x_refs = refs[:n_parts]
    w_refs = refs[n_parts:2 * n_parts]
    pos = 2 * n_parts
    res_ref = refs[pos] if has_res else None
    pos += int(has_res)
    o_ref = refs[pos]
    wb_refs = refs[pos + 1:]

    @pl.when(pl.program_id(1) == 0)
    def _():
        for w_ref, wb_ref in zip(w_refs, wb_refs):
            wb_ref[...] = w_ref[...].astype(BF16)

    acc = None
    for x_ref, wb_ref in zip(x_refs, wb_refs):
        d = jnp.dot(x_ref[...], wb_ref[...], preferred_element_type=F32)
        acc = d if acc is None else acc + d
    if has_res:
        acc = acc + res_ref[...]
    o_ref[...] = acc


def _matmul(xs, w, n_out, col_block0=0, tn=MM_TN, res=None, name="proj"):
    m = xs[0].shape[0]
    tm = MM_TM if m % MM_TM == 0 else m
    assert n_out % tn == 0
    grid = (n_out // tn, m // tm)
    in_specs, w_specs, scratch = [], [], []
    row = 0
    for x in xs:
        kp = x.shape[1]
        in_specs.append(pl.BlockSpec((tm, kp), lambda j, i: (i, 0)))
        assert row % kp == 0
        rb = row // kp
        w_specs.append(pl.BlockSpec((kp, tn), lambda j, i, rb=rb: (rb, j + col_block0)))
        scratch.append(pltpu.VMEM((kp, tn), BF16))
        row += kp
    assert row == w.shape[0]
    args = list(xs) + [w] * len(xs)
    in_specs = in_specs + w_specs
    if res is not None:
        in_specs.append(pl.BlockSpec((tm, tn), lambda j, i: (i, j)))
        args.append(res)
    return pl.pallas_call(
        functools.partial(_mm_kernel, n_parts=len(xs), has_res=res is not None),
        grid=grid, in_specs=in_specs,
        out_specs=pl.BlockSpec((tm, tn), lambda j, i: (i, j)),
        out_shape=jax.ShapeDtypeStruct((m, n_out), F32),
        scratch_shapes=scratch, compiler_params=_cparams(2), name=name)(*args)


def _ret_finish(o, g, gain):
    o = o * lax.rsqrt(jnp.mean(o * o, axis=-1, keepdims=True) + RMS_EPS) * gain
    return ((g / (1.0 + jnp.exp(-g))) * o).astype(BF16)


def _ret_prompt_kernel(q_ref, k_ref, v_ref, g_ref, gain_ref, lg_ref, o_ref, s_ref, *, dk):
    c = pl.program_id(2)
    chunk = q_ref.shape[0]

    @pl.when(c == 0)
    def _():
        s_ref[...] = jnp.zeros_like(s_ref)

    lg = lg_ref[:, :1]
    row = lax.broadcasted_iota(jnp.int32, (chunk, chunk), 0)
    col = lax.broadcasted_iota(jnp.int32, (chunk, chunk), 1)
    diff = (row - col).astype(F32)
    causal = diff >= 0
    decay_mask = jnp.where(causal, jnp.exp(jnp.where(causal, diff, 0.0) * lg), 0.0)
    pos = lax.broadcasted_iota(jnp.int32, (chunk, 1), 0).astype(F32)
    q_decay = jnp.exp((pos + 1.0) * lg)
    k_decay = jnp.exp((chunk - 1.0 - pos) * lg)
    chunk_decay = jnp.exp(chunk * lg)

    q = q_ref[...]
    k = k_ref[...] * (dk ** -0.5)
    vb = v_ref[...].astype(BF16)
    s = s_ref[...]
    scores = lax.dot_general(q.astype(BF16), k.astype(BF16), (((1,), (1,)), ((), ())),
                             preferred_element_type=F32) * decay_mask
    inner = jnp.dot(scores.astype(BF16), vb, preferred_element_type=F32)
    cross = jnp.dot((q * q_decay).astype(BF16), s.astype(BF16), preferred_element_type=F32)
    s_ref[...] = chunk_decay * s + lax.dot_general(
        (k * k_decay).astype(BF16), vb, (((0,), (0,)), ((), ())), preferred_element_type=F32)
    o_ref[...] = _ret_finish(inner + cross, g_ref[...], gain_ref[...])


def _ret_sample_kernel(q_ref, k_ref, v_ref, g_ref, gain_ref, lg_ref, s0_ref, o_ref, s_ref, acc_ref,
                       *, dk, t_len):
    b = pl.program_id(1)
    rows = q_ref.shape[0]
    lg = lg_ref[:, :1]
    r_b = lax.broadcasted_iota(jnp.int32, (rows, 1), 0) // t_len
    r_t = (lax.broadcasted_iota(jnp.int32, (rows, 1), 0) % t_len).astype(F32)
    q = q_ref[...]
    k = k_ref[...] * (dk ** -0.5)
    vb = v_ref[...].astype(BF16)

    @pl.when(b == 0)
    def _():
        row = lax.broadcasted_iota(jnp.int32, (rows, rows), 0)
        col = lax.broadcasted_iota(jnp.int32, (rows, rows), 1)
        diff = (row % t_len - col % t_len).astype(F32)
        ok = (row // t_len == col // t_len) & (diff >= 0)
        mask = jnp.where(ok, jnp.exp(jnp.where(ok, diff, 0.0) * lg), 0.0)
        scores = lax.dot_general(q.astype(BF16), k.astype(BF16), (((1,), (1,)), ((), ())),
                                 preferred_element_type=F32) * mask
        acc_ref[...] = jnp.dot(scores.astype(BF16), vb, preferred_element_type=F32)

    sel = r_b == b
    s0 = s0_ref[...]
    cross = jnp.dot((q * jnp.exp((r_t + 1.0) * lg)).astype(BF16), s0.astype(BF16),
                    preferred_element_type=F32)
    acc_ref[...] += jnp.where(sel, cross, 0.0)
    k_decay = jnp.where(sel, jnp.exp((t_len - 1.0 - r_t) * lg), 0.0)
    s_ref[...] = jnp.exp(t_len * lg) * s0 + lax.dot_general(
        (k * k_decay).astype(BF16), vb, (((0,), (0,)), ((), ())), preferred_element_type=F32)

    @pl.when(b == pl.num_programs(1) - 1)
    def _():
        o_ref[...] = _ret_finish(acc_ref[...], g_ref[...], gain_ref[...])


def _ret_log_decay(n_heads):
    lg = jnp.log(1.0 - 2.0 ** (-5.0 - jnp.arange(n_heads, dtype=F32)))
    return jnp.broadcast_to(lg[:, None, None], (n_heads, 1, LANE))


def _retention_prompt(p, gain, n_batch, seq, n_heads, dk):
    nc = seq // RET_CHUNK
    col = lambda grp: (lambda b, h, c: (b * nc + c, grp * n_heads + h))
    blk = lambda grp: pl.BlockSpec((RET_CHUNK, dk), col(grp))
    return pl.pallas_call(
        functools.partial(_ret_prompt_kernel, dk=dk),
        grid=(n_batch, n_heads, nc),
        in_specs=[blk(0), blk(1), blk(2), blk(3),
                  pl.BlockSpec((None, 1, dk), lambda b, h, c: (h, 0, 0)),
                  pl.BlockSpec((None, 1, LANE), lambda b, h, c: (h, 0, 0))],
        out_specs=[pl.BlockSpec((RET_CHUNK, dk), lambda b, h, c: (b * nc + c, h)),
                   pl.BlockSpec((None, None, dk, dk), lambda b, h, c: (b, h, 0, 0))],
        out_shape=[jax.ShapeDtypeStruct((n_batch * seq, n_heads * dk), BF16),
                   jax.ShapeDtypeStruct((n_batch, n_heads, dk, dk), F32)],
        compiler_params=_cparams(3), name="retention_prompt",
    )(p, p, p, p, gain.reshape(n_heads, 1, dk), _ret_log_decay(n_heads))


def _retention_sample(p, gain, s0, row0, n_batch, t_len, n_heads, dk):
    rows = n_batch * t_len
    rb = row0 // rows
    blk = lambda grp: pl.BlockSpec((rows, dk), lambda h, b, grp=grp: (rb, grp * n_heads + h))
    return pl.pallas_call(
        functools.partial(_ret_sample_kernel, dk=dk, t_len=t_len),
        grid=(n_heads, n_batch),
        in_specs=[blk(0), blk(1), blk(2), blk(3),
                  pl.BlockSpec((None, 1, dk), lambda h, b: (h, 0, 0)),
                  pl.BlockSpec((None, 1, LANE), lambda h, b: (h, 0, 0)),
                  pl.BlockSpec((None, None, dk, dk), lambda h, b: (b, h, 0, 0))],
        out_specs=[pl.BlockSpec((rows, dk), lambda h, b: (0, h)),
                   pl.BlockSpec((None, None, dk, dk), lambda h, b: (b, h, 0, 0))],
        out_shape=[jax.ShapeDtypeStruct((rows, n_heads * dk), BF16),
                   jax.ShapeDtypeStruct((n_batch, n_heads, dk, dk), F32)],
        scratch_shapes=[pltpu.VMEM((rows, dk), F32)],
        compiler_params=_cparams(2), name="retention_sample",
    )(p, p, p, p, gain.reshape(n_heads, 1, dk), _ret_log_decay(n_heads), s0)


def _sigmoid(x):
    return 1.0 / (1.0 + jnp.exp(-x))


def _rwkv_prep_kernel(*refs, sample, seq, t_len):
    if sample:
        (r_c, k_c, v_c, t_c, r_s, k_s, v_s, t_s,
         mu_r, mu_k, mu_v, mu_t, w0, w2, a0, a2, g2,
         r_o, w_o, k_o, v_o, a_o, g_o) = refs
    else:
        (r_c, k_c, v_c, t_c, r_p, k_p, v_p, t_p,
         mu_r, mu_k, mu_v, mu_t, w0, w2, a0, a2, g2,
         r_o, w_o, k_o, v_o, a_o, g_o) = refs
    tm = r_c.shape[0]
    row = lax.broadcasted_iota(jnp.int32, (tm, 1), 0)
    seq_start = (pl.program_id(0) * tm) % seq == 0

    def mix(cur_ref, other_ref, mu_ref):
        cur = cur_ref[...]
        rolled = pltpu.roll(cur, 1, 0)
        if sample:
            prev = jnp.where(row % t_len == 0, other_ref[...], rolled)
        else:
            last = other_ref[7:8, :]
            first = jnp.where(seq_start, jnp.zeros_like(last), last)
            prev = jnp.where(row == 0, first, rolled)
        return cur + (prev - cur) * mu_ref[...]

    other = (r_s, k_s, v_s, t_s) if sample else (r_p, k_p, v_p, t_p)
    r_o[...] = mix(r_c, other[0], mu_r)
    k_o[...] = mix(k_c, other[1], mu_k)
    v_o[...] = mix(v_c, other[2], mu_v)
    tail = mix(t_c, other[3], mu_t)
    lw, la, lgt = tail[:, :LANE], tail[:, LANE:2 * LANE], tail[:, 2 * LANE:]

    z = -(w0[...] + jnp.dot(jnp.tanh(lw).astype(BF16), w2[...].astype(BF16),
                            preferred_element_type=F32))
    softplus = jnp.maximum(z, 0.0) + jnp.log1p(jnp.exp(-jnp.abs(z)))
    w_o[...] = jnp.exp(-jnp.exp(-softplus - 0.5))
    a_o[...] = _sigmoid(a0[...] + jnp.dot(la.astype(BF16), a2[...].astype(BF16),
                                          preferred_element_type=F32))
    g_o[...] = jnp.dot(_sigmoid(lgt).astype(BF16), g2[...].astype(BF16),
                       preferred_element_type=F32)


def _rwkv_prep(p, tail, row0, n_rows, col_block0, width, params, shift_rows, seq, t_len):
    sample = shift_rows is not None
    tm = n_rows if sample else 256
    rb0 = row0 // tm
    grid = (n_rows // tm,)
    cur = lambda grp: pl.BlockSpec((tm, width), lambda i, grp=grp: (rb0 + i, col_block0 + grp))
    tail_w = tail.shape[1]
    in_specs = [cur(0), cur(1), cur(2), pl.BlockSpec((tm, tail_w), lambda i: (rb0 + i, 0))]
    args = [p, p, p, tail]
    if sample:
        sh_main, sh_tail = shift_rows
        in_specs += [pl.BlockSpec((tm, width), lambda i, grp=grp: (0, grp)) for grp in range(3)]
        in_specs += [pl.BlockSpec((tm, tail_w), lambda i: (0, 0))]
        args += [sh_main, sh_main, sh_main, sh_tail]
    else:
        per = tm // 8
        prev_rb = lambda i: jnp.maximum((rb0 + i) * per - 1, 0)
        in_specs += [pl.BlockSpec((8, width), lambda i, grp=grp: (prev_rb(i), col_block0 + grp))
                     for grp in range(3)]
        in_specs += [pl.BlockSpec((8, tail_w), lambda i: (prev_rb(i), 0))]
        args += [p, p, p, tail]
    full = lambda a: pl.BlockSpec(a.shape, lambda i: (0,) * a.ndim)
    in_specs += [full(a) for a in params]
    args += list(params)
    out_spec = pl.BlockSpec((tm, width), lambda i: (i, 0))
    return pl.pallas_call(
        functools.partial(_rwkv_prep_kernel, sample=sample, seq=seq, t_len=t_len),
        grid=grid, in_specs=in_specs, out_specs=[out_spec] * 6,
        out_shape=[jax.ShapeDtypeStruct((n_rows, width), F32)] * 6,
        compiler_params=_cparams(1), name="rwkv_prep_sample" if sample else "rwkv_prep_prompt",
    )(*args)


def _rwkv_scan_kernel(r_ref, w_ref, k_ref, v_ref, a_ref, g_ref,
                      kks_ref, ka_ref, rk_ref, lng_ref, lnb_ref, s0_ref,
                      y_ref, s_ref, ys_ref):
    n_v = s_ref.shape[0]

    @pl.when(pl.program_id(1) == 0)
    def _():
        s_ref[...] = s0_ref[...]

    def step(t, carry):
        r = r_ref[t]
        w = w_ref[t]
        kr = k_ref[t]
        v = v_ref[t]
        a = a_ref[t]
        kks = kr * kks_ref[...]
        kk = kks / jnp.maximum(jnp.sqrt(jnp.sum(kks * kks, axis=0, keepdims=True)), 1e-12)
        kf = kr * (1.0 + (a - 1.0) * ka_ref[...])
        kka = kk * a
        for vi in range(n_v):
            s = s_ref[vi]
            sa = jnp.sum(s * kk, axis=0, keepdims=True)
            s = s * w - sa * kka + v[vi:vi + 1, :] * kf
            s_ref[vi] = s
            ys_ref[vi:vi + 1, :] = jnp.sum(s * r, axis=0, keepdims=True)
        y = ys_ref[...]
        mu = jnp.mean(y, axis=0, keepdims=True)
        var = jnp.mean(jnp.square(y - mu), axis=0, keepdims=True)
        y = (y - mu) * lax.rsqrt(var + RWKV_LN_EPS) * lng_ref[...] + lnb_ref[...]
        bonus = jnp.sum(r * kf * rk_ref[...], axis=0, keepdims=True) * v
        y_ref[t] = (y + bonus) * g_ref[t]
        return carry

    lax.fori_loop(0, r_ref.shape[0], step, 0)


def _rwkv_scan(seqs, params, s0):
    t_len, n, pairs = seqs[0].shape
    tc = SCAN_TC if t_len % SCAN_TC == 0 else t_len
    grid = (pairs // LANE, t_len // tc)
    seq_spec = pl.BlockSpec((tc, n, LANE), lambda g, t: (t, 0, g))
    par_spec = pl.BlockSpec((n, LANE), lambda g, t: (0, g))
    st_spec = pl.BlockSpec((n, n, LANE), lambda g, t: (0, 0, g))
    return pl.pallas_call(
        _rwkv_scan_kernel, grid=grid,
        in_specs=[seq_spec] * 6 + [par_spec] * 5 + [st_spec],
        out_specs=[seq_spec, st_spec],
        out_shape=[jax.ShapeDtypeStruct((t_len, n, pairs), F32),
                   jax.ShapeDtypeStruct((n, n, pairs), F32)],
        scratch_shapes=[pltpu.VMEM((n, LANE), F32)],
        compiler_params=_cparams(2), name="rwkv_scan",
    )(*seqs, *params, s0)


def _swa_kernel(sink_ref, q_ref, kc_ref, vc_ref, kp_ref, vp_ref, qg_ref, kg_ref,
                o_ref, kn_ref, *, n_kv, group, hd, rows_per_head, stacked, norm_prev,
                first_has_prev, blocks_per_seq):
    rph = rows_per_head
    win = kc_ref.shape[0]
    n_heads = n_kv * group
    has_prev = jnp.logical_or(first_has_prev, pl.program_id(0) % blocks_per_seq > 0)

    def rms(x, g):
        return x * lax.rsqrt(jnp.mean(x * x, axis=-1, keepdims=True) + RMS_EPS) * g

    qpos = lax.broadcasted_iota(jnp.int32, (rph, 2 * win), 0)
    kpos = lax.broadcasted_iota(jnp.int32, (rph, 2 * win), 1) - win
    diff = qpos - kpos
    valid = (diff >= 0) & (diff < win) & (kpos >= jnp.where(has_prev, -win, 0))
    neg_dist = jnp.where(valid, -(diff.astype(F32)), -jnp.inf)
    qg = qg_ref[...]
    kg = kg_ref[...]

    for j in range(n_kv):
        sl = slice(j * hd, (j + 1) * hd)
        kc = rms(kc_ref[:, sl], kg)
        kn_ref[:, sl] = kc
        kp = kp_ref[:, sl]
        if norm_prev:
            kp = rms(kp, kg)
        kb = jnp.concatenate([kp, kc], axis=0).astype(BF16)
        vb = jnp.concatenate([vp_ref[:, sl], vc_ref[:, sl]], axis=0).astype(BF16)
        if stacked:
            q = q_ref[:, sl]
        else:
            q = jnp.concatenate([q_ref[:, (j * group + gi) * hd:(j * group + gi + 1) * hd]
                                 for gi in range(group)], axis=0)
        s = lax.dot_general(rms(q, qg).astype(BF16), kb, (((1,), (1,)), ((), ())),
                            preferred_element_type=F32)
        es, inv = [], []
        for gi in range(group):
            h = j * group + gi
            slope = 2.0 ** (-8.0 * (h + 1) / n_heads)
            sh = s[gi * rph:(gi + 1) * rph] * (hd ** -0.5) + slope * neg_dist
            sink = sink_ref[h]
            m = jnp.maximum(jnp.max(sh, axis=-1, keepdims=True), sink)
            e = jnp.exp(sh - m)
            inv.append(1.0 / (jnp.sum(e, axis=-1, keepdims=True) + jnp.exp(sink - m)))
            es.append(e.astype(BF16))
        o = jnp.dot(jnp.concatenate(es, axis=0), vb, preferred_element_type=F32)
        o = (o * jnp.concatenate(inv, axis=0)).astype(BF16)
        if stacked:
            o_ref[:, sl] = o
        else:
            for gi in range(group):
                h = j * group + gi
                o_ref[:, h * hd:(h + 1) * hd] = o[gi * rph:(gi + 1) * rph]


def _swa(q_arr, kc_arr, vc_arr, kp_arr, vp_arr, maps, n_blocks, q_block, rows_per_head, stacked,
         win, qg, kg, sinks, n_kv, group, hd, norm_prev, first_has_prev, blocks_per_seq):
    q_map, kc_map, vc_map, kp_map, vp_map = maps
    return pl.pallas_call(
        functools.partial(_swa_kernel, n_kv=n_kv, group=group, hd=hd, rows_per_head=rows_per_head,
                          stacked=stacked, norm_prev=norm_prev, first_has_prev=first_has_prev,
                          blocks_per_seq=blocks_per_seq),
        grid=(n_blocks,),
        in_specs=[pl.BlockSpec(memory_space=pltpu.SMEM),
                  pl.BlockSpec(q_block, q_map),
                  pl.BlockSpec((win, n_kv * hd), kc_map),
                  pl.BlockSpec((win, n_kv * hd), vc_map),
                  pl.BlockSpec((win, n_kv * hd), kp_map),
                  pl.BlockSpec((win, n_kv * hd), vp_map),
                  pl.BlockSpec((1, hd), lambda i: (0, 0)),
                  pl.BlockSpec((1, hd), lambda i: (0, 0))],
        out_specs=[pl.BlockSpec(q_block, lambda i: (i, 0)),
                   pl.BlockSpec((win, n_kv * hd), lambda i: (i, 0))],
        out_shape=[jax.ShapeDtypeStruct((n_blocks * q_block[0], q_block[1]), BF16),
                   jax.ShapeDtypeStruct((n_blocks * win, n_kv * hd), F32)],
        compiler_params=_cparams(1), name="swa_stacked%d" % int(stacked),
    )(sinks, q_arr, kc_arr, vc_arr, kp_arr, vp_arr, qg.reshape(1, hd), kg.reshape(1, hd))


def _moe_up_kernel(te_ref, nu_ref, xs_ref, wg_ref, wu_ref, rw_ref, hid_ref, wgb_ref, wub_ref):
    i = pl.program_id(0)
    changed = jnp.logical_or(i == 0, te_ref[i] != te_ref[jnp.maximum(i - 1, 0)])

    @pl.when(changed)
    def _():
        wgb_ref[...] = wg_ref[...].astype(BF16)
        wub_ref[...] = wu_ref[...].astype(BF16)

    @pl.when(i < nu_ref[0])
    def _():
        x = xs_ref[...]
        g = jnp.dot(x, wgb_ref[...], preferred_element_type=F32)
        u = jnp.dot(x, wub_ref[...], preferred_element_type=F32)
        hid_ref[...] = ((g / (1.0 + jnp.exp(-g))) * u * rw_ref[...]).astype(BF16)

    @pl.when(i >= nu_ref[0])
    def _():
        hid_ref[...] = jnp.zeros_like(hid_ref)


def _moe_down_kernel(te_ref, nu_ref, hid_ref, wd_ref, out_ref, wdb_ref):
    i = pl.program_id(0)
    changed = jnp.logical_or(i == 0, te_ref[i] != te_ref[jnp.maximum(i - 1, 0)])

    @pl.when(changed)
    def _():
        wdb_ref[...] = wd_ref[...].astype(BF16)

    @pl.when(i < nu_ref[0])
    def _():
        out_ref[...] = jnp.dot(hid_ref[...], wdb_ref[...], preferred_element_type=F32)

    @pl.when(i >= nu_ref[0])
    def _():
        out_ref[...] = jnp.zeros_like(out_ref)


def _moe(x, h, logits, b_group, b_expert, w_gate, w_up, w_down, layer):
    m, d = h.shape
    n_groups = b_group.shape[0]
    n_experts = b_expert.shape[0]
    per_group = n_experts // n_groups
    f = w_gate.shape[2] // n_experts
    top_k = 2

    g_logit = logits[:, :n_groups] + b_group
    g_prob = jax.nn.softmax(g_logit, axis=-1)
    g_idx = jnp.argmax(g_logit, axis=-1)
    g_gate = jnp.take_along_axis(g_prob, g_idx[:, None], axis=-1)
    e_logit = (logits[:, n_groups:n_groups + n_experts] + b_expert).reshape(m, n_groups, per_group)
    e_in = jnp.take_along_axis(e_logit, g_idx[:, None, None], axis=1)[:, 0]
    i1 = jnp.argmax(e_in, axis=-1)
    rest = jnp.where(jnp.arange(per_group)[None, :] == i1[:, None], -jnp.inf, e_in)
    i2 = jnp.argmax(rest, axis=-1)
    top_v = jnp.stack([jnp.max(e_in, axis=-1), jnp.max(rest, axis=-1)], axis=-1)
    top_i = jnp.stack([i1, i2], axis=-1)
    top_w = jax.nn.softmax(top_v, axis=-1) * g_gate
    eid = (g_idx[:, None] * per_group + top_i).astype(jnp.int32)

    n_pairs = m * top_k
    n_tiles = (n_pairs + n_experts * (MOE_TM - 1)) // MOE_TM + 1
    n_rows = n_tiles * MOE_TM
    flat_e = eid.reshape(-1)
    order = jnp.argsort(flat_e, stable=True).astype(jnp.int32)
    rank = jnp.argsort(order).astype(jnp.int32)
    counts = jnp.sum(flat_e[:, None] == jnp.arange(n_experts, dtype=jnp.int32)[None, :], axis=0,
                     dtype=jnp.int32)
    padded = ((counts + MOE_TM - 1) // MOE_TM) * MOE_TM
    pad_end = jnp.cumsum(padded)
    pad_start = pad_end - padded
    start = jnp.cumsum(counts) - counts
    n_used = (pad_end[-1] // MOE_TM).astype(jnp.int32)
    tile_row = jnp.minimum(jnp.arange(n_tiles, dtype=jnp.int32), n_used - 1) * MOE_TM
    tile_e = jnp.minimum(jnp.sum(pad_end[None, :] <= tile_row[:, None], axis=1, dtype=jnp.int32),
                         n_experts - 1)
    row = jnp.arange(n_rows, dtype=jnp.int32)
    row_e = jnp.repeat(tile_e, MOE_TM)
    in_group = row - pad_start[row_e]
    live = (in_group < counts[row_e]) & (row < n_used * MOE_TM)
    pair = order[jnp.clip(start[row_e] + in_group, 0, n_pairs - 1)]
    row_token = jnp.where(live, pair // top_k, 0)
    row_w = jnp.where(live, top_w.reshape(-1)[pair], 0.0)
    pos = (pad_start[flat_e] + rank - start[flat_e]).reshape(m, top_k)
    n_used = n_used.reshape(1)

    xs = h[row_token]
    hid = pl.pallas_call(
        _moe_up_kernel,
        grid_spec=pltpu.PrefetchScalarGridSpec(
            num_scalar_prefetch=2, grid=(n_tiles,),
            in_specs=[pl.BlockSpec((MOE_TM, d), lambda i, te, nu: (i, 0)),
                      pl.BlockSpec((None, d, f), lambda i, te, nu: (layer, 0, te[i])),
                      pl.BlockSpec((None, d, f), lambda i, te, nu: (layer, 0, te[i])),
                      pl.BlockSpec((MOE_TM, 1), lambda i, te, nu: (i, 0))],
            out_specs=pl.BlockSpec((MOE_TM, f), lambda i, te, nu: (i, 0)),
            scratch_shapes=[pltpu.VMEM((d, f), BF16), pltpu.VMEM((d, f), BF16)]),
        out_shape=jax.ShapeDtypeStruct((n_rows, f), BF16),
        compiler_params=_cparams(1), name="moe_up",
    )(tile_e, n_used, xs, w_gate, w_up, row_w.reshape(n_rows, 1))
    rows = pl.pallas_call(
        _moe_down_kernel,
        grid_spec=pltpu.PrefetchScalarGridSpec(
            num_scalar_prefetch=2, grid=(n_tiles,),
            in_specs=[pl.BlockSpec((MOE_TM, f), lambda i, te, nu: (i, 0)),
                      pl.BlockSpec((None, f, d), lambda i, te, nu: (layer, te[i], 0))],
            out_specs=pl.BlockSpec((MOE_TM, d), lambda i, te, nu: (i, 0)),
            scratch_shapes=[pltpu.VMEM((f, d), BF16)]),
        out_shape=jax.ShapeDtypeStruct((n_rows, d), F32),
        compiler_params=_cparams(1), name="moe_down",
    )(tile_e, n_used, hid, w_down)
    return x + rows[pos[:, 0]] + rows[pos[:, 1]]


def _pairs_major(a, n_batch, t_len, n_heads, n):
    return a.reshape(n_batch, t_len, n_heads, n).transpose(1, 3, 0, 2).reshape(
        t_len, n, n_batch * n_heads)


def _token_major(a, n_batch, t_len, n_heads, n):
    return a.reshape(t_len, n, n_batch, n_heads).transpose(2, 0, 3, 1).reshape(
        n_batch * t_len, n_heads * n)


def _param_pairs(a, n_batch, n_heads, n):
    return jnp.tile(a.reshape(n_heads, n).T, (1, n_batch))


def _even_layer(x, h, i, groups, state_ret, state_rwkv, state_shift, wts):
    (a_w_in, a_w_out, ret_norm_g, rwkv_mu, rwkv_w0, rwkv_w2, rwkv_a0, rwkv_a2, rwkv_g2,
     rwkv_kk_scale, rwkv_ka, rwkv_rk, rwkv_lnx_g, rwkv_lnx_b) = [w[i] for w in wts]
    ret_heads, ret_dk = state_ret.shape[2], state_ret.shape[3]
    rw_heads, rw_n = state_rwkv.shape[2], state_rwkv.shape[3]
    ret_w = ret_heads * ret_dk
    rw_w = rw_heads * rw_n
    lora_w, lora_a, lora_g = rwkv_w2.shape[0], rwkv_a2.shape[0], rwkv_g2.shape[0]
    assert lora_w == LANE and lora_a == LANE and lora_g <= LANE and ret_w == rw_w
    n_main = 4 * ret_w + 3 * rw_w
    n_shift = 3 * rw_w + lora_w + lora_a + lora_g
    (pb, pt), (sb, st) = groups
    m_p, m_s = pb * pt, sb * st

    p = _matmul([h], a_w_in, n_main, name="a_in")
    tail_cols = 3 * LANE
    w_tail = jnp.pad(a_w_in[:, n_main:], ((0, 0), (0, tail_cols - (lora_w + lora_a + lora_g))))
    tail = _matmul([h], w_tail, tail_cols, tn=tail_cols, name="a_in_tail")

    o_ret_p, ret_p = _retention_prompt(p, ret_norm_g, pb, pt, ret_heads, ret_dk)
    o_ret_s, ret_s = _retention_sample(p, ret_norm_g, state_ret[i], m_p, sb, st, ret_heads, ret_dk)

    pad_g = lambda a: jnp.pad(a, ((0, LANE - a.shape[0]), (0, 0)))
    mu = rwkv_mu
    mu_tail = jnp.pad(mu[3 * rw_w:], (0, tail_cols - (n_shift - 3 * rw_w)))
    prep_params = [mu[:rw_w].reshape(1, -1), mu[rw_w:2 * rw_w].reshape(1, -1),
                   mu[2 * rw_w:3 * rw_w].reshape(1, -1), mu_tail.reshape(1, -1),
                   rwkv_w0.reshape(1, -1), rwkv_w2, rwkv_a0.reshape(1, -1), rwkv_a2,
                   pad_g(rwkv_g2)]
    cb0 = (4 * ret_w) // rw_w
    shift0 = state_shift[i]
    sh_main = jnp.repeat(shift0[:, :3 * rw_w], st, axis=0)
    sh_tail = jnp.repeat(jnp.pad(shift0[:, 3 * rw_w:], ((0, 0), (0, tail_cols - (n_shift - 3 * rw_w)))),
                         st, axis=0)
    seq_p = _rwkv_prep(p, tail, 0, m_p, cb0, rw_w, prep_params, None, pt, st)
    seq_s = _rwkv_prep(p, tail, m_p, m_s, cb0, rw_w, prep_params, (sh_main, sh_tail), pt, st)

    outs = []
    for seqs, nb, nt, s0 in ((seq_p, pb, pt, None), (seq_s, sb, st, state_rwkv[i])):
        scan_params = [_param_pairs(a, nb, rw_heads, rw_n)
                       for a in (rwkv_kk_scale, rwkv_ka, rwkv_rk, rwkv_lnx_g, rwkv_lnx_b)]
        if s0 is None:
            s0_l = jnp.zeros((rw_n, rw_n, nb * rw_heads), F32)
        else:
            s0_l = s0.transpose(2, 3, 0, 1).reshape(rw_n, rw_n, nb * rw_heads)
        y, s_t = _rwkv_scan([_pairs_major(a, nb, nt, rw_heads, rw_n) for a in seqs], scan_params, s0_l)
        y = _token_major(y, nb, nt, rw_heads, rw_n).astype(BF16)
        s_t = s_t.reshape(rw_n, rw_n, nb, rw_heads).transpose(2, 3, 0, 1)
        outs.append((y, s_t))
    (y_p, rwkv_p), (y_s, rwkv_s) = outs

    o_ret = jnp.concatenate([o_ret_p, o_ret_s], axis=0)
    y = jnp.concatenate([y_p, y_s], axis=0)
    x = _matmul([o_ret, y], a_w_out, a_w_out.shape[1], res=x, name="a_out")

    def last_ps(row0, nb, nt):
        rows = row0 + jnp.arange(nb) * nt + nt - 1
        return jnp.concatenate([p[rows, 4 * ret_w:], tail[rows, :n_shift - 3 * rw_w]], axis=-1)

    return x, (ret_p, rwkv_p, last_ps(0, pb, pt)), (ret_s, rwkv_s, last_ps(m_p, sb, st))


def _odd_layer(x, h, i, groups, cache_k, cache_v, wts):
    c_w_in, c_w_out, c_q_norm_g, c_k_norm_g, c_sinks = [w[i] for w in wts]
    win, n_kv, hd = cache_k.shape[2], cache_k.shape[3], cache_k.shape[4]
    n_heads = c_sinks.shape[0]
    group = n_heads // n_kv
    mix_c = n_heads * hd
    kv_w = n_kv * hd
    (pb, pt), (sb, st) = groups
    m_p = pb * pt
    assert pt % win == 0 and mix_c % kv_w == 0
    p = _matmul([h], c_w_in, mix_c + 2 * kv_w, name="c_in")
    kcb, vcb = mix_c // kv_w, mix_c // kv_w + 1

    nb_p = m_p // win
    maps = (lambda n: (n, 0), lambda n: (n, kcb), lambda n: (n, vcb),
            lambda n: (jnp.maximum(n - 1, 0), kcb), lambda n: (jnp.maximum(n - 1, 0), vcb))
    o_p, kn_p = _swa(p, p, p, p, p, maps, nb_p, (win, mix_c), win, False, win, c_q_norm_g,
                     c_k_norm_g, c_sinks, n_kv, group, hd, True, False, pt // win)

    tq = 16
    ps = p[m_p:].reshape(sb, st, -1)
    q_s = jnp.pad(ps[:, :, :mix_c].reshape(sb, st, n_kv, group, hd),
                  ((0, 0), (0, tq - st), (0, 0), (0, 0), (0, 0)))
    q_s = q_s.transpose(0, 3, 1, 2, 4).reshape(sb * group * tq, kv_w)
    kv_pad = lambda a: jnp.pad(a, ((0, 0), (0, win - st), (0, 0))).reshape(sb * win, kv_w)
    k_s = kv_pad(ps[:, :, mix_c:mix_c + kv_w])
    v_s = kv_pad(ps[:, :, mix_c + kv_w:])
    kc0 = cache_k[i].reshape(sb * win, kv_w)
    vc0 = cache_v[i].reshape(sb * win, kv_w)
    same = lambda n: (n, 0)
    o_s, kn_s = _swa(q_s, k_s, v_s, kc0, vc0, (same,) * 5, sb, (group * tq, kv_w), tq, True, win,
                     c_q_norm_g, c_k_norm_g, c_sinks, n_kv, group, hd, False, True, 1)
    o_s = o_s.reshape(sb, group, tq, n_kv, hd)[:, :, :st].transpose(0, 2, 3, 1, 4).reshape(
        sb * st, mix_c)

    x = _matmul([jnp.concatenate([o_p, o_s], axis=0)], c_w_out, c_w_out.shape[1], res=x, name="c_out")

    new_k_p = kn_p.reshape(pb, pt, n_kv, hd)[:, -win:]
    new_v_p = p[:m_p, mix_c + kv_w:].reshape(pb, pt, n_kv, hd)[:, -win:]
    kn_s = kn_s.reshape(sb, win, n_kv, hd)[:, :st]
    new_k_s = jnp.concatenate([cache_k[i], kn_s], axis=1)[:, -win:]
    new_v_s = jnp.concatenate([cache_v[i], ps[:, :, mix_c + kv_w:].reshape(sb, st, n_kv, hd)],
                              axis=1)[:, -win:]
    return x, (new_k_p, new_v_p), (new_k_s, new_v_s)


def kernel(x_prompt, x_sample, state_ret, state_rwkv, state_rwkv_shift, cache_swa_k, cache_swa_v, norm_mix_g, norm_ffn_g, a_w_in, a_w_out, ret_norm_g, rwkv_mu, rwkv_w0, rwkv_w2, rwkv_a0, rwkv_a2, rwkv_g2, rwkv_kk_scale, rwkv_ka, rwkv_rk, rwkv_lnx_g, rwkv_lnx_b, c_w_in, c_w_out, c_q_norm_g, c_k_norm_g, c_sinks, moe_w_group, moe_b_group, moe_w_expert, moe_b_expert, moe_w_gate, moe_w_up, moe_w_down):
    pb, pt, d = x_prompt.shape
    sb, st, _ = x_sample.shape
    groups = ((pb, pt), (sb, st))
    depth = norm_mix_g.shape[0]
    a_wts = (a_w_in, a_w_out, ret_norm_g, rwkv_mu, rwkv_w0, rwkv_w2, rwkv_a0, rwkv_a2, rwkv_g2,
             rwkv_kk_scale, rwkv_ka, rwkv_rk, rwkv_lnx_g, rwkv_lnx_b)
    c_wts = (c_w_in, c_w_out, c_q_norm_g, c_k_norm_g, c_sinks)
    st_p = [[] for _ in range(5)]
    st_s = [[] for _ in range(5)]
    for l in range(depth):
        i = l // 2
        if l == 0:
            h, x = _rms_norm_first(x_prompt.reshape(pb * pt, d), x_sample.reshape(sb * st, d),
                                   norm_mix_g[l])
        else:
            h = _rms_norm(x, norm_mix_g[l])
        if l % 2 == 0:
            x, sp, ss = _even_layer(x, h, i, groups, state_ret, state_rwkv, state_rwkv_shift, a_wts)
            for j in range(3):
                st_p[j].append(sp[j])
                st_s[j].append(ss[j])
        else:
            x, sp, ss = _odd_layer(x, h, i, groups, cache_swa_k, cache_swa_v, c_wts)
            for j in range(2):
                st_p[3 + j].append(sp[j])
                st_s[3 + j].append(ss[j])
        n_router = moe_w_group.shape[2] + moe_w_expert.shape[2]
        w_router = jnp.pad(jnp.concatenate([moe_w_group[l], moe_w_expert[l]], axis=1),
                           ((0, 0), (0, LANE - n_router)))
        h, logits = _rms_norm(x, norm_ffn_g[l], w_router)
        x = _moe(x, h, logits, moe_b_group[l], moe_b_expert[l], moe_w_gate, moe_w_up, moe_w_down, l)
    m_p = pb * pt
    y_prompt = x[:m_p].reshape(pb, pt, d)
    y_sample = x[m_p:].reshape(sb, st, d)
    return (y_prompt, y_sample) + tuple(jnp.stack(s) for s in st_p) + tuple(jnp.stack(s) for s in st_s)
```

```python
import functools

import jax
import jax.numpy as jnp
from jax import lax
from jax.experimental import pallas as pl
from jax.experimental.pallas import tpu as pltpu

F32 = jnp.float32
BF16 = jnp.bfloat16

RMS_EPS = 1e-6
RWKV_LN_EPS = 64e-5
RET_CHUNK = 128

LANE = 128
VMEM_LIMIT = 56 * 1024 * 1024

MM_TM = 640
MM_TN = 512
NORM_TM = 320
MOE_TM = 256
SCAN_TC = 32
RET_HEADS_PER_STEP = 4


def _cparams(n_axes):
    return pltpu.CompilerParams(dimension_semantics=("arbitrary",) * n_axes,
                                vmem_limit_bytes=VMEM_LIMIT)


def _norm_kernel(x_ref, g_ref, h_ref):
    x = x_ref[...]
    y = x * lax.rsqrt(jnp.mean(x * x, axis=-1, keepdims=True) + RMS_EPS) * g_ref[...]
    h_ref[...] = y.astype(BF16)


def _norm_router_kernel(x_ref, g_ref, wr_ref, h_ref, logit_ref):
    x = x_ref[...]
    y = x * lax.rsqrt(jnp.mean(x * x, axis=-1, keepdims=True) + RMS_EPS) * g_ref[...]
    h_ref[...] = y.astype(BF16)
    logit_ref[...] = jnp.dot(y, wr_ref[...], precision=lax.Precision.HIGHEST,
                             preferred_element_type=F32)


def _norm_first_kernel(xp_ref, xs_ref, g_ref, h_ref, x_ref, *, n_prompt_tiles):
    x = jnp.where(pl.program_id(0) < n_prompt_tiles, xp_ref[...], xs_ref[...])
    x_ref[...] = x
    y = x * lax.rsqrt(jnp.mean(x * x, axis=-1, keepdims=True) + RMS_EPS) * g_ref[...]
    h_ref[...] = y.astype(BF16)


def _rms_norm_first(xp, xs, g):
    (m_p, d), m_s = xp.shape, xs.shape[0]
    tm = LANE
    assert m_p % tm == 0 and m_s % tm == 0
    npt = m_p // tm
    spec = pl.BlockSpec((tm, d), lambda i: (i, 0))
    return pl.pallas_call(
        functools.partial(_norm_first_kernel, n_prompt_tiles=npt), grid=((m_p + m_s) // tm,),
        in_specs=[pl.BlockSpec((tm, d), lambda i: (jnp.minimum(i, npt - 1), 0)),
                  pl.BlockSpec((tm, d), lambda i: (jnp.maximum(i - npt, 0), 0)),
                  pl.BlockSpec((1, d), lambda i: (0, 0))],
        out_specs=[spec, spec],
        out_shape=[jax.ShapeDtypeStruct((m_p + m_s, d), BF16),
                   jax.ShapeDtypeStruct((m_p + m_s, d), F32)],
        compiler_params=_cparams(1), name="rms_norm_first")(xp, xs, g.reshape(1, d))


def _rms_norm(x, g, w_router=None):
    m, d = x.shape
    tm = NORM_TM if m % NORM_TM == 0 else m
    grid = (m // tm,)
    x_spec = pl.BlockSpec((tm, d), lambda i: (i, 0))
    g_spec = pl.BlockSpec((1, d), lambda i: (0, 0))
    if w_router is None:
        return pl.pallas_call(
            _norm_kernel, grid=grid, in_specs=[x_spec, g_spec], out_specs=x_spec,
            out_shape=jax.ShapeDtypeStruct((m, d), BF16), compiler_params=_cparams(1),
            name="rms_norm")(x, g.reshape(1, d))
    nr = w_router.shape[1]
    return pl.pallas_call(
        _norm_router_kernel, grid=grid,
        in_specs=[x_spec, g_spec, pl.BlockSpec((d, nr), lambda i: (0, 0))],
        out_specs=[x_spec, pl.BlockSpec((tm, nr), lambda i: (i, 0))],
        out_shape=[jax.ShapeDtypeStruct((m, d), BF16), jax.ShapeDtypeStruct((m, nr), F32)],
        compiler_params=_cparams(1), name="rms_norm_router")(x, g.reshape(1, d), w_router)


def _mm_kernel(*refs, n_parts, has_res):
    x_refs = refs[:n_parts]
    w_refs = refs[n_parts:2 * n_parts]
    pos = 2 * n_parts
    res_ref = refs[pos] if has_res else None
    pos += int(has_res)
    o_ref = refs[pos]
    wb_refs = refs[pos + 1:]

    @pl.when(pl.program_id(1) == 0)
    def _():
        for w_ref, wb_ref in zip(w_refs, wb_refs):
            wb_ref[...] = w_ref[...].astype(BF16)

    acc = None
    for x_ref, wb_ref in zip(x_refs, wb_refs):
        d = jnp.dot(x_ref[...], wb_ref[...], preferred_element_type=F32)
        acc = d if acc is None else acc + d
    if has_res:
        acc = acc + res_ref[...]
    o_ref[...] = acc


def _matmul(xs, w, n_out, col_block0=0, tn=MM_TN, res=None, name="proj"):
    m = xs[0].shape[0]
    tm = MM_TM if m % MM_TM == 0 else m
    assert n_out % tn == 0
    grid = (n_out // tn, m // tm)
    in_specs, w_specs, scratch = [], [], []
    row = 0
    for x in xs:
        kp = x.shape[1]
        in_specs.append(pl.BlockSpec((tm, kp), lambda j, i: (i, 0)))
        assert row % kp == 0
        rb = row // kp
        w_specs.append(pl.BlockSpec((kp, tn), lambda j, i, rb=rb: (rb, j + col_block0)))
        scratch.append(pltpu.VMEM((kp, tn), BF16))
        row += kp
    assert row == w.shape[0]
    args = list(xs) + [w] * len(xs)
    in_specs = in_specs + w_specs
    if res is not None:
        in_specs.append(pl.BlockSpec((tm, tn), lambda j, i: (i, j)))
        args.append(res)
    return pl.pallas_call(
        functools.partial(_mm_kernel, n_parts=len(xs), has_res=res is not None),
        grid=grid, in_specs=in_specs,
        out_specs=pl.BlockSpec((tm, tn), lambda j, i: (i, j)),
        out_shape=jax.ShapeDtypeStruct((m, n_out), F32),
        scratch_shapes=scratch, compiler_params=_cparams(2), name=name)(*args)


def _ret_finish(o, g, gain):
    o = o * lax.rsqrt(jnp.mean(o * o, axis=-1, keepdims=True) + RMS_EPS) * gain
    return ((g / (1.0 + jnp.exp(-g))) * o).astype(BF16)


def _ret_prompt_kernel(q_ref, k_ref, v_ref, g_ref, gain_ref, lg_ref, o_ref, s_ref, *, dk):
    c = pl.program_id(2)
    chunk = q_ref.shape[0]

    @pl.when(c == 0)
    def _():
        s_ref[...] = jnp.zeros_like(s_ref)

    row = lax.broadcasted_iota(jnp.int32, (chunk, chunk), 0)
    col = lax.broadcasted_iota(jnp.int32, (chunk, chunk), 1)
    diff = (row - col).astype(F32)
    causal = diff >= 0
    dist = jnp.where(causal, diff, 0.0)
    pos = lax.broadcasted_iota(jnp.int32, (chunk, 1), 0).astype(F32)

    for hh in range(s_ref.shape[0]):
        sl = slice(hh * dk, (hh + 1) * dk)
        lg = lg_ref[hh][:, :1]
        decay_mask = jnp.where(causal, jnp.exp(dist * lg), 0.0)
        q_decay = jnp.exp((pos + 1.0) * lg)
        k_decay = jnp.exp((chunk - 1.0 - pos) * lg)
        chunk_decay = jnp.exp(chunk * lg)
        q = q_ref[:, sl]
        k = k_ref[:, sl] * (dk ** -0.5)
        vb = v_ref[:, sl].astype(BF16)
        s = s_ref[hh]
        scores = lax.dot_general(q.astype(BF16), k.astype(BF16), (((1,), (1,)), ((), ())),
                                 preferred_element_type=F32) * decay_mask
        inner = jnp.dot(scores.astype(BF16), vb, preferred_element_type=F32)
        cross = jnp.dot((q * q_decay).astype(BF16), s.astype(BF16), preferred_element_type=F32)
        s_ref[hh] = chunk_decay * s + lax.dot_general(
            (k * k_decay).astype(BF16), vb, (((0,), (0,)), ((), ())), preferred_element_type=F32)
        o_ref[:, sl] = _ret_finish(inner + cross, g_ref[:, sl], gain_ref[hh])


def _ret_sample_kernel(q_ref, k_ref, v_ref, g_ref, gain_ref, lg_ref, s0_ref, o_ref, s_ref, acc_ref,
                       *, dk, t_len):
    b = pl.program_id(1)
    rows = q_ref.shape[0]
    lg = lg_ref[:, :1]
    r_b = lax.broadcasted_iota(jnp.int32, (rows, 1), 0) // t_len
    r_t = (lax.broadcasted_iota(jnp.int32, (rows, 1), 0) % t_len).astype(F32)
    q = q_ref[...]
    k = k_ref[...] * (dk ** -0.5)
    vb = v_ref[...].astype(BF16)

    @pl.when(b == 0)
    def _():
        row = lax.broadcasted_iota(jnp.int32, (rows, rows), 0)
        col = lax.broadcasted_iota(jnp.int32, (rows, rows), 1)
        diff = (row % t_len - col % t_len).astype(F32)
        ok = (row // t_len == col // t_len) & (diff >= 0)
        mask = jnp.where(ok, jnp.exp(jnp.where(ok, diff, 0.0) * lg), 0.0)
        scores = lax.dot_general(q.astype(BF16), k.astype(BF16), (((1,), (1,)), ((), ())),
                                 preferred_element_type=F32) * mask
        acc_ref[...] = jnp.dot(scores.astype(BF16), vb, preferred_element_type=F32)

    sel = r_b == b
    s0 = s0_ref[...]
    cross = jnp.dot((q * jnp.exp((r_t + 1.0) * lg)).astype(BF16), s0.astype(BF16),
                    preferred_element_type=F32)
    acc_ref[...] += jnp.where(sel, cross, 0.0)
    k_decay = jnp.where(sel, jnp.exp((t_len - 1.0 - r_t) * lg), 0.0)
    s_ref[...] = jnp.exp(t_len * lg) * s0 + lax.dot_general(
        (k * k_decay).astype(BF16), vb, (((0,), (0,)), ((), ())), preferred_element_type=F32)

    @pl.when(b == pl.num_programs(1) - 1)
    def _():
        o_ref[...] = _ret_finish(acc_ref[...], g_ref[...], gain_ref[...])


def _ret_log_decay(n_heads):
    lg = jnp.log(1.0 - 2.0 ** (-5.0 - jnp.arange(n_heads, dtype=F32)))
    return jnp.broadcast_to(lg[:, None, None], (n_heads, 1, LANE))


def _retention_prompt(p, gain, n_batch, seq, n_heads, dk):
    nc = seq // RET_CHUNK
    hps = RET_HEADS_PER_STEP
    assert n_heads % hps == 0
    ng = n_heads // hps
    col = lambda grp: (lambda b, h, c: (b * nc + c, grp * ng + h))
    blk = lambda grp: pl.BlockSpec((RET_CHUNK, hps * dk), col(grp))
    return pl.pallas_call(
        functools.partial(_ret_prompt_kernel, dk=dk),
        grid=(n_batch, ng, nc),
        in_specs=[blk(0), blk(1), blk(2), blk(3),
                  pl.BlockSpec((hps, 1, dk), lambda b, h, c: (h, 0, 0)),
                  pl.BlockSpec((hps, 1, LANE), lambda b, h, c: (h, 0, 0))],
        out_specs=[pl.BlockSpec((RET_CHUNK, hps * dk), lambda b, h, c: (b * nc + c, h)),
                   pl.BlockSpec((None, hps, dk, dk), lambda b, h, c: (b, h, 0, 0))],
        out_shape=[jax.ShapeDtypeStruct((n_batch * seq, n_heads * dk), BF16),
                   jax.ShapeDtypeStruct((n_batch, n_heads, dk, dk), F32)],
        compiler_params=_cparams(3), name="retention_prompt",
    )(p, p, p, p, gain.reshape(n_heads, 1, dk), _ret_log_decay(n_heads))


def _retention_sample(p, gain, s0, row0, n_batch, t_len, n_heads, dk):
    rows = n_batch * t_len
    rb = row0 // rows
    blk = lambda grp: pl.BlockSpec((rows, dk), lambda h, b, grp=grp: (rb, grp * n_heads + h))
    return pl.pallas_call(
        functools.partial(_ret_sample_kernel, dk=dk, t_len=t_len),
        grid=(n_heads, n_batch),
        in_specs=[blk(0), blk(1), blk(2), blk(3),
                  pl.BlockSpec((None, 1, dk), lambda h, b: (h, 0, 0)),
                  pl.BlockSpec((None, 1, LANE), lambda h, b: (h, 0, 0)),
                  pl.BlockSpec((None, None, dk, dk), lambda h, b: (b, h, 0, 0))],
        out_specs=[pl.BlockSpec((rows, dk), lambda h, b: (0, h)),
                   pl.BlockSpec((None, None, dk, dk), lambda h, b: (b, h, 0, 0))],
        out_shape=[jax.ShapeDtypeStruct((rows, n_heads * dk), BF16),
                   jax.ShapeDtypeStruct((n_batch, n_heads, dk, dk), F32)],
        scratch_shapes=[pltpu.VMEM((rows, dk), F32)],
        compiler_params=_cparams(2), name="retention_sample",
    )(p, p, p, p, gain.reshape(n_heads, 1, dk), _ret_log_decay(n_heads), s0)


def _sigmoid(x):
    return 1.0 / (1.0 + jnp.exp(-x))


def _rwkv_prep_kernel(*refs, sample, seq, t_len):
    if sample:
        (r_c, k_c, v_c, t_c, r_s, k_s, v_s, t_s,
         mu_r, mu_k, mu_v, mu_t, w0, w2, a0, a2, g2,
         r_o, w_o, k_o, v_o, a_o, g_o) = refs
    else:
        (r_c, k_c, v_c, t_c, r_p, k_p, v_p, t_p,
         mu_r, mu_k, mu_v, mu_t, w0, w2, a0, a2, g2,
         r_o, w_o, k_o, v_o, a_o, g_o) = refs
    tm = r_c.shape[0]
    row = lax.broadcasted_iota(jnp.int32, (tm, 1), 0)
    seq_start = (pl.program_id(0) * tm) % seq == 0

    def mix(cur_ref, other_ref, mu_ref):
        cur = cur_ref[...]
        rolled = pltpu.roll(cur, 1, 0)
        if sample:
            prev = jnp.where(row % t_len == 0, other_ref[...], rolled)
        else:
            last = other_ref[7:8, :]
            first = jnp.where(seq_start, jnp.zeros_like(last), last)
            prev = jnp.where(row == 0, first, rolled)
        return cur + (prev - cur) * mu_ref[...]

    other = (r_s, k_s, v_s, t_s) if sample else (r_p, k_p, v_p, t_p)
    r_o[...] = mix(r_c, other[0], mu_r)
    k_o[...] = mix(k_c, other[1], mu_k)
    v_o[...] = mix(v_c, other[2], mu_v)
    tail = mix(t_c, other[3], mu_t)
    lw, la, lgt = tail[:, :LANE], tail[:, LANE:2 * LANE], tail[:, 2 * LANE:]

    z = -(w0[...] + jnp.dot(jnp.tanh(lw).astype(BF16), w2[...].astype(BF16),
                            preferred_element_type=F32))
    softplus = jnp.maximum(z, 0.0) + jnp.log1p(jnp.exp(-jnp.abs(z)))
    w_o[...] = jnp.exp(-jnp.exp(-softplus - 0.5))
    a_o[...] = _sigmoid(a0[...] + jnp.dot(la.astype(BF16), a2[...].astype(BF16),
                                          preferred_element_type=F32))
    g_o[...] = jnp.dot(_sigmoid(lgt).astype(BF16), g2[...].astype(BF16),
                       preferred_element_type=F32)


def _rwkv_prep(p, tail, row0, n_rows, col_block0, width, params, shift_rows, seq, t_len):
    sample = shift_rows is not None
    tm = n_rows if sample else 256
    rb0 = row0 // tm
    grid = (n_rows // tm,)
    cur = lambda grp: pl.BlockSpec((tm, width), lambda i, grp=grp: (rb0 + i, col_block0 + grp))
    tail_w = tail.shape[1]
    in_specs = [cur(0), cur(1), cur(2), pl.BlockSpec((tm, tail_w), lambda i: (rb0 + i, 0))]
    args = [p, p, p, tail]
    if sample:
        sh_main, sh_tail = shift_rows
        in_specs += [pl.BlockSpec((tm, width), lambda i, grp=grp: (0, grp)) for grp in range(3)]
        in_specs += [pl.BlockSpec((tm, tail_w), lambda i: (0, 0))]
        args += [sh_main, sh_main, sh_main, sh_tail]
    else:
        per = tm // 8
        prev_rb = lambda i: jnp.maximum((rb0 + i) * per - 1, 0)
        in_specs += [pl.BlockSpec((8, width), lambda i, grp=grp: (prev_rb(i), col_block0 + grp))
                     for grp in range(3)]
        in_specs += [pl.BlockSpec((8, tail_w), lambda i: (prev_rb(i), 0))]
        args += [p, p, p, tail]
    full = lambda a: pl.BlockSpec(a.shape, lambda i: (0,) * a.ndim)
    in_specs += [full(a) for a in params]
    args += list(params)
    out_spec = pl.BlockSpec((tm, width), lambda i: (i, 0))
    return pl.pallas_call(
        functools.partial(_rwkv_prep_kernel, sample=sample, seq=seq, t_len=t_len),
        grid=grid, in_specs=in_specs, out_specs=[out_spec] * 6,
        out_shape=[jax.ShapeDtypeStruct((n_rows, width), F32)] * 6,
        compiler_params=_cparams(1), name="rwkv_prep_sample" if sample else "rwkv_prep_prompt",
    )(*args)


def _rwkv_scan_kernel(r_ref, w_ref, k_ref, v_ref, a_ref, g_ref,
                      kks_ref, ka_ref, rk_ref, lng_ref, lnb_ref, s0_ref,
                      y_ref, s_ref, ys_ref):
    n_v = s_ref.shape[0]
    nb, _, nh, n = r_ref.shape

    @pl.when(pl.program_id(1) == 0)
    def _():
        s_ref[...] = s0_ref[...]

    def load(ref, t):
        return ref[:, t].reshape(nb * nh, n).T

    def step(t, carry):
        r = load(r_ref, t)
        w = load(w_ref, t)
        kr = load(k_ref, t)
        v = load(v_ref, t)
        a = load(a_ref, t)
        kks = kr * kks_ref[...]
        kk = kks / jnp.maximum(jnp.sqrt(jnp.sum(kks * kks, axis=0, keepdims=True)), 1e-12)
        kf = kr * (1.0 + (a - 1.0) * ka_ref[...])
        kka = kk * a
        for vi in range(n_v):
            s = s_ref[vi]
            sa = jnp.sum(s * kk, axis=0, keepdims=True)
            s = s * w - sa * kka + v[vi:vi + 1, :] * kf
            s_ref[vi] = s
            ys_ref[vi:vi + 1, :] = jnp.sum(s * r, axis=0, keepdims=True)
        y = ys_ref[...]
        mu = jnp.mean(y, axis=0, keepdims=True)
        var = jnp.mean(jnp.square(y - mu), axis=0, keepdims=True)
        y = (y - mu) * lax.rsqrt(var + RWKV_LN_EPS) * lng_ref[...] + lnb_ref[...]
        bonus = jnp.sum(r * kf * rk_ref[...], axis=0, keepdims=True) * v
        y = (y + bonus) * load(g_ref, t)
        y_ref[:, t] = y.T.reshape(nb, nh, n).astype(y_ref.dtype)
        return carry

    lax.fori_loop(0, r_ref.shape[1], step, 0, unroll=2)


def _rwkv_scan(seqs, params, s0, n_batch, t_len, n_heads, n):
    pairs = n_batch * n_heads
    bpg = LANE // n_heads
    assert LANE % n_heads == 0 and n_batch % bpg == 0
    tc = SCAN_TC if t_len % SCAN_TC == 0 else t_len
    grid = (pairs // LANE, t_len // tc)
    seq_spec = pl.BlockSpec((bpg, tc, n_heads, n), lambda g, t: (g, t, 0, 0))
    par_spec = pl.BlockSpec((n, LANE), lambda g, t: (0, g))
    st_spec = pl.BlockSpec((n, n, LANE), lambda g, t: (0, 0, g))
    y, s_t = pl.pallas_call(
        _rwkv_scan_kernel, grid=grid,
        in_specs=[seq_spec] * 6 + [par_spec] * 5 + [st_spec],
        out_specs=[seq_spec, st_spec],
        out_shape=[jax.ShapeDtypeStruct((n_batch, t_len, n_heads, n), BF16),
                   jax.ShapeDtypeStruct((n, n, pairs), F32)],
        scratch_shapes=[pltpu.VMEM((n, LANE), F32)],
        compiler_params=_cparams(2), name="rwkv_scan",
    )(*[a.reshape(n_batch, t_len, n_heads, n) for a in seqs], *params, s0)
    return y.reshape(n_batch * t_len, n_heads * n), s_t


def _swa_kernel(sink_ref, q_ref, kc_ref, vc_ref, kp_ref, vp_ref, qg_ref, kg_ref,
                o_ref, kn_ref, *, n_kv, group, hd, rows_per_head, stacked, norm_prev,
                first_has_prev, blocks_per_seq):
    rph = rows_per_head
    win = kc_ref.shape[0]
    n_heads = n_kv * group
    has_prev = jnp.logical_or(first_has_prev, pl.program_id(0) % blocks_per_seq > 0)

    def rms(x, g):
        return x * lax.rsqrt(jnp.mean(x * x, axis=-1, keepdims=True) + RMS_EPS) * g

    qpos = lax.broadcasted_iota(jnp.int32, (rph, 2 * win), 0)
    kpos = lax.broadcasted_iota(jnp.int32, (rph, 2 * win), 1) - win
    diff = qpos - kpos
    valid = (diff >= 0) & (diff < win) & (kpos >= jnp.where(has_prev, -win, 0))
    neg_dist = jnp.where(valid, -(diff.astype(F32)), -jnp.inf)
    qg = qg_ref[...]
    kg = kg_ref[...]

    for j in range(n_kv):
        sl = slice(j * hd, (j + 1) * hd)
        kc = rms(kc_ref[:, sl], kg)
        kn_ref[:, sl] = kc
        kp = kp_ref[:, sl]
        if norm_prev:
            kp = rms(kp, kg)
        kb = jnp.concatenate([kp, kc], axis=0).astype(BF16)
        vb = jnp.concatenate([vp_ref[:, sl], vc_ref[:, sl]], axis=0).astype(BF16)
        if stacked:
            q = q_ref[:, sl]
        else:
            q = jnp.concatenate([q_ref[:, (j * group + gi) * hd:(j * group + gi + 1) * hd]
                                 for gi in range(group)], axis=0)
        s = lax.dot_general(rms(q, qg).astype(BF16), kb, (((1,), (1,)), ((), ())),
                            preferred_element_type=F32)
        es, inv = [], []
        for gi in range(group):
            h = j * group + gi
            slope = 2.0 ** (-8.0 * (h + 1) / n_heads)
            sh = s[gi * rph:(gi + 1) * rph] * (hd ** -0.5) + slope * neg_dist
            sink = sink_ref[h]
            m = jnp.maximum(jnp.max(sh, axis=-1, keepdims=True), sink)
            e = jnp.exp(sh - m)
            inv.append(1.0 / (jnp.sum(e, axis=-1, keepdims=True) + jnp.exp(sink - m)))
            es.append(e.astype(BF16))
        o = jnp.dot(jnp.concatenate(es, axis=0), vb, preferred_element_type=F32)
        o = (o * jnp.concatenate(inv, axis=0)).astype(BF16)
        if stacked:
            o_ref[:, sl] = o
        else:
            for gi in range(group):
                h = j * group + gi
                o_ref[:, h * hd:(h + 1) * hd] = o[gi * rph:(gi + 1) * rph]


def _swa(q_arr, kc_arr, vc_arr, kp_arr, vp_arr, maps, n_blocks, q_block, rows_per_head, stacked,
         win, qg, kg, sinks, n_kv, group, hd, norm_prev, first_has_prev, blocks_per_seq):
    q_map, kc_map, vc_map, kp_map, vp_map = maps
    return pl.pallas_call(
        functools.partial(_swa_kernel, n_kv=n_kv, group=group, hd=hd, rows_per_head=rows_per_head,
                          stacked=stacked, norm_prev=norm_prev, first_has_prev=first_has_prev,
                          blocks_per_seq=blocks_per_seq),
        grid=(n_blocks,),
        in_specs=[pl.BlockSpec(memory_space=pltpu.SMEM),
                  pl.BlockSpec(q_block, q_map),
                  pl.BlockSpec((win, n_kv * hd), kc_map),
                  pl.BlockSpec((win, n_kv * hd), vc_map),
                  pl.BlockSpec((win, n_kv * hd), kp_map),
                  pl.BlockSpec((win, n_kv * hd), vp_map),
                  pl.BlockSpec((1, hd), lambda i: (0, 0)),
                  pl.BlockSpec((1, hd), lambda i: (0, 0))],
        out_specs=[pl.BlockSpec(q_block, lambda i: (i, 0)),
                   pl.BlockSpec((win, n_kv * hd), lambda i: (i, 0))],
        out_shape=[jax.ShapeDtypeStruct((n_blocks * q_block[0], q_block[1]), BF16),
                   jax.ShapeDtypeStruct((n_blocks * win, n_kv * hd), F32)],
        compiler_params=_cparams(1), name="swa_stacked%d" % int(stacked),
    )(sinks, q_arr, kc_arr, vc_arr, kp_arr, vp_arr, qg.reshape(1, hd), kg.reshape(1, hd))


def _moe_up_kernel(te_ref, nu_ref, xs_ref, wg_ref, wu_ref, rw_ref, hid_ref, wgb_ref, wub_ref):
    i = pl.program_id(0)
    changed = jnp.logical_or(i == 0, te_ref[i] != te_ref[jnp.maximum(i - 1, 0)])

    @pl.when(changed)
    def _():
        wgb_ref[...] = wg_ref[...].astype(BF16)
        wub_ref[...] = wu_ref[...].astype(BF16)

    @pl.when(i < nu_ref[0])
    def _():
        x = xs_ref[...]
        g = jnp.dot(x, wgb_ref[...], preferred_element_type=F32)
        u = jnp.dot(x, wub_ref[...], preferred_element_type=F32)
        hid_ref[...] = ((g / (1.0 + jnp.exp(-g))) * u * rw_ref[...]).astype(BF16)

    @pl.when(i >= nu_ref[0])
    def _():
        hid_ref[...] = jnp.zeros_like(hid_ref)


def _moe_down_kernel(te_ref, nu_ref, hid_ref, wd_ref, out_ref, wdb_ref):
    i = pl.program_id(0)
    changed = jnp.logical_or(i == 0, te_ref[i] != te_ref[jnp.maximum(i - 1, 0)])

    @pl.when(changed)
    def _():
        wdb_ref[...] = wd_ref[...].astype(BF16)

    @pl.when(i < nu_ref[0])
    def _():
        out_ref[...] = jnp.dot(hid_ref[...], wdb_ref[...], preferred_element_type=F32)

    @pl.when(i >= nu_ref[0])
    def _():
        out_ref[...] = jnp.zeros_like(out_ref)


def _moe(x, h, logits, b_group, b_expert, w_gate, w_up, w_down, layer, split=None):
    m, d = h.shape
    n_groups = b_group.shape[0]
    n_experts = b_expert.shape[0]
    per_group = n_experts // n_groups
    f = w_gate.shape[2] // n_experts
    top_k = 2

    g_logit = logits[:, :n_groups] + b_group
    g_prob = jax.nn.softmax(g_logit, axis=-1)
    g_idx = jnp.argmax(g_logit, axis=-1)
    g_gate = jnp.take_along_axis(g_prob, g_idx[:, None], axis=-1)
    e_logit = (logits[:, n_groups:n_groups + n_experts] + b_expert).reshape(m, n_groups, per_group)
    e_in = jnp.take_along_axis(e_logit, g_idx[:, None, None], axis=1)[:, 0]
    i1 = jnp.argmax(e_in, axis=-1)
    rest = jnp.where(jnp.arange(per_group)[None, :] == i1[:, None], -jnp.inf, e_in)
    i2 = jnp.argmax(rest, axis=-1)
    top_v = jnp.stack([jnp.max(e_in, axis=-1), jnp.max(rest, axis=-1)], axis=-1)
    top_i = jnp.stack([i1, i2], axis=-1)
    top_w = jax.nn.softmax(top_v, axis=-1) * g_gate
    eid = (g_idx[:, None] * per_group + top_i).astype(jnp.int32)

    n_pairs = m * top_k
    n_tiles = (n_pairs + n_experts * (MOE_TM - 1)) // MOE_TM + 1
    n_rows = n_tiles * MOE_TM
    flat_e = eid.reshape(-1)
    order = jnp.argsort(flat_e, stable=True).astype(jnp.int32)
    rank = jnp.argsort(order).astype(jnp.int32)
    counts = jnp.sum(flat_e[None, :] == jnp.arange(n_experts, dtype=jnp.int32)[:, None], axis=1,
                     dtype=jnp.int32)
    padded = ((counts + MOE_TM - 1) // MOE_TM) * MOE_TM
    pad_end = jnp.cumsum(padded)
    pad_start = pad_end - padded
    start = jnp.cumsum(counts) - counts
    n_used = (pad_end[-1] // MOE_TM).astype(jnp.int32)
    tile_row = jnp.minimum(jnp.arange(n_tiles, dtype=jnp.int32), n_used - 1) * MOE_TM
    tile_e = jnp.minimum(jnp.sum(pad_end[None, :] <= tile_row[:, None], axis=1, dtype=jnp.int32),
                         n_experts - 1)
    row = jnp.arange(n_rows, dtype=jnp.int32)
    row_e = jnp.repeat(tile_e, MOE_TM)
    in_group = row - pad_start[row_e]
    live = (in_group < counts[row_e]) & (row < n_used * MOE_TM)
    pair = order[jnp.clip(start[row_e] + in_group, 0, n_pairs - 1)]
    row_token = jnp.where(live, pair // top_k, row % m)
    row_w = jnp.where(live, top_w.reshape(-1)[pair], 0.0)
    pos = (pad_start[flat_e] + rank - start[flat_e]).reshape(m, top_k)
    n_used = n_used.reshape(1)

    xs = jnp.take(h, row_token, axis=0, mode="clip")
    hid = pl.pallas_call(
        _moe_up_kernel,
        grid_spec=pltpu.PrefetchScalarGridSpec(
            num_scalar_prefetch=2, grid=(n_tiles,),
            in_specs=[pl.BlockSpec((MOE_TM, d), lambda i, te, nu: (i, 0)),
                      pl.BlockSpec((None, d, f), lambda i, te, nu: (layer, 0, te[i])),
                      pl.BlockSpec((None, d, f), lambda i, te, nu: (layer, 0, te[i])),
                      pl.BlockSpec((MOE_TM, 1), lambda i, te, nu: (i, 0))],
            out_specs=pl.BlockSpec((MOE_TM, f), lambda i, te, nu: (i, 0)),
            scratch_shapes=[pltpu.VMEM((d, f), BF16), pltpu.VMEM((d, f), BF16)]),
        out_shape=jax.ShapeDtypeStruct((n_rows, f), BF16),
        compiler_params=_cparams(1), name="moe_up",
    )(tile_e, n_used, xs, w_gate, w_up, row_w.reshape(n_rows, 1))
    rows = pl.pallas_call(
        _moe_down_kernel,
        grid_spec=pltpu.PrefetchScalarGridSpec(
            num_scalar_prefetch=2, grid=(n_tiles,),
            in_specs=[pl.BlockSpec((MOE_TM, f), lambda i, te, nu: (i, 0)),
                      pl.BlockSpec((None, f, d), lambda i, te, nu: (layer, te[i], 0))],
            out_specs=pl.BlockSpec((MOE_TM, d), lambda i, te, nu: (i, 0)),
            scratch_shapes=[pltpu.VMEM((f, d), BF16)]),
        out_shape=jax.ShapeDtypeStruct((n_rows, d), F32),
        compiler_params=_cparams(1), name="moe_down",
    )(tile_e, n_used, hid, w_down)

    def combine(lo, hi):
        return (x[lo:hi] + jnp.take(rows, pos[lo:hi, 0], axis=0, mode="clip")
                + jnp.take(rows, pos[lo:hi, 1], axis=0, mode="clip"))

    if split is None:
        return combine(0, m)
    return combine(0, split), combine(split, m)


def _param_pairs(a, n_batch, n_heads, n):
    return jnp.tile(a.reshape(n_heads, n).T, (1, n_batch))


def _even_layer(x, h, i, groups, state_ret, state_rwkv, state_shift, wts):
    (a_w_in, a_w_out, ret_norm_g, rwkv_mu, rwkv_w0, rwkv_w2, rwkv_a0, rwkv_a2, rwkv_g2,
     rwkv_kk_scale, rwkv_ka, rwkv_rk, rwkv_lnx_g, rwkv_lnx_b) = [w[i] for w in wts]
    ret_heads, ret_dk = state_ret.shape[2], state_ret.shape[3]
    rw_heads, rw_n = state_rwkv.shape[2], state_rwkv.shape[3]
    ret_w = ret_heads * ret_dk
    rw_w = rw_heads * rw_n
    lora_w, lora_a, lora_g = rwkv_w2.shape[0], rwkv_a2.shape[0], rwkv_g2.shape[0]
    assert lora_w == LANE and lora_a == LANE and lora_g <= LANE and ret_w == rw_w
    n_main = 4 * ret_w + 3 * rw_w
    n_shift = 3 * rw_w + lora_w + lora_a + lora_g
    (pb, pt), (sb, st) = groups
    m_p, m_s = pb * pt, sb * st

    p = _matmul([h], a_w_in, n_main, name="a_in")
    tail_cols = 3 * LANE
    w_tail = jnp.pad(a_w_in[:, n_main:], ((0, 0), (0, tail_cols - (lora_w + lora_a + lora_g))))
    tail = _matmul([h], w_tail, tail_cols, tn=tail_cols, name="a_in_tail")

    o_ret_p, ret_p = _retention_prompt(p, ret_norm_g, pb, pt, ret_heads, ret_dk)
    o_ret_s, ret_s = _retention_sample(p, ret_norm_g, state_ret[i], m_p, sb, st, ret_heads, ret_dk)

    pad_g = lambda a: jnp.pad(a, ((0, LANE - a.shape[0]), (0, 0)))
    mu = rwkv_mu
    mu_tail = jnp.pad(mu[3 * rw_w:], (0, tail_cols - (n_shift - 3 * rw_w)))
    prep_params = [mu[:rw_w].reshape(1, -1), mu[rw_w:2 * rw_w].reshape(1, -1),
                   mu[2 * rw_w:3 * rw_w].reshape(1, -1), mu_tail.reshape(1, -1),
                   rwkv_w0.reshape(1, -1), rwkv_w2, rwkv_a0.reshape(1, -1), rwkv_a2,
                   pad_g(rwkv_g2)]
    cb0 = (4 * ret_w) // rw_w
    shift0 = state_shift[i]
    sh_main = jnp.repeat(shift0[:, :3 * rw_w], st, axis=0)
    sh_tail = jnp.repeat(jnp.pad(shift0[:, 3 * rw_w:], ((0, 0), (0, tail_cols - (n_shift - 3 * rw_w)))),
                         st, axis=0)
    seq_p = _rwkv_prep(p, tail, 0, m_p, cb0, rw_w, prep_params, None, pt, st)
    seq_s = _rwkv_prep(p, tail, m_p, m_s, cb0, rw_w, prep_params, (sh_main, sh_tail), pt, st)

    outs = []
    for seqs, nb, nt, s0 in ((seq_p, pb, pt, None), (seq_s, sb, st, state_rwkv[i])):
        scan_params = [_param_pairs(a, nb, rw_heads, rw_n)
                       for a in (rwkv_kk_scale, rwkv_ka, rwkv_rk, rwkv_lnx_g, rwkv_lnx_b)]
        if s0 is None:
            s0_l = jnp.zeros((rw_n, rw_n, nb * rw_heads), F32)
        else:
            s0_l = s0.transpose(2, 3, 0, 1).reshape(rw_n, rw_n, nb * rw_heads)
        y, s_t = _rwkv_scan(seqs, scan_params, s0_l, nb, nt, rw_heads, rw_n)
        s_t = s_t.reshape(rw_n, rw_n, nb, rw_heads).transpose(2, 3, 0, 1)
        outs.append((y, s_t))
    (y_p, rwkv_p), (y_s, rwkv_s) = outs

    o_ret = jnp.concatenate([o_ret_p, o_ret_s], axis=0)
    y = jnp.concatenate([y_p, y_s], axis=0)
    x = _matmul([o_ret, y], a_w_out, a_w_out.shape[1], res=x, name="a_out")

    def last_ps(row0, nb, nt):
        last = lambda a, lo, hi: a[row0:row0 + nb * nt].reshape(nb, nt, -1)[:, nt - 1, lo:hi]
        return jnp.concatenate([last(p, 4 * ret_w, n_main), last(tail, 0, n_shift - 3 * rw_w)],
                               axis=-1)

    return x, (ret_p, rwkv_p, last_ps(0, pb, pt)), (ret_s, rwkv_s, last_ps(m_p, sb, st))


def _odd_layer(x, h, i, groups, cache_k, cache_v, wts):
    c_w_in, c_w_out, c_q_norm_g, c_k_norm_g, c_sinks = [w[i] for w in wts]
    win, n_kv, hd = cache_k.shape[2], cache_k.shape[3], cache_k.shape[4]
    n_heads = c_sinks.shape[0]
    group = n_heads // n_kv
    mix_c = n_heads * hd
    kv_w = n_kv * hd
    (pb, pt), (sb, st) = groups
    m_p = pb * pt
    assert pt % win == 0 and mix_c % kv_w == 0
    p = _matmul([h], c_w_in, mix_c + 2 * kv_w, name="c_in")
    kcb, vcb = mix_c // kv_w, mix_c // kv_w + 1

    nb_p = m_p // win
    maps = (lambda n: (n, 0), lambda n: (n, kcb), lambda n: (n, vcb),
            lambda n: (jnp.maximum(n - 1, 0), kcb), lambda n: (jnp.maximum(n - 1, 0), vcb))
    o_p, kn_p = _swa(p, p, p, p, p, maps, nb_p, (win, mix_c), win, False, win, c_q_norm_g,
                     c_k_norm_g, c_sinks, n_kv, group, hd, True, False, pt // win)

    tq = 16
    ps = p[m_p:].reshape(sb, st, -1)
    q_s = jnp.pad(ps[:, :, :mix_c].reshape(sb, st, n_kv, group, hd),
                  ((0, 0), (0, tq - st), (0, 0), (0, 0), (0, 0)))
    q_s = q_s.transpose(0, 3, 1, 2, 4).reshape(sb * group * tq, kv_w)
    kv_pad = lambda a: jnp.pad(a, ((0, 0), (0, win - st), (0, 0))).reshape(sb * win, kv_w)
    k_s = kv_pad(ps[:, :, mix_c:mix_c + kv_w])
    v_s = kv_pad(ps[:, :, mix_c + kv_w:])
    kc0 = cache_k[i].reshape(sb * win, kv_w)
    vc0 = cache_v[i].reshape(sb * win, kv_w)
    same = lambda n: (n, 0)
    o_s, kn_s = _swa(q_s, k_s, v_s, kc0, vc0, (same,) * 5, sb, (group * tq, kv_w), tq, True, win,
                     c_q_norm_g, c_k_norm_g, c_sinks, n_kv, group, hd, False, True, 1)
    o_s = o_s.reshape(sb, group, tq, n_kv, hd)[:, :, :st].transpose(0, 2, 3, 1, 4).reshape(
        sb * st, mix_c)

    x = _matmul([jnp.concatenate([o_p, o_s], axis=0)], c_w_out, c_w_out.shape[1], res=x, name="c_out")

    new_k_p = kn_p.reshape(pb, pt, n_kv, hd)[:, -win:]
    new_v_p = p[:m_p, mix_c + kv_w:].reshape(pb, pt, n_kv, hd)[:, -win:]
    kn_s = kn_s.reshape(sb, win, n_kv, hd)[:, :st]
    new_k_s = jnp.concatenate([cache_k[i], kn_s], axis=1)[:, -win:]
    new_v_s = jnp.concatenate([cache_v[i], ps[:, :, mix_c + kv_w:].reshape(sb, st, n_kv, hd)],
                              axis=1)[:, -win:]
    return x, (new_k_p, new_v_p), (new_k_s, new_v_s)


def kernel(x_prompt, x_sample, state_ret, state_rwkv, state_rwkv_shift, cache_swa_k, cache_swa_v, norm_mix_g, norm_ffn_g, a_w_in, a_w_out, ret_norm_g, rwkv_mu, rwkv_w0, rwkv_w2, rwkv_a0, rwkv_a2, rwkv_g2, rwkv_kk_scale, rwkv_ka, rwkv_rk, rwkv_lnx_g, rwkv_lnx_b, c_w_in, c_w_out, c_q_norm_g, c_k_norm_g, c_sinks, moe_w_group, moe_b_group, moe_w_expert, moe_b_expert, moe_w_gate, moe_w_up, moe_w_down):
    pb, pt, d = x_prompt.shape
    sb, st, _ = x_sample.shape
    groups = ((pb, pt), (sb, st))
    depth = norm_mix_g.shape[0]
    a_wts = (a_w_in, a_w_out, ret_norm_g, rwkv_mu, rwkv_w0, rwkv_w2, rwkv_a0, rwkv_a2, rwkv_g2,
             rwkv_kk_scale, rwkv_ka, rwkv_rk, rwkv_lnx_g, rwkv_lnx_b)
    c_wts = (c_w_in, c_w_out, c_q_norm_g, c_k_norm_g, c_sinks)
    st_p = [[] for _ in range(5)]
    st_s = [[] for _ in range(5)]
    for l in range(depth):
        i = l // 2
        if l == 0:
            h, x = _rms_norm_first(x_prompt.reshape(pb * pt, d), x_sample.reshape(sb * st, d),
                                   norm_mix_g[l])
        else:
            h = _rms_norm(x, norm_mix_g[l])
        if l % 2 == 0:
            x, sp, ss = _even_layer(x, h, i, groups, state_ret, state_rwkv, state_rwkv_shift, a_wts)
            for j in range(3):
                st_p[j].append(sp[j])
                st_s[j].append(ss[j])
        else:
            x, sp, ss = _odd_layer(x, h, i, groups, cache_swa_k, cache_swa_v, c_wts)
            for j in range(2):
                st_p[3 + j].append(sp[j])
                st_s[3 + j].append(ss[j])
        n_router = moe_w_group.shape[2] + moe_w_expert.shape[2]
        w_router = jnp.pad(jnp.concatenate([moe_w_group[l], moe_w_expert[l]], axis=1),
                           ((0, 0), (0, LANE - n_router)))
        h, logits = _rms_norm(x, norm_ffn_g[l], w_router)
        x = _moe(x, h, logits, moe_b_group[l], moe_b_expert[l], moe_w_gate, moe_w_up, moe_w_down, l,
                 split=pb * pt if l == depth - 1 else None)
    y_prompt = x[0].reshape(pb, pt, d)
    y_sample = x[1].reshape(sb, st, d)
    return (y_prompt, y_sample) + tuple(jnp.stack(s) for s in st_p) + tuple(jnp.stack(s) for s in st_s)
```

```python
import functools

import jax
import jax.numpy as jnp
from jax import lax
from jax.experimental import pallas as pl
from jax.experimental.pallas import tpu as pltpu

F32 = jnp.float32
BF16 = jnp.bfloat16

RMS_EPS = 1e-6
RWKV_LN_EPS = 64e-5
RET_CHUNK = 128

LANE = 128
VMEM_LIMIT = 56 * 1024 * 1024

MM_TM = 640
MM_TN = 512
NORM_TM = 320
MOE_TM = 256
SCAN_TC = 32
RET_HEADS_PER_STEP = 4


def _cparams(n_axes):
    return pltpu.CompilerParams(dimension_semantics=("arbitrary",) * n_axes,
                                vmem_limit_bytes=VMEM_LIMIT)


def _norm_kernel(x_ref, g_ref, h_ref):
    x = x_ref[...]
    y = x * lax.rsqrt(jnp.mean(x * x, axis=-1, keepdims=True) + RMS_EPS) * g_ref[...]
    h_ref[...] = y.astype(BF16)


def _norm_router_kernel(x_ref, g_ref, wr_ref, h_ref, logit_ref):
    x = x_ref[...]
    y = x * lax.rsqrt(jnp.mean(x * x, axis=-1, keepdims=True) + RMS_EPS) * g_ref[...]
    h_ref[...] = y.astype(BF16)
    logit_ref[...] = jnp.dot(y, wr_ref[...], precision=lax.Precision.HIGHEST,
                             preferred_element_type=F32)


def _norm_first_kernel(xp_ref, xs_ref, g_ref, h_ref, x_ref, *, n_prompt_tiles):
    x = jnp.where(pl.program_id(0) < n_prompt_tiles, xp_ref[...], xs_ref[...])
    x_ref[...] = x
    y = x * lax.rsqrt(jnp.mean(x * x, axis=-1, keepdims=True) + RMS_EPS) * g_ref[...]
    h_ref[...] = y.astype(BF16)


def _rms_norm_first(xp, xs, g):
    (m_p, d), m_s = xp.shape, xs.shape[0]
    tm = LANE
    assert m_p % tm == 0 and m_s % tm == 0
    npt = m_p // tm
    spec = pl.BlockSpec((tm, d), lambda i: (i, 0))
    return pl.pallas_call(
        functools.partial(_norm_first_kernel, n_prompt_tiles=npt), grid=((m_p + m_s) // tm,),
        in_specs=[pl.BlockSpec((tm, d), lambda i: (jnp.minimum(i, npt - 1), 0)),
                  pl.BlockSpec((tm, d), lambda i: (jnp.maximum(i - npt, 0), 0)),
                  pl.BlockSpec((1, d), lambda i: (0, 0))],
        out_specs=[spec, spec],
        out_shape=[jax.ShapeDtypeStruct((m_p + m_s, d), BF16),
                   jax.ShapeDtypeStruct((m_p + m_s, d), F32)],
        compiler_params=_cparams(1), name="rms_norm_first")(xp, xs, g.reshape(1, d))


def _rms_norm(x, g, w_router=None):
    m, d = x.shape
    tm = NORM_TM if m % NORM_TM == 0 else m
    grid = (m // tm,)
    x_spec = pl.BlockSpec((tm, d), lambda i: (i, 0))
    g_spec = pl.BlockSpec((1, d), lambda i: (0, 0))
    if w_router is None:
        return pl.pallas_call(
            _norm_kernel, grid=grid, in_specs=[x_spec, g_spec], out_specs=x_spec,
            out_shape=jax.ShapeDtypeStruct((m, d), BF16), compiler_params=_cparams(1),
            name="rms_norm")(x, g.reshape(1, d))
    nr = w_router.shape[1]
    return pl.pallas_call(
        _norm_router_kernel, grid=grid,
        in_specs=[x_spec, g_spec, pl.BlockSpec((d, nr), lambda i: (0, 0))],
        out_specs=[x_spec, pl.BlockSpec((tm, nr), lambda i: (i, 0))],
        out_shape=[jax.ShapeDtypeStruct((m, d), BF16), jax.ShapeDtypeStruct((m, nr), F32)],
        compiler_params=_cparams(1), name="rms_norm_router")(x, g.reshape(1, d), w_router)


def _mm_kernel(*refs, n_parts, has_res):
    x_refs = refs[:n_parts]
    w_refs = refs[n_parts:2 * n_parts]
    pos = 2 * n_parts
    res_ref = refs[pos] if has_res else None
    pos += int(has_res)
    o_ref = refs[pos]
    wb_refs = refs[pos + 1:]

    @pl.when(pl.program_id(1) == 0)
    def _():
        for w_ref, wb_ref in zip(w_refs, wb_refs):
            wb_ref[...] = w_ref[...].astype(BF16)

    acc = None
    for x_ref, wb_ref in zip(x_refs, wb_refs):
        d = jnp.dot(x_ref[...], wb_ref[...], preferred_element_type=F32)
        acc = d if acc is None else acc + d
    if has_res:
        acc = acc + res_ref[...]
    o_ref[...] = acc


def _matmul(xs, w, n_out, col_block0=0, tn=MM_TN, res=None, name="proj"):
    m = xs[0].shape[0]
    tm = MM_TM if m % MM_TM == 0 else m
    assert n_out % tn == 0
    grid = (n_out // tn, m // tm)
    in_specs, w_specs, scratch = [], [], []
    row = 0
    for x in xs:
        kp = x.shape[1]
        in_specs.append(pl.BlockSpec((tm, kp), lambda j, i: (i, 0)))
        assert row % kp == 0
        rb = row // kp
        w_specs.append(pl.BlockSpec((kp, tn), lambda j, i, rb=rb: (rb, j + col_block0)))
        scratch.append(pltpu.VMEM((kp, tn), BF16))
        row += kp
    assert row == w.shape[0]
    args = list(xs) + [w] * len(xs)
    in_specs = in_specs + w_specs
    if res is not None:
        in_specs.append(pl.BlockSpec((tm, tn), lambda j, i: (i, j)))
        args.append(res)
    return pl.pallas_call(
        functools.partial(_mm_kernel, n_parts=len(xs), has_res=res is not None),
        grid=grid, in_specs=in_specs,
        out_specs=pl.BlockSpec((tm, tn), lambda j, i: (i, j)),
        out_shape=jax.ShapeDtypeStruct((m, n_out), F32),
        scratch_shapes=scratch, compiler_params=_cparams(2), name=name)(*args)


def _ret_finish(o, g, gain):
    o = o * lax.rsqrt(jnp.mean(o * o, axis=-1, keepdims=True) + RMS_EPS) * gain
    return ((g / (1.0 + jnp.exp(-g))) * o).astype(BF16)


def _ret_prompt_kernel(q_ref, k_ref, v_ref, g_ref, gain_ref, lg_ref, o_ref, s_ref, *, dk):
    c = pl.program_id(2)
    chunk = q_ref.shape[0]

    @pl.when(c == 0)
    def _():
        s_ref[...] = jnp.zeros_like(s_ref)

    row = lax.broadcasted_iota(jnp.int32, (chunk, chunk), 0)
    col = lax.broadcasted_iota(jnp.int32, (chunk, chunk), 1)
    diff = (row - col).astype(F32)
    causal = diff >= 0
    dist = jnp.where(causal, diff, 0.0)
    pos = lax.broadcasted_iota(jnp.int32, (chunk, 1), 0).astype(F32)

    for hh in range(s_ref.shape[0]):
        sl = slice(hh * dk, (hh + 1) * dk)
        lg = lg_ref[hh][:, :1]
        decay_mask = jnp.where(causal, jnp.exp(dist * lg), 0.0)
        q_decay = jnp.exp((pos + 1.0) * lg)
        k_decay = jnp.exp((chunk - 1.0 - pos) * lg)
        chunk_decay = jnp.exp(chunk * lg)
        q = q_ref[:, sl]
        k = k_ref[:, sl] * (dk ** -0.5)
        vb = v_ref[:, sl].astype(BF16)
        s = s_ref[hh]
        scores = lax.dot_general(q.astype(BF16), k.astype(BF16), (((1,), (1,)), ((), ())),
                                 preferred_element_type=F32) * decay_mask
        inner = jnp.dot(scores.astype(BF16), vb, preferred_element_type=F32)
        cross = jnp.dot((q * q_decay).astype(BF16), s.astype(BF16), preferred_element_type=F32)
        s_ref[hh] = chunk_decay * s + lax.dot_general(
            (k * k_decay).astype(BF16), vb, (((0,), (0,)), ((), ())), preferred_element_type=F32)
        o_ref[:, sl] = _ret_finish(inner + cross, g_ref[:, sl], gain_ref[hh])


def _ret_sample_kernel(q_ref, k_ref, v_ref, g_ref, gain_ref, lg_ref, s0_ref, o_ref, s_ref, acc_ref,
                       *, dk, t_len):
    b = pl.program_id(1)
    rows = q_ref.shape[0]
    lg = lg_ref[:, :1]
    r_b = lax.broadcasted_iota(jnp.int32, (rows, 1), 0) // t_len
    r_t = (lax.broadcasted_iota(jnp.int32, (rows, 1), 0) % t_len).astype(F32)
    q = q_ref[...]
    k = k_ref[...] * (dk ** -0.5)
    vb = v_ref[...].astype(BF16)

    @pl.when(b == 0)
    def _():
        row = lax.broadcasted_iota(jnp.int32, (rows, rows), 0)
        col = lax.broadcasted_iota(jnp.int32, (rows, rows), 1)
        diff = (row % t_len - col % t_len).astype(F32)
        ok = (row // t_len == col // t_len) & (diff >= 0)
        mask = jnp.where(ok, jnp.exp(jnp.where(ok, diff, 0.0) * lg), 0.0)
        scores = lax.dot_general(q.astype(BF16), k.astype(BF16), (((1,), (1,)), ((), ())),
                                 preferred_element_type=F32) * mask
        acc_ref[...] = jnp.dot(scores.astype(BF16), vb, preferred_element_type=F32)

    sel = r_b == b
    s0 = s0_ref[...]
    cross = jnp.dot((q * jnp.exp((r_t + 1.0) * lg)).astype(BF16), s0.astype(BF16),
                    preferred_element_type=F32)
    acc_ref[...] += jnp.where(sel, cross, 0.0)
    k_decay = jnp.where(sel, jnp.exp((t_len - 1.0 - r_t) * lg), 0.0)
    s_ref[...] = jnp.exp(t_len * lg) * s0 + lax.dot_general(
        (k * k_decay).astype(BF16), vb, (((0,), (0,)), ((), ())), preferred_element_type=F32)

    @pl.when(b == pl.num_programs(1) - 1)
    def _():
        o_ref[...] = _ret_finish(acc_ref[...], g_ref[...], gain_ref[...])


def _ret_log_decay(n_heads):
    lg = jnp.log(1.0 - 2.0 ** (-5.0 - jnp.arange(n_heads, dtype=F32)))
    return jnp.broadcast_to(lg[:, None, None], (n_heads, 1, LANE))


def _retention_prompt(p, gain, n_batch, seq, n_heads, dk):
    nc = seq // RET_CHUNK
    hps = RET_HEADS_PER_STEP
    assert n_heads % hps == 0
    ng = n_heads // hps
    col = lambda grp: (lambda b, h, c: (b * nc + c, grp * ng + h))
    blk = lambda grp: pl.BlockSpec((RET_CHUNK, hps * dk), col(grp))
    return pl.pallas_call(
        functools.partial(_ret_prompt_kernel, dk=dk),
        grid=(n_batch, ng, nc),
        in_specs=[blk(0), blk(1), blk(2), blk(3),
                  pl.BlockSpec((hps, 1, dk), lambda b, h, c: (h, 0, 0)),
                  pl.BlockSpec((hps, 1, LANE), lambda b, h, c: (h, 0, 0))],
        out_specs=[pl.BlockSpec((RET_CHUNK, hps * dk), lambda b, h, c: (b * nc + c, h)),
                   pl.BlockSpec((None, hps, dk, dk), lambda b, h, c: (b, h, 0, 0))],
        out_shape=[jax.ShapeDtypeStruct((n_batch * seq, n_heads * dk), BF16),
                   jax.ShapeDtypeStruct((n_batch, n_heads, dk, dk), F32)],
        compiler_params=_cparams(3), name="retention_prompt",
    )(p, p, p, p, gain.reshape(n_heads, 1, dk), _ret_log_decay(n_heads))


def _retention_sample(p, gain, s0, row0, n_batch, t_len, n_heads, dk):
    rows = n_batch * t_len
    rb = row0 // rows
    blk = lambda grp: pl.BlockSpec((rows, dk), lambda h, b, grp=grp: (rb, grp * n_heads + h))
    return pl.pallas_call(
        functools.partial(_ret_sample_kernel, dk=dk, t_len=t_len),
        grid=(n_heads, n_batch),
        in_specs=[blk(0), blk(1), blk(2), blk(3),
                  pl.BlockSpec((None, 1, dk), lambda h, b: (h, 0, 0)),
                  pl.BlockSpec((None, 1, LANE), lambda h, b: (h, 0, 0)),
                  pl.BlockSpec((None, None, dk, dk), lambda h, b: (b, h, 0, 0))],
        out_specs=[pl.BlockSpec((rows, dk), lambda h, b: (0, h)),
                   pl.BlockSpec((None, None, dk, dk), lambda h, b: (b, h, 0, 0))],
        out_shape=[jax.ShapeDtypeStruct((rows, n_heads * dk), BF16),
                   jax.ShapeDtypeStruct((n_batch, n_heads, dk, dk), F32)],
        scratch_shapes=[pltpu.VMEM((rows, dk), F32)],
        compiler_params=_cparams(2), name="retention_sample",
    )(p, p, p, p, gain.reshape(n_heads, 1, dk), _ret_log_decay(n_heads), s0)


def _sigmoid(x):
    return 1.0 / (1.0 + jnp.exp(-x))


def _rwkv_prep_kernel(*refs, sample, seq, t_len):
    if sample:
        (r_c, k_c, v_c, t_c, r_s, k_s, v_s, t_s,
         mu_r, mu_k, mu_v, mu_t, w0, w2, a0, a2, g2,
         r_o, w_o, k_o, v_o, a_o, g_o) = refs
    else:
        (r_c, k_c, v_c, t_c, r_p, k_p, v_p, t_p,
         mu_r, mu_k, mu_v, mu_t, w0, w2, a0, a2, g2,
         r_o, w_o, k_o, v_o, a_o, g_o) = refs
    tm = r_c.shape[0]
    row = lax.broadcasted_iota(jnp.int32, (tm, 1), 0)
    seq_start = (pl.program_id(0) * tm) % seq == 0

    def mix(cur_ref, other_ref, mu_ref):
        cur = cur_ref[...]
        rolled = pltpu.roll(cur, 1, 0)
        if sample:
            prev = jnp.where(row % t_len == 0, other_ref[...], rolled)
        else:
            last = other_ref[7:8, :]
            first = jnp.where(seq_start, jnp.zeros_like(last), last)
            prev = jnp.where(row == 0, first, rolled)
        return cur + (prev - cur) * mu_ref[...]

    other = (r_s, k_s, v_s, t_s) if sample else (r_p, k_p, v_p, t_p)
    r_o[...] = mix(r_c, other[0], mu_r)
    k_o[...] = mix(k_c, other[1], mu_k)
    v_o[...] = mix(v_c, other[2], mu_v)
    tail = mix(t_c, other[3], mu_t)
    lw, la, lgt = tail[:, :LANE], tail[:, LANE:2 * LANE], tail[:, 2 * LANE:]

    z = -(w0[...] + jnp.dot(jnp.tanh(lw).astype(BF16), w2[...].astype(BF16),
                            preferred_element_type=F32))
    softplus = jnp.maximum(z, 0.0) + jnp.log1p(jnp.exp(-jnp.abs(z)))
    w_o[...] = jnp.exp(-jnp.exp(-softplus - 0.5))
    a_o[...] = _sigmoid(a0[...] + jnp.dot(la.astype(BF16), a2[...].astype(BF16),
                                          preferred_element_type=F32))
    g_o[...] = jnp.dot(_sigmoid(lgt).astype(BF16), g2[...].astype(BF16),
                       preferred_element_type=F32)


def _rwkv_prep(p, tail, row0, n_rows, col_block0, width, params, shift_rows, seq, t_len):
    sample = shift_rows is not None
    tm = n_rows if sample else 256
    rb0 = row0 // tm
    grid = (n_rows // tm,)
    cur = lambda grp: pl.BlockSpec((tm, width), lambda i, grp=grp: (rb0 + i, col_block0 + grp))
    tail_w = tail.shape[1]
    in_specs = [cur(0), cur(1), cur(2), pl.BlockSpec((tm, tail_w), lambda i: (rb0 + i, 0))]
    args = [p, p, p, tail]
    if sample:
        sh_main, sh_tail = shift_rows
        in_specs += [pl.BlockSpec((tm, width), lambda i, grp=grp: (0, grp)) for grp in range(3)]
        in_specs += [pl.BlockSpec((tm, tail_w), lambda i: (0, 0))]
        args += [sh_main, sh_main, sh_main, sh_tail]
    else:
        per = tm // 8
        prev_rb = lambda i: jnp.maximum((rb0 + i) * per - 1, 0)
        in_specs += [pl.BlockSpec((8, width), lambda i, grp=grp: (prev_rb(i), col_block0 + grp))
                     for grp in range(3)]
        in_specs += [pl.BlockSpec((8, tail_w), lambda i: (prev_rb(i), 0))]
        args += [p, p, p, tail]
    full = lambda a: pl.BlockSpec(a.shape, lambda i: (0,) * a.ndim)
    in_specs += [full(a) for a in params]
    args += list(params)
    out_spec = pl.BlockSpec((tm, width), lambda i: (i, 0))
    return pl.pallas_call(
        functools.partial(_rwkv_prep_kernel, sample=sample, seq=seq, t_len=t_len),
        grid=grid, in_specs=in_specs, out_specs=[out_spec] * 6,
        out_shape=[jax.ShapeDtypeStruct((n_rows, width), F32)] * 6,
        compiler_params=_cparams(1), name="rwkv_prep_sample" if sample else "rwkv_prep_prompt",
    )(*args)


def _rwkv_scan_kernel(r_ref, w_ref, k_ref, v_ref, a_ref, g_ref,
                      kks_ref, ka_ref, rk_ref, lng_ref, lnb_ref, s0_ref,
                      y_ref, s_ref, ys_ref):
    n_v = s_ref.shape[0]

    @pl.when(pl.program_id(1) == 0)
    def _():
        s_ref[...] = s0_ref[...]

    def step(t, carry):
        r = r_ref[t]
        w = w_ref[t]
        kr = k_ref[t]
        v = v_ref[t]
        a = a_ref[t]
        kks = kr * kks_ref[...]
        kk = kks / jnp.maximum(jnp.sqrt(jnp.sum(kks * kks, axis=0, keepdims=True)), 1e-12)
        kf = kr * (1.0 + (a - 1.0) * ka_ref[...])
        kka = kk * a
        for vi in range(n_v):
            s = s_ref[vi]
            sa = jnp.sum(s * kk, axis=0, keepdims=True)
            s = s * w - sa * kka + v[vi:vi + 1, :] * kf
            s_ref[vi] = s
            ys_ref[vi:vi + 1, :] = jnp.sum(s * r, axis=0, keepdims=True)
        y = ys_ref[...]
        mu = jnp.mean(y, axis=0, keepdims=True)
        var = jnp.mean(jnp.square(y - mu), axis=0, keepdims=True)
        y = (y - mu) * lax.rsqrt(var + RWKV_LN_EPS) * lng_ref[...] + lnb_ref[...]
        bonus = jnp.sum(r * kf * rk_ref[...], axis=0, keepdims=True) * v
        y_ref[t] = ((y + bonus) * g_ref[t]).astype(y_ref.dtype)
        return carry

    lax.fori_loop(0, r_ref.shape[0], step, 0)


def _pairs_major(a, n_batch, t_len, n_heads, n):
    return a.reshape(n_batch, t_len, n_heads, n).transpose(1, 3, 0, 2).reshape(
        t_len, n, n_batch * n_heads)


def _rwkv_scan(seqs, params, s0, n_batch, t_len, n_heads, n):
    pairs = n_batch * n_heads
    tc = SCAN_TC if t_len % SCAN_TC == 0 else t_len
    grid = (pairs // LANE, t_len // tc)
    seq_spec = pl.BlockSpec((tc, n, LANE), lambda g, t: (t, 0, g))
    par_spec = pl.BlockSpec((n, LANE), lambda g, t: (0, g))
    st_spec = pl.BlockSpec((n, n, LANE), lambda g, t: (0, 0, g))
    y, s_t = pl.pallas_call(
        _rwkv_scan_kernel, grid=grid,
        in_specs=[seq_spec] * 6 + [par_spec] * 5 + [st_spec],
        out_specs=[seq_spec, st_spec],
        out_shape=[jax.ShapeDtypeStruct((t_len, n, pairs), BF16),
                   jax.ShapeDtypeStruct((n, n, pairs), F32)],
        scratch_shapes=[pltpu.VMEM((n, LANE), F32)],
        compiler_params=_cparams(2), name="rwkv_scan",
    )(*[_pairs_major(a, n_batch, t_len, n_heads, n) for a in seqs], *params, s0)
    y = y.reshape(t_len, n, n_batch, n_heads).transpose(2, 0, 3, 1)
    return y.reshape(n_batch * t_len, n_heads * n), s_t


def _swa_kernel(sink_ref, q_ref, kc_ref, vc_ref, kp_ref, vp_ref, qg_ref, kg_ref,
                o_ref, kn_ref, *, n_kv, group, hd, rows_per_head, stacked, norm_prev,
                first_has_prev, blocks_per_seq):
    rph = rows_per_head
    win = kc_ref.shape[0]
    n_heads = n_kv * group
    has_prev = jnp.logical_or(first_has_prev, pl.program_id(0) % blocks_per_seq > 0)

    def rms(x, g):
        return x * lax.rsqrt(jnp.mean(x * x, axis=-1, keepdims=True) + RMS_EPS) * g

    qpos = lax.broadcasted_iota(jnp.int32, (rph, 2 * win), 0)
    kpos = lax.broadcasted_iota(jnp.int32, (rph, 2 * win), 1) - win
    diff = qpos - kpos
    valid = (diff >= 0) & (diff < win) & (kpos >= jnp.where(has_prev, -win, 0))
    neg_dist = jnp.where(valid, -(diff.astype(F32)), -jnp.inf)
    qg = qg_ref[...]
    kg = kg_ref[...]

    for j in range(n_kv):
        sl = slice(j * hd, (j + 1) * hd)
        kc = rms(kc_ref[:, sl], kg)
        kn_ref[:, sl] = kc
        kp = kp_ref[:, sl]
        if norm_prev:
            kp = rms(kp, kg)
        kb = jnp.concatenate([kp, kc], axis=0).astype(BF16)
        vb = jnp.concatenate([vp_ref[:, sl], vc_ref[:, sl]], axis=0).astype(BF16)
        if stacked:
            q = q_ref[:, sl]
        else:
            q = jnp.concatenate([q_ref[:, (j * group + gi) * hd:(j * group + gi + 1) * hd]
                                 for gi in range(group)], axis=0)
        s = lax.dot_general(rms(q, qg).astype(BF16), kb, (((1,), (1,)), ((), ())),
                            preferred_element_type=F32)
        es, inv = [], []
        for gi in range(group):
            h = j * group + gi
            slope = 2.0 ** (-8.0 * (h + 1) / n_heads)
            sh = s[gi * rph:(gi + 1) * rph] * (hd ** -0.5) + slope * neg_dist
            sink = sink_ref[h]
            m = jnp.maximum(jnp.max(sh, axis=-1, keepdims=True), sink)
            e = jnp.exp(sh - m)
            inv.append(1.0 / (jnp.sum(e, axis=-1, keepdims=True) + jnp.exp(sink - m)))
            es.append(e.astype(BF16))
        o = jnp.dot(jnp.concatenate(es, axis=0), vb, preferred_element_type=F32)
        o = (o * jnp.concatenate(inv, axis=0)).astype(BF16)
        if stacked:
            o_ref[:, sl] = o
        else:
            for gi in range(group):
                h = j * group + gi
                o_ref[:, h * hd:(h + 1) * hd] = o[gi * rph:(gi + 1) * rph]


def _swa(q_arr, kc_arr, vc_arr, kp_arr, vp_arr, maps, n_blocks, q_block, rows_per_head, stacked,
         win, qg, kg, sinks, n_kv, group, hd, norm_prev, first_has_prev, blocks_per_seq):
    q_map, kc_map, vc_map, kp_map, vp_map = maps
    return pl.pallas_call(
        functools.partial(_swa_kernel, n_kv=n_kv, group=group, hd=hd, rows_per_head=rows_per_head,
                          stacked=stacked, norm_prev=norm_prev, first_has_prev=first_has_prev,
                          blocks_per_seq=blocks_per_seq),
        grid=(n_blocks,),
        in_specs=[pl.BlockSpec(memory_space=pltpu.SMEM),
                  pl.BlockSpec(q_block, q_map),
                  pl.BlockSpec((win, n_kv * hd), kc_map),
                  pl.BlockSpec((win, n_kv * hd), vc_map),
                  pl.BlockSpec((win, n_kv * hd), kp_map),
                  pl.BlockSpec((win, n_kv * hd), vp_map),
                  pl.BlockSpec((1, hd), lambda i: (0, 0)),
                  pl.BlockSpec((1, hd), lambda i: (0, 0))],
        out_specs=[pl.BlockSpec(q_block, lambda i: (i, 0)),
                   pl.BlockSpec((win, n_kv * hd), lambda i: (i, 0))],
        out_shape=[jax.ShapeDtypeStruct((n_blocks * q_block[0], q_block[1]), BF16),
                   jax.ShapeDtypeStruct((n_blocks * win, n_kv * hd), F32)],
        compiler_params=_cparams(1), name="swa_stacked%d" % int(stacked),
    )(sinks, q_arr, kc_arr, vc_arr, kp_arr, vp_arr, qg.reshape(1, hd), kg.reshape(1, hd))


def _moe_up_kernel(te_ref, nu_ref, xs_ref, wg_ref, wu_ref, rw_ref, hid_ref, wgb_ref, wub_ref):
    i = pl.program_id(0)
    changed = jnp.logical_or(i == 0, te_ref[i] != te_ref[jnp.maximum(i - 1, 0)])

    @pl.when(changed)
    def _():
        wgb_ref[...] = wg_ref[...].astype(BF16)
        wub_ref[...] = wu_ref[...].astype(BF16)

    @pl.when(i < nu_ref[0])
    def _():
        x = xs_ref[...]
        g = jnp.dot(x, wgb_ref[...], preferred_element_type=F32)
        u = jnp.dot(x, wub_ref[...], preferred_element_type=F32)
        hid_ref[...] = ((g / (1.0 + jnp.exp(-g))) * u * rw_ref[...]).astype(BF16)

    @pl.when(i >= nu_ref[0])
    def _():
        hid_ref[...] = jnp.zeros_like(hid_ref)


def _moe_down_kernel(te_ref, nu_ref, hid_ref, wd_ref, out_ref, wdb_ref):
    i = pl.program_id(0)
    changed = jnp.logical_or(i == 0, te_ref[i] != te_ref[jnp.maximum(i - 1, 0)])

    @pl.when(changed)
    def _():
        wdb_ref[...] = wd_ref[...].astype(BF16)

    @pl.when(i < nu_ref[0])
    def _():
        out_ref[...] = jnp.dot(hid_ref[...], wdb_ref[...], preferred_element_type=F32)

    @pl.when(i >= nu_ref[0])
    def _():
        out_ref[...] = jnp.zeros_like(out_ref)


def _moe(x, h, logits, b_group, b_expert, w_gate, w_up, w_down, layer, next_gain=None, split=None):
    m, d = h.shape
    n_groups = b_group.shape[0]
    n_experts = b_expert.shape[0]
    per_group = n_experts // n_groups
    f = w_gate.shape[2] // n_experts
    top_k = 2

    g_logit = logits[:, :n_groups] + b_group
    g_prob = jax.nn.softmax(g_logit, axis=-1)
    g_idx = jnp.argmax(g_logit, axis=-1)
    g_gate = jnp.take_along_axis(g_prob, g_idx[:, None], axis=-1)
    e_logit = (logits[:, n_groups:n_groups + n_experts] + b_expert).reshape(m, n_groups, per_group)
    e_in = jnp.take_along_axis(e_logit, g_idx[:, None, None], axis=1)[:, 0]
    i1 = jnp.argmax(e_in, axis=-1)
    rest = jnp.where(jnp.arange(per_group)[None, :] == i1[:, None], -jnp.inf, e_in)
    i2 = jnp.argmax(rest, axis=-1)
    top_v = jnp.stack([jnp.max(e_in, axis=-1), jnp.max(rest, axis=-1)], axis=-1)
    top_i = jnp.stack([i1, i2], axis=-1)
    top_w = jax.nn.softmax(top_v, axis=-1) * g_gate
    eid = (g_idx[:, None] * per_group + top_i).astype(jnp.int32)

    n_pairs = m * top_k
    n_tiles = (n_pairs + n_experts * (MOE_TM - 1)) // MOE_TM + 1
    n_rows = n_tiles * MOE_TM
    flat_e = eid.reshape(-1)
    order = jnp.argsort(flat_e, stable=True).astype(jnp.int32)
    rank = jnp.argsort(order).astype(jnp.int32)
    counts = jnp.sum(flat_e[None, :] == jnp.arange(n_experts, dtype=jnp.int32)[:, None], axis=1,
                     dtype=jnp.int32)
    padded = ((counts + MOE_TM - 1) // MOE_TM) * MOE_TM
    pad_end = jnp.cumsum(padded)
    pad_start = pad_end - padded
    start = jnp.cumsum(counts) - counts
    n_used = (pad_end[-1] // MOE_TM).astype(jnp.int32)
    tile_row = jnp.minimum(jnp.arange(n_tiles, dtype=jnp.int32), n_used - 1) * MOE_TM
    tile_e = jnp.minimum(jnp.sum(pad_end[None, :] <= tile_row[:, None], axis=1, dtype=jnp.int32),
                         n_experts - 1)
    row = jnp.arange(n_rows, dtype=jnp.int32)
    row_e = jnp.repeat(tile_e, MOE_TM)
    in_group = row - pad_start[row_e]
    live = (in_group < counts[row_e]) & (row < n_used * MOE_TM)
    pair = order[jnp.clip(start[row_e] + in_group, 0, n_pairs - 1)]
    row_token = jnp.where(live, pair // top_k, row % m)
    row_w = jnp.where(live, top_w.reshape(-1)[pair], 0.0)
    pos = (pad_start[flat_e] + rank - start[flat_e]).reshape(m, top_k)
    n_used = n_used.reshape(1)

    xs = jnp.take(h, row_token, axis=0, mode="clip")
    hid = pl.pallas_call(
        _moe_up_kernel,
        grid_spec=pltpu.PrefetchScalarGridSpec(
            num_scalar_prefetch=2, grid=(n_tiles,),
            in_specs=[pl.BlockSpec((MOE_TM, d), lambda i, te, nu: (i, 0)),
                      pl.BlockSpec((None, d, f), lambda i, te, nu: (layer, 0, te[i])),
                      pl.BlockSpec((None, d, f), lambda i, te, nu: (layer, 0, te[i])),
                      pl.BlockSpec((MOE_TM, 1), lambda i, te, nu: (i, 0))],
            out_specs=pl.BlockSpec((MOE_TM, f), lambda i, te, nu: (i, 0)),
            scratch_shapes=[pltpu.VMEM((d, f), BF16), pltpu.VMEM((d, f), BF16)]),
        out_shape=jax.ShapeDtypeStruct((n_rows, f), BF16),
        compiler_params=_cparams(1), name="moe_up",
    )(tile_e, n_used, xs, w_gate, w_up, row_w.reshape(n_rows, 1))
    rows = pl.pallas_call(
        _moe_down_kernel,
        grid_spec=pltpu.PrefetchScalarGridSpec(
            num_scalar_prefetch=2, grid=(n_tiles,),
            in_specs=[pl.BlockSpec((MOE_TM, f), lambda i, te, nu: (i, 0)),
                      pl.BlockSpec((None, f, d), lambda i, te, nu: (layer, te[i], 0))],
            out_specs=pl.BlockSpec((MOE_TM, d), lambda i, te, nu: (i, 0)),
            scratch_shapes=[pltpu.VMEM((f, d), BF16)]),
        out_shape=jax.ShapeDtypeStruct((n_rows, d), F32),
        compiler_params=_cparams(1), name="moe_down",
    )(tile_e, n_used, hid, w_down)

    return _moe_combine(x, rows, pos.reshape(-1), next_gain, split)


def _combine_kernel(pos_ref, x_ref, rows_hbm, *refs, top_k, with_norm, split_tile):
    if with_norm:
        g_ref, xo_ref, h_ref, buf, sem = refs
    else:
        lo_ref, hi_ref, buf, sem = refs
    i = pl.program_id(0)
    n_tiles = pl.num_programs(0)
    tm = x_ref.shape[0]
    slot = i % 2

    def gather_copy(tile, t, k, slot_):
        r = pos_ref[(tile * tm + t) * top_k + k]
        return pltpu.make_async_copy(rows_hbm.at[pl.ds(r, 1)],
                                     buf.at[slot_, pl.ds(k * tm + t, 1)], sem.at[slot_])

    def start_tile(tile, slot_):
        def body(t, carry):
            for k in range(top_k):
                gather_copy(tile, t, k, slot_).start()
            return carry
        lax.fori_loop(0, tm, body, 0, unroll=8)

    @pl.when(i == 0)
    def _():
        start_tile(0, 0)

    @pl.when(i + 1 < n_tiles)
    def _():
        start_tile(i + 1, 1 - slot)

    pltpu.make_async_copy(rows_hbm.at[pl.ds(0, top_k * tm)], buf.at[slot], sem.at[slot]).wait()
    y = x_ref[...]
    for k in range(top_k):
        y = y + buf[slot, k * tm:(k + 1) * tm]
    if with_norm:
        xo_ref[...] = y
        h_ref[...] = (y * lax.rsqrt(jnp.mean(y * y, axis=-1, keepdims=True) + RMS_EPS)
                      * g_ref[...]).astype(BF16)
    else:
        @pl.when(i < split_tile)
        def _():
            lo_ref[...] = y

        @pl.when(i >= split_tile)
        def _():
            hi_ref[...] = y


def _moe_combine(x, rows, pos, next_gain, split):
    m, d = x.shape
    top_k = pos.shape[0] // m
    tm = LANE
    assert m % tm == 0
    with_norm = next_gain is not None
    x_spec = pl.BlockSpec((tm, d), lambda i, p: (i, 0))
    in_specs = [x_spec, pl.BlockSpec(memory_space=pl.ANY)]
    args = [x, rows]
    if with_norm:
        split_tile = 0
        in_specs.append(pl.BlockSpec((1, d), lambda i, p: (0, 0)))
        args.append(next_gain.reshape(1, d))
        out_specs = [x_spec, x_spec]
        out_shape = [jax.ShapeDtypeStruct((m, d), F32), jax.ShapeDtypeStruct((m, d), BF16)]
    else:
        assert split % tm == 0
        split_tile = split // tm
        out_specs = [pl.BlockSpec((tm, d), lambda i, p: (jnp.minimum(i, split_tile - 1), 0)),
                     pl.BlockSpec((tm, d), lambda i, p: (jnp.maximum(i - split_tile, 0), 0))]
        out_shape = [jax.ShapeDtypeStruct((split, d), F32), jax.ShapeDtypeStruct((m - split, d), F32)]
    return pl.pallas_call(
        functools.partial(_combine_kernel, top_k=top_k, with_norm=with_norm, split_tile=split_tile),
        grid_spec=pltpu.PrefetchScalarGridSpec(
            num_scalar_prefetch=1, grid=(m // tm,), in_specs=in_specs, out_specs=out_specs,
            scratch_shapes=[pltpu.VMEM((2, top_k * tm, d), F32), pltpu.SemaphoreType.DMA((2,))]),
        out_shape=out_shape, compiler_params=_cparams(1),
        name="moe_combine_norm" if with_norm else "moe_combine_split",
    )(pos, *args)


def _param_pairs(a, n_batch, n_heads, n):
    return jnp.tile(a.reshape(n_heads, n).T, (1, n_batch))


def _even_layer(x, h, i, groups, state_ret, state_rwkv, state_shift, wts):
    (a_w_in, a_w_out, ret_norm_g, rwkv_mu, rwkv_w0, rwkv_w2, rwkv_a0, rwkv_a2, rwkv_g2,
     rwkv_kk_scale, rwkv_ka, rwkv_rk, rwkv_lnx_g, rwkv_lnx_b) = [w[i] for w in wts]
    ret_heads, ret_dk = state_ret.shape[2], state_ret.shape[3]
    rw_heads, rw_n = state_rwkv.shape[2], state_rwkv.shape[3]
    ret_w = ret_heads * ret_dk
    rw_w = rw_heads * rw_n
    lora_w, lora_a, lora_g = rwkv_w2.shape[0], rwkv_a2.shape[0], rwkv_g2.shape[0]
    assert lora_w == LANE and lora_a == LANE and lora_g <= LANE and ret_w == rw_w
    n_main = 4 * ret_w + 3 * rw_w
    n_shift = 3 * rw_w + lora_w + lora_a + lora_g
    (pb, pt), (sb, st) = groups
    m_p, m_s = pb * pt, sb * st

    p = _matmul([h], a_w_in, n_main, name="a_in")
    tail_cols = 3 * LANE
    w_tail = jnp.pad(a_w_in[:, n_main:], ((0, 0), (0, tail_cols - (lora_w + lora_a + lora_g))))
    tail = _matmul([h], w_tail, tail_cols, tn=tail_cols, name="a_in_tail")

    o_ret_p, ret_p = _retention_prompt(p, ret_norm_g, pb, pt, ret_heads, ret_dk)
    o_ret_s, ret_s = _retention_sample(p, ret_norm_g, state_ret[i], m_p, sb, st, ret_heads, ret_dk)

    pad_g = lambda a: jnp.pad(a, ((0, LANE - a.shape[0]), (0, 0)))
    mu = rwkv_mu
    mu_tail = jnp.pad(mu[3 * rw_w:], (0, tail_cols - (n_shift - 3 * rw_w)))
    prep_params = [mu[:rw_w].reshape(1, -1), mu[rw_w:2 * rw_w].reshape(1, -1),
                   mu[2 * rw_w:3 * rw_w].reshape(1, -1), mu_tail.reshape(1, -1),
                   rwkv_w0.reshape(1, -1), rwkv_w2, rwkv_a0.reshape(1, -1), rwkv_a2,
                   pad_g(rwkv_g2)]
    cb0 = (4 * ret_w) // rw_w
    shift0 = state_shift[i]
    sh_main = jnp.repeat(shift0[:, :3 * rw_w], st, axis=0)
    sh_tail = jnp.repeat(jnp.pad(shift0[:, 3 * rw_w:], ((0, 0), (0, tail_cols - (n_shift - 3 * rw_w)))),
                         st, axis=0)
    seq_p = _rwkv_prep(p, tail, 0, m_p, cb0, rw_w, prep_params, None, pt, st)
    seq_s = _rwkv_prep(p, tail, m_p, m_s, cb0, rw_w, prep_params, (sh_main, sh_tail), pt, st)

    outs = []
    for seqs, nb, nt, s0 in ((seq_p, pb, pt, None), (seq_s, sb, st, state_rwkv[i])):
        scan_params = [_param_pairs(a, nb, rw_heads, rw_n)
                       for a in (rwkv_kk_scale, rwkv_ka, rwkv_rk, rwkv_lnx_g, rwkv_lnx_b)]
        if s0 is None:
            s0_l = jnp.zeros((rw_n, rw_n, nb * rw_heads), F32)
        else:
            s0_l = s0.transpose(2, 3, 0, 1).reshape(rw_n, rw_n, nb * rw_heads)
        y, s_t = _rwkv_scan(seqs, scan_params, s0_l, nb, nt, rw_heads, rw_n)
        s_t = s_t.reshape(rw_n, rw_n, nb, rw_heads).transpose(2, 3, 0, 1)
        outs.append((y, s_t))
    (y_p, rwkv_p), (y_s, rwkv_s) = outs

    o_ret = jnp.concatenate([o_ret_p, o_ret_s], axis=0)
    y = jnp.concatenate([y_p, y_s], axis=0)
    x = _matmul([o_ret, y], a_w_out, a_w_out.shape[1], res=x, name="a_out")

    def last_ps(row0, nb, nt):
        last = lambda a, lo, hi: lax.slice(a, (row0 + nt - 1, lo), (row0 + nb * nt, hi), (nt, 1))
        return jnp.concatenate([last(p, 4 * ret_w, n_main), last(tail, 0, n_shift - 3 * rw_w)],
                               axis=-1)

    return x, (ret_p, rwkv_p, last_ps(0, pb, pt)), (ret_s, rwkv_s, last_ps(m_p, sb, st))


def _odd_layer(x, h, i, groups, cache_k, cache_v, wts):
    c_w_in, c_w_out, c_q_norm_g, c_k_norm_g, c_sinks = [w[i] for w in wts]
    win, n_kv, hd = cache_k.shape[2], cache_k.shape[3], cache_k.shape[4]
    n_heads = c_sinks.shape[0]
    group = n_heads // n_kv
    mix_c = n_heads * hd
    kv_w = n_kv * hd
    (pb, pt), (sb, st) = groups
    m_p = pb * pt
    assert pt % win == 0 and mix_c % kv_w == 0
    p = _matmul([h], c_w_in, mix_c + 2 * kv_w, name="c_in")
    kcb, vcb = mix_c // kv_w, mix_c // kv_w + 1

    nb_p = m_p // win
    maps = (lambda n: (n, 0), lambda n: (n, kcb), lambda n: (n, vcb),
            lambda n: (jnp.maximum(n - 1, 0), kcb), lambda n: (jnp.maximum(n - 1, 0), vcb))
    o_p, kn_p = _swa(p, p, p, p, p, maps, nb_p, (win, mix_c), win, False, win, c_q_norm_g,
                     c_k_norm_g, c_sinks, n_kv, group, hd, True, False, pt // win)

    tq = 16
    ps = p[m_p:].reshape(sb, st, -1)
    q_s = jnp.pad(ps[:, :, :mix_c].reshape(sb, st, n_kv, group, hd),
                  ((0, 0), (0, tq - st), (0, 0), (0, 0), (0, 0)))
    q_s = q_s.transpose(0, 3, 1, 2, 4).reshape(sb * group * tq, kv_w)
    kv_pad = lambda a: jnp.pad(a, ((0, 0), (0, win - st), (0, 0))).reshape(sb * win, kv_w)
    k_s = kv_pad(ps[:, :, mix_c:mix_c + kv_w])
    v_s = kv_pad(ps[:, :, mix_c + kv_w:])
    kc0 = cache_k[i].reshape(sb * win, kv_w)
    vc0 = cache_v[i].reshape(sb * win, kv_w)
    same = lambda n: (n, 0)
    o_s, kn_s = _swa(q_s, k_s, v_s, kc0, vc0, (same,) * 5, sb, (group * tq, kv_w), tq, True, win,
                     c_q_norm_g, c_k_norm_g, c_sinks, n_kv, group, hd, False, True, 1)
    o_s = o_s.reshape(sb, group, tq, n_kv, hd)[:, :, :st].transpose(0, 2, 3, 1, 4).reshape(
        sb * st, mix_c)

    x = _matmul([jnp.concatenate([o_p, o_s], axis=0)], c_w_out, c_w_out.shape[1], res=x, name="c_out")

    new_k_p = kn_p.reshape(pb, pt, n_kv, hd)[:, -win:]
    new_v_p = p[:m_p, mix_c + kv_w:].reshape(pb, pt, n_kv, hd)[:, -win:]
    kn_s = kn_s.reshape(sb, win, n_kv, hd)[:, :st]
    new_k_s = jnp.concatenate([cache_k[i], kn_s], axis=1)[:, -win:]
    new_v_s = jnp.concatenate([cache_v[i], ps[:, :, mix_c + kv_w:].reshape(sb, st, n_kv, hd)],
                              axis=1)[:, -win:]
    return x, (new_k_p, new_v_p), (new_k_s, new_v_s)


def kernel(x_prompt, x_sample, state_ret, state_rwkv, state_rwkv_shift, cache_swa_k, cache_swa_v, norm_mix_g, norm_ffn_g, a_w_in, a_w_out, ret_norm_g, rwkv_mu, rwkv_w0, rwkv_w2, rwkv_a0, rwkv_a2, rwkv_g2, rwkv_kk_scale, rwkv_ka, rwkv_rk, rwkv_lnx_g, rwkv_lnx_b, c_w_in, c_w_out, c_q_norm_g, c_k_norm_g, c_sinks, moe_w_group, moe_b_group, moe_w_expert, moe_b_expert, moe_w_gate, moe_w_up, moe_w_down):
    pb, pt, d = x_prompt.shape
    sb, st, _ = x_sample.shape
    groups = ((pb, pt), (sb, st))
    depth = norm_mix_g.shape[0]
    a_wts = (a_w_in, a_w_out, ret_norm_g, rwkv_mu, rwkv_w0, rwkv_w2, rwkv_a0, rwkv_a2, rwkv_g2,
             rwkv_kk_scale, rwkv_ka, rwkv_rk, rwkv_lnx_g, rwkv_lnx_b)
    c_wts = (c_w_in, c_w_out, c_q_norm_g, c_k_norm_g, c_sinks)
    st_p = [[] for _ in range(5)]
    st_s = [[] for _ in range(5)]
    for l in range(depth):
        i = l // 2
        if l == 0:
            h, x = _rms_norm_first(x_prompt.reshape(pb * pt, d), x_sample.reshape(sb * st, d),
                                   norm_mix_g[l])
        else:
            h = h_next
        if l % 2 == 0:
            x, sp, ss = _even_layer(x, h, i, groups, state_ret, state_rwkv, state_rwkv_shift, a_wts)
            for j in range(3):
                st_p[j].append(sp[j])
                st_s[j].append(ss[j])
        else:
            x, sp, ss = _odd_layer(x, h, i, groups, cache_swa_k, cache_swa_v, c_wts)
            for j in range(2):
                st_p[3 + j].append(sp[j])
                st_s[3 + j].append(ss[j])
        n_router = moe_w_group.shape[2] + moe_w_expert.shape[2]
        w_router = jnp.pad(jnp.concatenate([moe_w_group[l], moe_w_expert[l]], axis=1),
                           ((0, 0), (0, LANE - n_router)))
        h, logits = _rms_norm(x, norm_ffn_g[l], w_router)
        moe_args = (x, h, logits, moe_b_group[l], moe_b_expert[l], moe_w_gate, moe_w_up, moe_w_down, l)
        if l < depth - 1:
            x, h_next = _moe(*moe_args, next_gain=norm_mix_g[l + 1])
        else:
            y_p, y_s = _moe(*moe_args, split=pb * pt)
    y_prompt = y_p.reshape(pb, pt, d)
    y_sample = y_s.reshape(sb, st, d)
    stack = lambda s: s[0][None] if len(s) == 1 else jnp.stack(s)
    return (y_prompt, y_sample) + tuple(stack(s) for s in st_p) + tuple(stack(s) for s in st_s)
```

```python
import functools

import jax
import jax.numpy as jnp
from jax import lax
from jax.experimental import pallas as pl
from jax.experimental.pallas import tpu as pltpu

F32 = jnp.float32
BF16 = jnp.bfloat16

RMS_EPS = 1e-6
RWKV_LN_EPS = 64e-5
RET_CHUNK = 128

LANE = 128
VMEM_LIMIT = 56 * 1024 * 1024

MM_TM = 640
MM_TN = 512
NORM_TM = 320
MOE_TM = 256
SCAN_TC = 32
RET_HEADS_PER_STEP = 4


def _cparams(n_axes):
    return pltpu.CompilerParams(dimension_semantics=("arbitrary",) * n_axes,
                                vmem_limit_bytes=VMEM_LIMIT)


def _norm_kernel(x_ref, g_ref, h_ref):
    x = x_ref[...]
    y = x * lax.rsqrt(jnp.mean(x * x, axis=-1, keepdims=True) + RMS_EPS) * g_ref[...]
    h_ref[...] = y.astype(BF16)


def _norm_router_kernel(x_ref, g_ref, wr_ref, h_ref, logit_ref):
    x = x_ref[...]
    y = x * lax.rsqrt(jnp.mean(x * x, axis=-1, keepdims=True) + RMS_EPS) * g_ref[...]
    h_ref[...] = y.astype(BF16)
    logit_ref[...] = jnp.dot(y, wr_ref[...], precision=lax.Precision.HIGHEST,
                             preferred_element_type=F32)


def _norm_first_kernel(xp_ref, xs_ref, g_ref, h_ref, x_ref, *, n_prompt_tiles):
    x = jnp.where(pl.program_id(0) < n_prompt_tiles, xp_ref[...], xs_ref[...])
    x_ref[...] = x
    y = x * lax.rsqrt(jnp.mean(x * x, axis=-1, keepdims=True) + RMS_EPS) * g_ref[...]
    h_ref[...] = y.astype(BF16)


def _rms_norm_first(xp, xs, g):
    (m_p, d), m_s = xp.shape, xs.shape[0]
    tm = LANE
    assert m_p % tm == 0 and m_s % tm == 0
    npt = m_p // tm
    spec = pl.BlockSpec((tm, d), lambda i: (i, 0))
    return pl.pallas_call(
        functools.partial(_norm_first_kernel, n_prompt_tiles=npt), grid=((m_p + m_s) // tm,),
        in_specs=[pl.BlockSpec((tm, d), lambda i: (jnp.minimum(i, npt - 1), 0)),
                  pl.BlockSpec((tm, d), lambda i: (jnp.maximum(i - npt, 0), 0)),
                  pl.BlockSpec((1, d), lambda i: (0, 0))],
        out_specs=[spec, spec],
        out_shape=[jax.ShapeDtypeStruct((m_p + m_s, d), BF16),
                   jax.ShapeDtypeStruct((m_p + m_s, d), F32)],
        compiler_params=_cparams(1), name="rms_norm_first")(xp, xs, g.reshape(1, d))


def _rms_norm(x, g, w_router=None):
    m, d = x.shape
    tm = NORM_TM if m % NORM_TM == 0 else m
    grid = (m // tm,)
    x_spec = pl.BlockSpec((tm, d), lambda i: (i, 0))
    g_spec = pl.BlockSpec((1, d), lambda i: (0, 0))
    if w_router is None:
        return pl.pallas_call(
            _norm_kernel, grid=grid, in_specs=[x_spec, g_spec], out_specs=x_spec,
            out_shape=jax.ShapeDtypeStruct((m, d), BF16), compiler_params=_cparams(1),
            name="rms_norm")(x, g.reshape(1, d))
    nr = w_router.shape[1]
    return pl.pallas_call(
        _norm_router_kernel, grid=grid,
        in_specs=[x_spec, g_spec, pl.BlockSpec((d, nr), lambda i: (0, 0))],
        out_specs=[x_spec, pl.BlockSpec((tm, nr), lambda i: (i, 0))],
        out_shape=[jax.ShapeDtypeStruct((m, d), BF16), jax.ShapeDtypeStruct((m, nr), F32)],
        compiler_params=_cparams(1), name="rms_norm_router")(x, g.reshape(1, d), w_router)


def _mm_kernel(*refs, n_parts, has_res, cols_valid):
    x_refs = refs[:n_parts]
    w_refs = refs[n_parts:2 * n_parts]
    pos = 2 * n_parts
    res_ref = refs[pos] if has_res else None
    pos += int(has_res)
    o_ref = refs[pos]
    wb_refs = refs[pos + 1:]

    @pl.when(pl.program_id(1) == 0)
    def _():
        for w_ref, wb_ref in zip(w_refs, wb_refs):
            w = w_ref[...]
            if cols_valid is not None:
                col = lax.broadcasted_iota(jnp.int32, w.shape, 1) + pl.program_id(0) * w.shape[1]
                w = jnp.where(col < cols_valid, w, 0.0)
            wb_ref[...] = w.astype(BF16)

    acc = None
    for x_ref, wb_ref in zip(x_refs, wb_refs):
        d = jnp.dot(x_ref[...], wb_ref[...], preferred_element_type=F32)
        acc = d if acc is None else acc + d
    if has_res:
        acc = acc + res_ref[...]
    o_ref[...] = acc


def _matmul(xs, w, n_out, col_block0=0, tn=MM_TN, res=None, name="proj"):
    over = (col_block0 * tn + n_out) > w.shape[1]
    cols_valid = w.shape[1] - col_block0 * tn if over else None
    m = xs[0].shape[0]
    tm = MM_TM if m % MM_TM == 0 else m
    assert n_out % tn == 0
    grid = (n_out // tn, m // tm)
    in_specs, w_specs, scratch = [], [], []
    row = 0
    for x in xs:
        kp = x.shape[1]
        in_specs.append(pl.BlockSpec((tm, kp), lambda j, i: (i, 0)))
        assert row % kp == 0
        rb = row // kp
        w_specs.append(pl.BlockSpec((kp, tn), lambda j, i, rb=rb: (rb, j + col_block0)))
        scratch.append(pltpu.VMEM((kp, tn), BF16))
        row += kp
    assert row == w.shape[0]
    args = list(xs) + [w] * len(xs)
    in_specs = in_specs + w_specs
    if res is not None:
        in_specs.append(pl.BlockSpec((tm, tn), lambda j, i: (i, j)))
        args.append(res)
    return pl.pallas_call(
        functools.partial(_mm_kernel, n_parts=len(xs), has_res=res is not None,
                          cols_valid=cols_valid),
        grid=grid, in_specs=in_specs,
        out_specs=pl.BlockSpec((tm, tn), lambda j, i: (i, j)),
        out_shape=jax.ShapeDtypeStruct((m, n_out), F32),
        scratch_shapes=scratch, compiler_params=_cparams(2), name=name)(*args)


def _ret_finish(o, g, gain):
    o = o * lax.rsqrt(jnp.mean(o * o, axis=-1, keepdims=True) + RMS_EPS) * gain
    return ((g / (1.0 + jnp.exp(-g))) * o).astype(BF16)


def _ret_prompt_kernel(q_ref, k_ref, v_ref, g_ref, gain_ref, lg_ref, o_ref, s_ref, *, dk):
    c = pl.program_id(2)
    chunk = q_ref.shape[0]

    @pl.when(c == 0)
    def _():
        s_ref[...] = jnp.zeros_like(s_ref)

    row = lax.broadcasted_iota(jnp.int32, (chunk, chunk), 0)
    col = lax.broadcasted_iota(jnp.int32, (chunk, chunk), 1)
    diff = (row - col).astype(F32)
    causal = diff >= 0
    dist = jnp.where(causal, diff, 0.0)
    pos = lax.broadcasted_iota(jnp.int32, (chunk, 1), 0).astype(F32)

    for hh in range(s_ref.shape[0]):
        sl = slice(hh * dk, (hh + 1) * dk)
        lg = lg_ref[hh][:, :1]
        decay_mask = jnp.where(causal, jnp.exp(dist * lg), 0.0)
        q_decay = jnp.exp((pos + 1.0) * lg)
        k_decay = jnp.exp((chunk - 1.0 - pos) * lg)
        chunk_decay = jnp.exp(chunk * lg)
        q = q_ref[:, sl]
        k = k_ref[:, sl] * (dk ** -0.5)
        vb = v_ref[:, sl].astype(BF16)
        s = s_ref[hh]
        scores = lax.dot_general(q.astype(BF16), k.astype(BF16), (((1,), (1,)), ((), ())),
                                 preferred_element_type=F32) * decay_mask
        inner = jnp.dot(scores.astype(BF16), vb, preferred_element_type=F32)
        cross = jnp.dot((q * q_decay).astype(BF16), s.astype(BF16), preferred_element_type=F32)
        s_ref[hh] = chunk_decay * s + lax.dot_general(
            (k * k_decay).astype(BF16), vb, (((0,), (0,)), ((), ())), preferred_element_type=F32)
        o_ref[:, sl] = _ret_finish(inner + cross, g_ref[:, sl], gain_ref[hh])


def _ret_sample_kernel(q_ref, k_ref, v_ref, g_ref, gain_ref, lg_ref, s0_ref, o_ref, s_ref, acc_ref,
                       *, dk, t_len):
    b = pl.program_id(1)
    rows = q_ref.shape[0]
    lg = lg_ref[:, :1]
    r_b = lax.broadcasted_iota(jnp.int32, (rows, 1), 0) // t_len
    r_t = (lax.broadcasted_iota(jnp.int32, (rows, 1), 0) % t_len).astype(F32)
    q = q_ref[...]
    k = k_ref[...] * (dk ** -0.5)
    vb = v_ref[...].astype(BF16)

    @pl.when(b == 0)
    def _():
        row = lax.broadcasted_iota(jnp.int32, (rows, rows), 0)
        col = lax.broadcasted_iota(jnp.int32, (rows, rows), 1)
        diff = (row % t_len - col % t_len).astype(F32)
        ok = (row // t_len == col // t_len) & (diff >= 0)
        mask = jnp.where(ok, jnp.exp(jnp.where(ok, diff, 0.0) * lg), 0.0)
        scores = lax.dot_general(q.astype(BF16), k.astype(BF16), (((1,), (1,)), ((), ())),
                                 preferred_element_type=F32) * mask
        acc_ref[...] = jnp.dot(scores.astype(BF16), vb, preferred_element_type=F32)

    sel = r_b == b
    s0 = s0_ref[...]
    cross = jnp.dot((q * jnp.exp((r_t + 1.0) * lg)).astype(BF16), s0.astype(BF16),
                    preferred_element_type=F32)
    acc_ref[...] += jnp.where(sel, cross, 0.0)
    k_decay = jnp.where(sel, jnp.exp((t_len - 1.0 - r_t) * lg), 0.0)
    s_ref[...] = jnp.exp(t_len * lg) * s0 + lax.dot_general(
        (k * k_decay).astype(BF16), vb, (((0,), (0,)), ((), ())), preferred_element_type=F32)

    @pl.when(b == pl.num_programs(1) - 1)
    def _():
        o_ref[...] = _ret_finish(acc_ref[...], g_ref[...], gain_ref[...])


def _ret_log_decay(n_heads):
    lg = jnp.log(1.0 - 2.0 ** (-5.0 - jnp.arange(n_heads, dtype=F32)))
    return jnp.broadcast_to(lg[:, None, None], (n_heads, 1, LANE))


def _retention_prompt(p, gain, n_batch, seq, n_heads, dk):
    nc = seq // RET_CHUNK
    hps = RET_HEADS_PER_STEP
    assert n_heads % hps == 0
    ng = n_heads // hps
    col = lambda grp: (lambda b, h, c: (b * nc + c, grp * ng + h))
    blk = lambda grp: pl.BlockSpec((RET_CHUNK, hps * dk), col(grp))
    return pl.pallas_call(
        functools.partial(_ret_prompt_kernel, dk=dk),
        grid=(n_batch, ng, nc),
        in_specs=[blk(0), blk(1), blk(2), blk(3),
                  pl.BlockSpec((hps, 1, dk), lambda b, h, c: (h, 0, 0)),
                  pl.BlockSpec((hps, 1, LANE), lambda b, h, c: (h, 0, 0))],
        out_specs=[pl.BlockSpec((RET_CHUNK, hps * dk), lambda b, h, c: (b * nc + c, h)),
                   pl.BlockSpec((None, hps, dk, dk), lambda b, h, c: (b, h, 0, 0))],
        out_shape=[jax.ShapeDtypeStruct((n_batch * seq, n_heads * dk), BF16),
                   jax.ShapeDtypeStruct((n_batch, n_heads, dk, dk), F32)],
        compiler_params=_cparams(3), name="retention_prompt",
    )(p, p, p, p, gain.reshape(n_heads, 1, dk), _ret_log_decay(n_heads))


def _retention_sample(p, gain, s0, row0, n_batch, t_len, n_heads, dk):
    rows = n_batch * t_len
    rb = row0 // rows
    blk = lambda grp: pl.BlockSpec((rows, dk), lambda h, b, grp=grp: (rb, grp * n_heads + h))
    return pl.pallas_call(
        functools.partial(_ret_sample_kernel, dk=dk, t_len=t_len),
        grid=(n_heads, n_batch),
        in_specs=[blk(0), blk(1), blk(2), blk(3),
                  pl.BlockSpec((None, 1, dk), lambda h, b: (h, 0, 0)),
                  pl.BlockSpec((None, 1, LANE), lambda h, b: (h, 0, 0)),
                  pl.BlockSpec((None, None, dk, dk), lambda h, b: (b, h, 0, 0))],
        out_specs=[pl.BlockSpec((rows, dk), lambda h, b: (0, h)),
                   pl.BlockSpec((None, None, dk, dk), lambda h, b: (b, h, 0, 0))],
        out_shape=[jax.ShapeDtypeStruct((rows, n_heads * dk), BF16),
                   jax.ShapeDtypeStruct((n_batch, n_heads, dk, dk), F32)],
        scratch_shapes=[pltpu.VMEM((rows, dk), F32)],
        compiler_params=_cparams(2), name="retention_sample",
    )(p, p, p, p, gain.reshape(n_heads, 1, dk), _ret_log_decay(n_heads), s0)


def _sigmoid(x):
    return 1.0 / (1.0 + jnp.exp(-x))


def _rwkv_prep_kernel(*refs, sample, seq, t_len):
    if sample:
        (r_c, k_c, v_c, t_c, r_s, k_s, v_s, t_s,
         mu_r, mu_k, mu_v, mu_t, w0, w2, a0, a2, g2,
         r_o, w_o, k_o, v_o, a_o, g_o) = refs
    else:
        (r_c, k_c, v_c, t_c, r_p, k_p, v_p, t_p,
         mu_r, mu_k, mu_v, mu_t, w0, w2, a0, a2, g2,
         r_o, w_o, k_o, v_o, a_o, g_o) = refs
    tm = r_c.shape[0]
    row = lax.broadcasted_iota(jnp.int32, (tm, 1), 0)
    seq_start = (pl.program_id(0) * tm) % seq == 0

    def mix(cur_ref, other_ref, mu_ref):
        cur = cur_ref[...]
        rolled = pltpu.roll(cur, 1, 0)
        if sample:
            prev = jnp.where(row % t_len == 0, other_ref[...], rolled)
        else:
            last = other_ref[7:8, :]
            first = jnp.where(seq_start, jnp.zeros_like(last), last)
            prev = jnp.where(row == 0, first, rolled)
        return cur + (prev - cur) * mu_ref[...]

    other = (r_s, k_s, v_s, t_s) if sample else (r_p, k_p, v_p, t_p)
    r_o[...] = mix(r_c, other[0], mu_r)
    k_o[...] = mix(k_c, other[1], mu_k)
    v_o[...] = mix(v_c, other[2], mu_v)
    tail = mix(t_c, other[3], mu_t)
    lw, la, lgt = tail[:, :LANE], tail[:, LANE:2 * LANE], tail[:, 2 * LANE:3 * LANE]

    z = -(w0[...] + jnp.dot(jnp.tanh(lw).astype(BF16), w2[...].astype(BF16),
                            preferred_element_type=F32))
    softplus = jnp.maximum(z, 0.0) + jnp.log1p(jnp.exp(-jnp.abs(z)))
    w_o[...] = jnp.exp(-jnp.exp(-softplus - 0.5))
    a_o[...] = _sigmoid(a0[...] + jnp.dot(la.astype(BF16), a2[...].astype(BF16),
                                          preferred_element_type=F32))
    g_o[...] = jnp.dot(_sigmoid(lgt).astype(BF16), g2[...].astype(BF16),
                       preferred_element_type=F32)


def _rwkv_prep(p, tail, row0, n_rows, col_block0, width, params, shift_rows, seq, t_len):
    sample = shift_rows is not None
    tm = n_rows if sample else 256
    rb0 = row0 // tm
    grid = (n_rows // tm,)
    cur = lambda grp: pl.BlockSpec((tm, width), lambda i, grp=grp: (rb0 + i, col_block0 + grp))
    tail_w = tail.shape[1]
    in_specs = [cur(0), cur(1), cur(2), pl.BlockSpec((tm, tail_w), lambda i: (rb0 + i, 0))]
    args = [p, p, p, tail]
    if sample:
        sh_main, sh_tail = shift_rows
        in_specs += [pl.BlockSpec((tm, width), lambda i, grp=grp: (0, grp)) for grp in range(3)]
        in_specs += [pl.BlockSpec((tm, tail_w), lambda i: (0, 0))]
        args += [sh_main, sh_main, sh_main, sh_tail]
    else:
        per = tm // 8
        prev_rb = lambda i: jnp.maximum((rb0 + i) * per - 1, 0)
        in_specs += [pl.BlockSpec((8, width), lambda i, grp=grp: (prev_rb(i), col_block0 + grp))
                     for grp in range(3)]
        in_specs += [pl.BlockSpec((8, tail_w), lambda i: (prev_rb(i), 0))]
        args += [p, p, p, tail]
    full = lambda a: pl.BlockSpec(a.shape, lambda i: (0,) * a.ndim)
    in_specs += [full(a) for a in params]
    args += list(params)
    out_spec = pl.BlockSpec((tm, width), lambda i: (i, 0))
    return pl.pallas_call(
        functools.partial(_rwkv_prep_kernel, sample=sample, seq=seq, t_len=t_len),
        grid=grid, in_specs=in_specs, out_specs=[out_spec] * 6,
        out_shape=[jax.ShapeDtypeStruct((n_rows, width), F32)] * 6,
        compiler_params=_cparams(1), name="rwkv_prep_sample" if sample else "rwkv_prep_prompt",
    )(*args)


def _rwkv_scan_kernel(r_ref, w_ref, k_ref, v_ref, a_ref, g_ref,
                      kks_ref, ka_ref, rk_ref, lng_ref, lnb_ref, s0_ref,
                      y_ref, s_ref, ys_ref):
    n_v = s_ref.shape[0]

    @pl.when(pl.program_id(1) == 0)
    def _():
        s_ref[...] = s0_ref[...]

    def step(t, carry):
        r = r_ref[t]
        w = w_ref[t]
        kr = k_ref[t]
        v = v_ref[t]
        a = a_ref[t]
        kks = kr * kks_ref[...]
        kk = kks / jnp.maximum(jnp.sqrt(jnp.sum(kks * kks, axis=0, keepdims=True)), 1e-12)
        kf = kr * (1.0 + (a - 1.0) * ka_ref[...])
        kka = kk * a
        for vi in range(n_v):
            s = s_ref[vi]
            sa = jnp.sum(s * kk, axis=0, keepdims=True)
            s = s * w - sa * kka + v[vi:vi + 1, :] * kf
            s_ref[vi] = s
            ys_ref[vi:vi + 1, :] = jnp.sum(s * r, axis=0, keepdims=True)
        y = ys_ref[...]
        mu = jnp.mean(y, axis=0, keepdims=True)
        var = jnp.mean(jnp.square(y - mu), axis=0, keepdims=True)
        y = (y - mu) * lax.rsqrt(var + RWKV_LN_EPS) * lng_ref[...] + lnb_ref[...]
        bonus = jnp.sum(r * kf * rk_ref[...], axis=0, keepdims=True) * v
        y_ref[t] = ((y + bonus) * g_ref[t]).astype(y_ref.dtype)
        return carry

    lax.fori_loop(0, r_ref.shape[0], step, 0)


def _pairs_major(a, n_batch, t_len, n_heads, n):
    return a.reshape(n_batch, t_len, n_heads, n).transpose(1, 3, 0, 2).reshape(
        t_len, n, n_batch * n_heads)


def _rwkv_scan(seqs, params, s0, n_batch, t_len, n_heads, n):
    pairs = n_batch * n_heads
    tc = SCAN_TC if t_len % SCAN_TC == 0 else t_len
    grid = (pairs // LANE, t_len // tc)
    seq_spec = pl.BlockSpec((tc, n, LANE), lambda g, t: (t, 0, g))
    par_spec = pl.BlockSpec((n, LANE), lambda g, t: (0, g))
    st_spec = pl.BlockSpec((n, n, LANE), lambda g, t: (0, 0, g))
    y, s_t = pl.pallas_call(
        _rwkv_scan_kernel, grid=grid,
        in_specs=[seq_spec] * 6 + [par_spec] * 5 + [st_spec],
        out_specs=[seq_spec, st_spec],
        out_shape=[jax.ShapeDtypeStruct((t_len, n, pairs), BF16),
                   jax.ShapeDtypeStruct((n, n, pairs), F32)],
        scratch_shapes=[pltpu.VMEM((n, LANE), F32)],
        compiler_params=_cparams(2), name="rwkv_scan",
    )(*[_pairs_major(a, n_batch, t_len, n_heads, n) for a in seqs], *params, s0)
    y = y.reshape(t_len, n, n_batch, n_heads).transpose(2, 0, 3, 1)
    return y.reshape(n_batch * t_len, n_heads * n), s_t


def _swa_kernel(sink_ref, q_ref, kc_ref, vc_ref, kp_ref, vp_ref, qg_ref, kg_ref,
                o_ref, kn_ref, *, n_kv, group, hd, rows_per_head, stacked, norm_prev,
                first_has_prev, blocks_per_seq):
    rph = rows_per_head
    win = kc_ref.shape[0]
    n_heads = n_kv * group
    has_prev = jnp.logical_or(first_has_prev, pl.program_id(0) % blocks_per_seq > 0)

    def rms(x, g):
        return x * lax.rsqrt(jnp.mean(x * x, axis=-1, keepdims=True) + RMS_EPS) * g

    qpos = lax.broadcasted_iota(jnp.int32, (rph, 2 * win), 0)
    kpos = lax.broadcasted_iota(jnp.int32, (rph, 2 * win), 1) - win
    diff = qpos - kpos
    valid = (diff >= 0) & (diff < win) & (kpos >= jnp.where(has_prev, -win, 0))
    neg_dist = jnp.where(valid, -(diff.astype(F32)), -jnp.inf)
    qg = qg_ref[...]
    kg = kg_ref[...]

    for j in range(n_kv):
        sl = slice(j * hd, (j + 1) * hd)
        kc = rms(kc_ref[:, sl], kg)
        kn_ref[:, sl] = kc
        kp = kp_ref[:, sl]
        if norm_prev:
            kp = rms(kp, kg)
        kb = jnp.concatenate([kp, kc], axis=0).astype(BF16)
        vb = jnp.concatenate([vp_ref[:, sl], vc_ref[:, sl]], axis=0).astype(BF16)
        if stacked:
            q = q_ref[:, sl]
        else:
            q = jnp.concatenate([q_ref[:, (j * group + gi) * hd:(j * group + gi + 1) * hd]
                                 for gi in range(group)], axis=0)
        s = lax.dot_general(rms(q, qg).astype(BF16), kb, (((1,), (1,)), ((), ())),
                            preferred_element_type=F32)
        es, inv = [], []
        for gi in range(group):
            h = j * group + gi
            slope = 2.0 ** (-8.0 * (h + 1) / n_heads)
            sh = s[gi * rph:(gi + 1) * rph] * (hd ** -0.5) + slope * neg_dist
            sink = sink_ref[h]
            m = jnp.maximum(jnp.max(sh, axis=-1, keepdims=True), sink)
            e = jnp.exp(sh - m)
            inv.append(1.0 / (jnp.sum(e, axis=-1, keepdims=True) + jnp.exp(sink - m)))
            es.append(e.astype(BF16))
        o = jnp.dot(jnp.concatenate(es, axis=0), vb, preferred_element_type=F32)
        o = (o * jnp.concatenate(inv, axis=0)).astype(BF16)
        if stacked:
            o_ref[:, sl] = o
        else:
            for gi in range(group):
                h = j * group + gi
                o_ref[:, h * hd:(h + 1) * hd] = o[gi * rph:(gi + 1) * rph]


def _swa(q_arr, kc_arr, vc_arr, kp_arr, vp_arr, maps, n_blocks, q_block, rows_per_head, stacked,
         win, qg, kg, sinks, n_kv, group, hd, norm_prev, first_has_prev, blocks_per_seq):
    q_map, kc_map, vc_map, kp_map, vp_map = maps
    return pl.pallas_call(
        functools.partial(_swa_kernel, n_kv=n_kv, group=group, hd=hd, rows_per_head=rows_per_head,
                          stacked=stacked, norm_prev=norm_prev, first_has_prev=first_has_prev,
                          blocks_per_seq=blocks_per_seq),
        grid=(n_blocks,),
        in_specs=[pl.BlockSpec(memory_space=pltpu.SMEM),
                  pl.BlockSpec(q_block, q_map),
                  pl.BlockSpec((win, n_kv * hd), kc_map),
                  pl.BlockSpec((win, n_kv * hd), vc_map),
                  pl.BlockSpec((win, n_kv * hd), kp_map),
                  pl.BlockSpec((win, n_kv * hd), vp_map),
                  pl.BlockSpec((1, hd), lambda i: (0, 0)),
                  pl.BlockSpec((1, hd), lambda i: (0, 0))],
        out_specs=[pl.BlockSpec(q_block, lambda i: (i, 0)),
                   pl.BlockSpec((win, n_kv * hd), lambda i: (i, 0))],
        out_shape=[jax.ShapeDtypeStruct((n_blocks * q_block[0], q_block[1]), BF16),
                   jax.ShapeDtypeStruct((n_blocks * win, n_kv * hd), F32)],
        compiler_params=_cparams(1), name="swa_stacked%d" % int(stacked),
    )(sinks, q_arr, kc_arr, vc_arr, kp_arr, vp_arr, qg.reshape(1, hd), kg.reshape(1, hd))


def _moe_up_kernel(te_ref, nu_ref, xs_ref, wg_ref, wu_ref, rw_ref, hid_ref, wgb_ref, wub_ref):
    i = pl.program_id(0)
    changed = jnp.logical_or(i == 0, te_ref[i] != te_ref[jnp.maximum(i - 1, 0)])

    @pl.when(changed)
    def _():
        wgb_ref[...] = wg_ref[...].astype(BF16)
        wub_ref[...] = wu_ref[...].astype(BF16)

    @pl.when(i < nu_ref[0])
    def _():
        x = xs_ref[...]
        g = jnp.dot(x, wgb_ref[...], preferred_element_type=F32)
        u = jnp.dot(x, wub_ref[...], preferred_element_type=F32)
        hid_ref[...] = ((g / (1.0 + jnp.exp(-g))) * u * rw_ref[...]).astype(BF16)

    @pl.when(i >= nu_ref[0])
    def _():
        hid_ref[...] = jnp.zeros_like(hid_ref)


def _moe_down_kernel(te_ref, nu_ref, hid_ref, wd_ref, out_ref, wdb_ref):
    i = pl.program_id(0)
    changed = jnp.logical_or(i == 0, te_ref[i] != te_ref[jnp.maximum(i - 1, 0)])

    @pl.when(changed)
    def _():
        wdb_ref[...] = wd_ref[...].astype(BF16)

    @pl.when(i < nu_ref[0])
    def _():
        out_ref[...] = jnp.dot(hid_ref[...], wdb_ref[...], preferred_element_type=F32)

    @pl.when(i >= nu_ref[0])
    def _():
        out_ref[...] = jnp.zeros_like(out_ref)


def _moe(x, h, logits, b_group, b_expert, w_gate, w_up, w_down, layer, next_gain=None, split=None):
    m, d = h.shape
    n_groups = b_group.shape[0]
    n_experts = b_expert.shape[0]
    per_group = n_experts // n_groups
    f = w_gate.shape[2] // n_experts
    top_k = 2

    g_logit = logits[:, :n_groups] + b_group
    g_prob = jax.nn.softmax(g_logit, axis=-1)
    g_idx = jnp.argmax(g_logit, axis=-1)
    g_gate = jnp.take_along_axis(g_prob, g_idx[:, None], axis=-1)
    e_logit = (logits[:, n_groups:n_groups + n_experts] + b_expert).reshape(m, n_groups, per_group)
    e_in = jnp.take_along_axis(e_logit, g_idx[:, None, None], axis=1)[:, 0]
    i1 = jnp.argmax(e_in, axis=-1)
    rest = jnp.where(jnp.arange(per_group)[None, :] == i1[:, None], -jnp.inf, e_in)
    i2 = jnp.argmax(rest, axis=-1)
    top_v = jnp.stack([jnp.max(e_in, axis=-1), jnp.max(rest, axis=-1)], axis=-1)
    top_i = jnp.stack([i1, i2], axis=-1)
    top_w = jax.nn.softmax(top_v, axis=-1) * g_gate
    eid = (g_idx[:, None] * per_group + top_i).astype(jnp.int32)

    n_pairs = m * top_k
    n_tiles = (n_pairs + n_experts * (MOE_TM - 1)) // MOE_TM + 1
    n_rows = n_tiles * MOE_TM
    flat_e = eid.reshape(-1)
    order = jnp.argsort(flat_e, stable=True).astype(jnp.int32)
    rank = jnp.argsort(order).astype(jnp.int32)
    counts = jnp.dot(jnp.ones((8, n_pairs), BF16), jax.nn.one_hot(flat_e, n_experts, dtype=BF16),
                     preferred_element_type=F32)[0].astype(jnp.int32)
    padded = ((counts + MOE_TM - 1) // MOE_TM) * MOE_TM
    pad_end = jnp.cumsum(padded)
    pad_start = pad_end - padded
    start = jnp.cumsum(counts) - counts
    n_used = (pad_end[-1] // MOE_TM).astype(jnp.int32)
    tile_row = jnp.minimum(jnp.arange(n_tiles, dtype=jnp.int32), n_used - 1) * MOE_TM
    tile_e = jnp.minimum(jnp.sum(pad_end[None, :] <= tile_row[:, None], axis=1, dtype=jnp.int32),
                         n_experts - 1)
    row = jnp.arange(n_rows, dtype=jnp.int32)
    row_e = jnp.repeat(tile_e, MOE_TM)
    in_group = row - pad_start[row_e]
    live = (in_group < counts[row_e]) & (row < n_used * MOE_TM)
    pair = order[jnp.clip(start[row_e] + in_group, 0, n_pairs - 1)]
    row_token = jnp.where(live, pair // top_k, row % m)
    row_w = jnp.where(live, top_w.reshape(-1)[pair], 0.0)
    pos = (pad_start[flat_e] + rank - start[flat_e]).reshape(m, top_k)
    n_used = n_used.reshape(1)

    xs = jnp.take(h, row_token, axis=0, mode="clip")
    hid = pl.pallas_call(
        _moe_up_kernel,
        grid_spec=pltpu.PrefetchScalarGridSpec(
            num_scalar_prefetch=2, grid=(n_tiles,),
            in_specs=[pl.BlockSpec((MOE_TM, d), lambda i, te, nu: (i, 0)),
                      pl.BlockSpec((None, d, f), lambda i, te, nu: (layer, 0, te[i])),
                      pl.BlockSpec((None, d, f), lambda i, te, nu: (layer, 0, te[i])),
                      pl.BlockSpec((MOE_TM, 1), lambda i, te, nu: (i, 0))],
            out_specs=pl.BlockSpec((MOE_TM, f), lambda i, te, nu: (i, 0)),
            scratch_shapes=[pltpu.VMEM((d, f), BF16), pltpu.VMEM((d, f), BF16)]),
        out_shape=jax.ShapeDtypeStruct((n_rows, f), BF16),
        compiler_params=_cparams(1), name="moe_up",
    )(tile_e, n_used, xs, w_gate, w_up, row_w.reshape(n_rows, 1))
    rows = pl.pallas_call(
        _moe_down_kernel,
        grid_spec=pltpu.PrefetchScalarGridSpec(
            num_scalar_prefetch=2, grid=(n_tiles,),
            in_specs=[pl.BlockSpec((MOE_TM, f), lambda i, te, nu: (i, 0)),
                      pl.BlockSpec((None, f, d), lambda i, te, nu: (layer, te[i], 0))],
            out_specs=pl.BlockSpec((MOE_TM, d), lambda i, te, nu: (i, 0)),
            scratch_shapes=[pltpu.VMEM((f, d), BF16)]),
        out_shape=jax.ShapeDtypeStruct((n_rows, d), F32),
        compiler_params=_cparams(1), name="moe_down",
    )(tile_e, n_used, hid, w_down)

    return _moe_combine(x, rows, pos.reshape(-1), next_gain, split)


def _combine_kernel(pos_ref, x_ref, rows_hbm, *refs, top_k, with_norm, split_tile):
    if with_norm:
        g_ref, xo_ref, h_ref, buf, sem = refs
    else:
        lo_ref, hi_ref, buf, sem = refs
    i = pl.program_id(0)
    n_tiles = pl.num_programs(0)
    tm = x_ref.shape[0]
    slot = i % 2

    def gather_copy(tile, t, k, slot_):
        r = pos_ref[(tile * tm + t) * top_k + k]
        return pltpu.make_async_copy(rows_hbm.at[pl.ds(r, 1)],
                                     buf.at[slot_, pl.ds(k * tm + t, 1)], sem.at[slot_])

    def start_tile(tile, slot_):
        def body(t, carry):
            for k in range(top_k):
                gather_copy(tile, t, k, slot_).start()
            return carry
        lax.fori_loop(0, tm, body, 0, unroll=8)

    @pl.when(i == 0)
    def _():
        start_tile(0, 0)

    @pl.when(i + 1 < n_tiles)
    def _():
        start_tile(i + 1, 1 - slot)

    pltpu.make_async_copy(rows_hbm.at[pl.ds(0, top_k * tm)], buf.at[slot], sem.at[slot]).wait()
    y = x_ref[...]
    for k in range(top_k):
        y = y + buf[slot, k * tm:(k + 1) * tm]
    if with_norm:
        xo_ref[...] = y
        h_ref[...] = (y * lax.rsqrt(jnp.mean(y * y, axis=-1, keepdims=True) + RMS_EPS)
                      * g_ref[...]).astype(BF16)
    else:
        @pl.when(i < split_tile)
        def _():
            lo_ref[...] = y

        @pl.when(i >= split_tile)
        def _():
            hi_ref[...] = y


def _moe_combine(x, rows, pos, next_gain, split):
    m, d = x.shape
    top_k = pos.shape[0] // m
    tm = LANE
    assert m % tm == 0
    with_norm = next_gain is not None
    x_spec = pl.BlockSpec((tm, d), lambda i, p: (i, 0))
    in_specs = [x_spec, pl.BlockSpec(memory_space=pl.ANY)]
    args = [x, rows]
    if with_norm:
        split_tile = 0
        in_specs.append(pl.BlockSpec((1, d), lambda i, p: (0, 0)))
        args.append(next_gain.reshape(1, d))
        out_specs = [x_spec, x_spec]
        out_shape = [jax.ShapeDtypeStruct((m, d), F32), jax.ShapeDtypeStruct((m, d), BF16)]
    else:
        assert split % tm == 0
        split_tile = split // tm
        out_specs = [pl.BlockSpec((tm, d), lambda i, p: (jnp.minimum(i, split_tile - 1), 0)),
                     pl.BlockSpec((tm, d), lambda i, p: (jnp.maximum(i - split_tile, 0), 0))]
        out_shape = [jax.ShapeDtypeStruct((split, d), F32), jax.ShapeDtypeStruct((m - split, d), F32)]
    return pl.pallas_call(
        functools.partial(_combine_kernel, top_k=top_k, with_norm=with_norm, split_tile=split_tile),
        grid_spec=pltpu.PrefetchScalarGridSpec(
            num_scalar_prefetch=1, grid=(m // tm,), in_specs=in_specs, out_specs=out_specs,
            scratch_shapes=[pltpu.VMEM((2, top_k * tm, d), F32), pltpu.SemaphoreType.DMA((2,))]),
        out_shape=out_shape, compiler_params=_cparams(1),
        name="moe_combine_norm" if with_norm else "moe_combine_split",
    )(pos, *args)


def _param_pairs(a, n_batch, n_heads, n):
    return jnp.tile(a.reshape(n_heads, n).T, (1, n_batch))


def _even_layer(x, h, i, groups, state_ret, state_rwkv, state_shift, wts):
    (a_w_in, a_w_out, ret_norm_g, rwkv_mu, rwkv_w0, rwkv_w2, rwkv_a0, rwkv_a2, rwkv_g2,
     rwkv_kk_scale, rwkv_ka, rwkv_rk, rwkv_lnx_g, rwkv_lnx_b) = [w[i] for w in wts]
    ret_heads, ret_dk = state_ret.shape[2], state_ret.shape[3]
    rw_heads, rw_n = state_rwkv.shape[2], state_rwkv.shape[3]
    ret_w = ret_heads * ret_dk
    rw_w = rw_heads * rw_n
    lora_w, lora_a, lora_g = rwkv_w2.shape[0], rwkv_a2.shape[0], rwkv_g2.shape[0]
    assert lora_w == LANE and lora_a == LANE and lora_g <= LANE and ret_w == rw_w
    n_main = 4 * ret_w + 3 * rw_w
    n_shift = 3 * rw_w + lora_w + lora_a + lora_g
    (pb, pt), (sb, st) = groups
    m_p, m_s = pb * pt, sb * st

    p = _matmul([h], a_w_in, n_main, name="a_in")
    tail_cols = MM_TN
    assert n_main % tail_cols == 0 and n_main + tail_cols >= a_w_in.shape[1] and 3 * LANE <= tail_cols
    tail = _matmul([h], a_w_in, tail_cols, col_block0=n_main // tail_cols, name="a_in_tail")

    o_ret_p, ret_p = _retention_prompt(p, ret_norm_g, pb, pt, ret_heads, ret_dk)
    o_ret_s, ret_s = _retention_sample(p, ret_norm_g, state_ret[i], m_p, sb, st, ret_heads, ret_dk)

    pad_g = lambda a: jnp.pad(a, ((0, LANE - a.shape[0]), (0, 0)))
    mu = rwkv_mu
    mu_tail = jnp.pad(mu[3 * rw_w:], (0, tail_cols - (n_shift - 3 * rw_w)))
    prep_params = [mu[:rw_w].reshape(1, -1), mu[rw_w:2 * rw_w].reshape(1, -1),
                   mu[2 * rw_w:3 * rw_w].reshape(1, -1), mu_tail.reshape(1, -1),
                   rwkv_w0.reshape(1, -1), rwkv_w2, rwkv_a0.reshape(1, -1), rwkv_a2,
                   pad_g(rwkv_g2)]
    cb0 = (4 * ret_w) // rw_w
    shift0 = state_shift[i]
    sh_main = jnp.repeat(shift0[:, :3 * rw_w], st, axis=0)
    sh_tail = jnp.repeat(jnp.pad(shift0[:, 3 * rw_w:], ((0, 0), (0, tail_cols - (n_shift - 3 * rw_w)))),
                         st, axis=0)
    seq_p = _rwkv_prep(p, tail, 0, m_p, cb0, rw_w, prep_params, None, pt, st)
    seq_s = _rwkv_prep(p, tail, m_p, m_s, cb0, rw_w, prep_params, (sh_main, sh_tail), pt, st)

    outs = []
    for seqs, nb, nt, s0 in ((seq_p, pb, pt, None), (seq_s, sb, st, state_rwkv[i])):
        scan_params = [_param_pairs(a, nb, rw_heads, rw_n)
                       for a in (rwkv_kk_scale, rwkv_ka, rwkv_rk, rwkv_lnx_g, rwkv_lnx_b)]
        if s0 is None:
            s0_l = jnp.zeros((rw_n, rw_n, nb * rw_heads), F32)
        else:
            s0_l = s0.transpose(2, 3, 0, 1).reshape(rw_n, rw_n, nb * rw_heads)
        y, s_t = _rwkv_scan(seqs, scan_params, s0_l, nb, nt, rw_heads, rw_n)
        s_t = s_t.reshape(rw_n, rw_n, nb, rw_heads).transpose(2, 3, 0, 1)
        outs.append((y, s_t))
    (y_p, rwkv_p), (y_s, rwkv_s) = outs

    o_ret = jnp.concatenate([o_ret_p, o_ret_s], axis=0)
    y = jnp.concatenate([y_p, y_s], axis=0)
    x = _matmul([o_ret, y], a_w_out, a_w_out.shape[1], res=x, name="a_out")

    def last_ps(row0, nb, nt):
        def last(a, lo, hi):
            if nb * nt <= 1024:
                return a[row0:row0 + nb * nt, lo:hi].reshape(nb, nt, hi - lo)[:, nt - 1]
            return jnp.concatenate([a[row0 + (b + 1) * nt - 1:row0 + (b + 1) * nt, lo:hi]
                                    for b in range(nb)], axis=0)
        return jnp.concatenate([last(p, 4 * ret_w, n_main), last(tail, 0, n_shift - 3 * rw_w)],
                               axis=-1)

    return x, (ret_p, rwkv_p, last_ps(0, pb, pt)), (ret_s, rwkv_s, last_ps(m_p, sb, st))


def _odd_layer(x, h, i, groups, cache_k, cache_v, wts):
    c_w_in, c_w_out, c_q_norm_g, c_k_norm_g, c_sinks = [w[i] for w in wts]
    win, n_kv, hd = cache_k.shape[2], cache_k.shape[3], cache_k.shape[4]
    n_heads = c_sinks.shape[0]
    group = n_heads // n_kv
    mix_c = n_heads * hd
    kv_w = n_kv * hd
    (pb, pt), (sb, st) = groups
    m_p = pb * pt
    assert pt % win == 0 and mix_c % kv_w == 0
    p = _matmul([h], c_w_in, mix_c + 2 * kv_w, name="c_in")
    kcb, vcb = mix_c // kv_w, mix_c // kv_w + 1

    nb_p = m_p // win
    maps = (lambda n: (n, 0), lambda n: (n, kcb), lambda n: (n, vcb),
            lambda n: (jnp.maximum(n - 1, 0), kcb), lambda n: (jnp.maximum(n - 1, 0), vcb))
    o_p, kn_p = _swa(p, p, p, p, p, maps, nb_p, (win, mix_c), win, False, win, c_q_norm_g,
                     c_k_norm_g, c_sinks, n_kv, group, hd, True, False, pt // win)

    tq = 16
    ps = p[m_p:].reshape(sb, st, -1)
    q_s = jnp.pad(ps[:, :, :mix_c].reshape(sb, st, n_kv, group, hd),
                  ((0, 0), (0, tq - st), (0, 0), (0, 0), (0, 0)))
    q_s = q_s.transpose(0, 3, 1, 2, 4).reshape(sb * group * tq, kv_w)
    kv_pad = lambda a: jnp.pad(a, ((0, 0), (0, win - st), (0, 0))).reshape(sb * win, kv_w)
    k_s = kv_pad(ps[:, :, mix_c:mix_c + kv_w])
    v_s = kv_pad(ps[:, :, mix_c + kv_w:])
    kc0 = cache_k[i].reshape(sb * win, kv_w)
    vc0 = cache_v[i].reshape(sb * win, kv_w)
    same = lambda n: (n, 0)
    o_s, kn_s = _swa(q_s, k_s, v_s, kc0, vc0, (same,) * 5, sb, (group * tq, kv_w), tq, True, win,
                     c_q_norm_g, c_k_norm_g, c_sinks, n_kv, group, hd, False, True, 1)
    o_s = o_s.reshape(sb, group, tq, n_kv, hd)[:, :, :st].transpose(0, 2, 3, 1, 4).reshape(
        sb * st, mix_c)

    x = _matmul([jnp.concatenate([o_p, o_s], axis=0)], c_w_out, c_w_out.shape[1], res=x, name="c_out")

    new_k_p = kn_p.reshape(pb, pt, n_kv, hd)[:, -win:]
    new_v_p = p[:m_p, mix_c + kv_w:].reshape(pb, pt, n_kv, hd)[:, -win:]
    kn_s = kn_s.reshape(sb, win, n_kv, hd)[:, :st]
    new_k_s = jnp.concatenate([cache_k[i], kn_s], axis=1)[:, -win:]
    new_v_s = jnp.concatenate([cache_v[i], ps[:, :, mix_c + kv_w:].reshape(sb, st, n_kv, hd)],
                              axis=1)[:, -win:]
    return x, (new_k_p, new_v_p), (new_k_s, new_v_s)


def kernel(x_prompt, x_sample, state_ret, state_rwkv, state_rwkv_shift, cache_swa_k, cache_swa_v, norm_mix_g, norm_ffn_g, a_w_in, a_w_out, ret_norm_g, rwkv_mu, rwkv_w0, rwkv_w2, rwkv_a0, rwkv_a2, rwkv_g2, rwkv_kk_scale, rwkv_ka, rwkv_rk, rwkv_lnx_g, rwkv_lnx_b, c_w_in, c_w_out, c_q_norm_g, c_k_norm_g, c_sinks, moe_w_group, moe_b_group, moe_w_expert, moe_b_expert, moe_w_gate, moe_w_up, moe_w_down):
    pb, pt, d = x_prompt.shape
    sb, st, _ = x_sample.shape
    groups = ((pb, pt), (sb, st))
    depth = norm_mix_g.shape[0]
    a_wts = (a_w_in, a_w_out, ret_norm_g, rwkv_mu, rwkv_w0, rwkv_w2, rwkv_a0, rwkv_a2, rwkv_g2,
             rwkv_kk_scale, rwkv_ka, rwkv_rk, rwkv_lnx_g, rwkv_lnx_b)
    c_wts = (c_w_in, c_w_out, c_q_norm_g, c_k_norm_g, c_sinks)
    st_p = [[] for _ in range(5)]
    st_s = [[] for _ in range(5)]
    for l in range(depth):
        i = l // 2
        if l == 0:
            h, x = _rms_norm_first(x_prompt.reshape(pb * pt, d), x_sample.reshape(sb * st, d),
                                   norm_mix_g[l])
        else:
            h = h_next
        if l % 2 == 0:
            x, sp, ss = _even_layer(x, h, i, groups, state_ret, state_rwkv, state_rwkv_shift, a_wts)
            for j in range(3):
                st_p[j].append(sp[j])
                st_s[j].append(ss[j])
        else:
            x, sp, ss = _odd_layer(x, h, i, groups, cache_swa_k, cache_swa_v, c_wts)
            for j in range(2):
                st_p[3 + j].append(sp[j])
                st_s[3 + j].append(ss[j])
        n_router = moe_w_group.shape[2] + moe_w_expert.shape[2]
        w_router = jnp.pad(jnp.concatenate([moe_w_group[l], moe_w_expert[l]], axis=1),
                           ((0, 0), (0, LANE - n_router)))
        h, logits = _rms_norm(x, norm_ffn_g[l], w_router)
        moe_args = (x, h, logits, moe_b_group[l], moe_b_expert[l], moe_w_gate, moe_w_up, moe_w_down, l)
        if l < depth - 1:
            x, h_next = _moe(*moe_args, next_gain=norm_mix_g[l + 1])
        else:
            y_p, y_s = _moe(*moe_args, split=pb * pt)
    y_prompt = y_p.reshape(pb, pt, d)
    y_sample = y_s.reshape(sb, st, d)
    stack = lambda s: s[0][None] if len(s) == 1 else jnp.stack(s)
    return (y_prompt, y_sample) + tuple(stack(s) for s in st_p) + tuple(stack(s) for s in st_s)
```

```python
import functools

import jax
import jax.numpy as jnp
from jax import lax
from jax.experimental import pallas as pl
from jax.experimental.pallas import tpu as pltpu

F32 = jnp.float32
BF16 = jnp.bfloat16

RMS_EPS = 1e-6
RWKV_LN_EPS = 64e-5
RET_CHUNK = 128

LANE = 128
VMEM_LIMIT = 56 * 1024 * 1024

MM_TM = 640
MM_TN = 512
NORM_TM = 320
MOE_TM = 256
SCAN_TC = 32
RET_HEADS_PER_STEP = 4


def _cparams(n_axes):
    return pltpu.CompilerParams(dimension_semantics=("arbitrary",) * n_axes,
                                vmem_limit_bytes=VMEM_LIMIT)


def _norm_kernel(x_ref, g_ref, h_ref):
    x = x_ref[...]
    y = x * lax.rsqrt(jnp.mean(x * x, axis=-1, keepdims=True) + RMS_EPS) * g_ref[...]
    h_ref[...] = y.astype(BF16)


def _norm_router_kernel(x_ref, g_ref, wr_ref, h_ref, logit_ref):
    x = x_ref[...]
    y = x * lax.rsqrt(jnp.mean(x * x, axis=-1, keepdims=True) + RMS_EPS) * g_ref[...]
    h_ref[...] = y.astype(BF16)
    logit_ref[...] = jnp.dot(y, wr_ref[...], precision=lax.Precision.HIGHEST,
                             preferred_element_type=F32)


def _norm_first_kernel(xp_ref, xs_ref, g_ref, h_ref, x_ref, *, n_prompt_tiles):
    x = jnp.where(pl.program_id(0) < n_prompt_tiles, xp_ref[...], xs_ref[...])
    x_ref[...] = x
    y = x * lax.rsqrt(jnp.mean(x * x, axis=-1, keepdims=True) + RMS_EPS) * g_ref[...]
    h_ref[...] = y.astype(BF16)


def _rms_norm_first(xp, xs, g):
    (m_p, d), m_s = xp.shape, xs.shape[0]
    tm = LANE
    assert m_p % tm == 0 and m_s % tm == 0
    npt = m_p // tm
    spec = pl.BlockSpec((tm, d), lambda i: (i, 0))
    return pl.pallas_call(
        functools.partial(_norm_first_kernel, n_prompt_tiles=npt), grid=((m_p + m_s) // tm,),
        in_specs=[pl.BlockSpec((tm, d), lambda i: (jnp.minimum(i, npt - 1), 0)),
                  pl.BlockSpec((tm, d), lambda i: (jnp.maximum(i - npt, 0), 0)),
                  pl.BlockSpec((1, d), lambda i: (0, 0))],
        out_specs=[spec, spec],
        out_shape=[jax.ShapeDtypeStruct((m_p + m_s, d), BF16),
                   jax.ShapeDtypeStruct((m_p + m_s, d), F32)],
        compiler_params=_cparams(1), name="rms_norm_first")(xp, xs, g.reshape(1, d))


def _rms_norm(x, g, w_router=None):
    m, d = x.shape
    tm = NORM_TM if m % NORM_TM == 0 else m
    grid = (m // tm,)
    x_spec = pl.BlockSpec((tm, d), lambda i: (i, 0))
    g_spec = pl.BlockSpec((1, d), lambda i: (0, 0))
    if w_router is None:
        return pl.pallas_call(
            _norm_kernel, grid=grid, in_specs=[x_spec, g_spec], out_specs=x_spec,
            out_shape=jax.ShapeDtypeStruct((m, d), BF16), compiler_params=_cparams(1),
            name="rms_norm")(x, g.reshape(1, d))
    nr = w_router.shape[1]
    return pl.pallas_call(
        _norm_router_kernel, grid=grid,
        in_specs=[x_spec, g_spec, pl.BlockSpec((d, nr), lambda i: (0, 0))],
        out_specs=[x_spec, pl.BlockSpec((tm, nr), lambda i: (i, 0))],
        out_shape=[jax.ShapeDtypeStruct((m, d), BF16), jax.ShapeDtypeStruct((m, nr), F32)],
        compiler_params=_cparams(1), name="rms_norm_router")(x, g.reshape(1, d), w_router)


def _mm_kernel(*refs, n_parts, has_res, cols_valid, w_t):
    x_refs = refs[:n_parts]
    w_refs = refs[n_parts:2 * n_parts]
    pos = 2 * n_parts
    res_ref = refs[pos] if has_res else None
    pos += int(has_res)
    o_ref = refs[pos]
    wb_refs = refs[pos + 1:]
    n_axis = 0 if w_t else 1

    @pl.when(pl.program_id(1) == 0)
    def _():
        for w_ref, wb_ref in zip(w_refs, wb_refs):
            w = w_ref[...]
            if cols_valid is not None:
                col = (lax.broadcasted_iota(jnp.int32, w.shape, n_axis)
                       + pl.program_id(0) * w.shape[n_axis])
                w = jnp.where(col < cols_valid, w, 0.0)
            wb_ref[...] = w.astype(BF16)

    acc = None
    for x_ref, wb_ref in zip(x_refs, wb_refs):
        d = lax.dot_general(x_ref[...], wb_ref[...], (((1,), (1 if w_t else 0,)), ((), ())),
                            preferred_element_type=F32)
        acc = d if acc is None else acc + d
    if has_res:
        acc = acc + res_ref[...]
    o_ref[...] = acc


def _matmul(xs, w, n_out, col_block0=0, tn=MM_TN, res=None, name="proj", w_t=False):
    k_total, n_total = (w.shape[1], w.shape[0]) if w_t else w.shape
    over = (col_block0 * tn + n_out) > n_total
    cols_valid = n_total - col_block0 * tn if over else None
    m = xs[0].shape[0]
    tm = MM_TM if m % MM_TM == 0 else m
    assert n_out % tn == 0
    grid = (n_out // tn, m // tm)
    in_specs, w_specs, scratch = [], [], []
    row = 0
    for x in xs:
        kp = x.shape[1]
        in_specs.append(pl.BlockSpec((tm, kp), lambda j, i: (i, 0)))
        assert row % kp == 0
        rb = row // kp
        if w_t:
            w_specs.append(pl.BlockSpec((tn, kp), lambda j, i, rb=rb: (j + col_block0, rb)))
            scratch.append(pltpu.VMEM((tn, kp), BF16))
        else:
            w_specs.append(pl.BlockSpec((kp, tn), lambda j, i, rb=rb: (rb, j + col_block0)))
            scratch.append(pltpu.VMEM((kp, tn), BF16))
        row += kp
    assert row == k_total
    args = list(xs) + [w] * len(xs)
    in_specs = in_specs + w_specs
    if res is not None:
        in_specs.append(pl.BlockSpec((tm, tn), lambda j, i: (i, j)))
        args.append(res)
    return pl.pallas_call(
        functools.partial(_mm_kernel, n_parts=len(xs), has_res=res is not None,
                          cols_valid=cols_valid, w_t=w_t),
        grid=grid, in_specs=in_specs,
        out_specs=pl.BlockSpec((tm, tn), lambda j, i: (i, j)),
        out_shape=jax.ShapeDtypeStruct((m, n_out), F32),
        scratch_shapes=scratch, compiler_params=_cparams(2), name=name)(*args)


def _ret_finish(o, g, gain):
    o = o * lax.rsqrt(jnp.mean(o * o, axis=-1, keepdims=True) + RMS_EPS) * gain
    return ((g / (1.0 + jnp.exp(-g))) * o).astype(BF16)


def _ret_prompt_kernel(q_ref, k_ref, v_ref, g_ref, gain_ref, lg_ref, o_ref, s_ref, *, dk):
    c = pl.program_id(2)
    chunk = q_ref.shape[0]

    @pl.when(c == 0)
    def _():
        s_ref[...] = jnp.zeros_like(s_ref)

    row = lax.broadcasted_iota(jnp.int32, (chunk, chunk), 0)
    col = lax.broadcasted_iota(jnp.int32, (chunk, chunk), 1)
    diff = (row - col).astype(F32)
    causal = diff >= 0
    dist = jnp.where(causal, diff, 0.0)
    pos = lax.broadcasted_iota(jnp.int32, (chunk, 1), 0).astype(F32)

    for hh in range(s_ref.shape[0]):
        sl = slice(hh * dk, (hh + 1) * dk)
        lg = lg_ref[hh][:, :1]
        decay_mask = jnp.where(causal, jnp.exp(dist * lg), 0.0)
        q_decay = jnp.exp((pos + 1.0) * lg)
        k_decay = jnp.exp((chunk - 1.0 - pos) * lg)
        chunk_decay = jnp.exp(chunk * lg)
        q = q_ref[:, sl]
        k = k_ref[:, sl] * (dk ** -0.5)
        vb = v_ref[:, sl].astype(BF16)
        s = s_ref[hh]
        scores = lax.dot_general(q.astype(BF16), k.astype(BF16), (((1,), (1,)), ((), ())),
                                 preferred_element_type=F32) * decay_mask
        inner = jnp.dot(scores.astype(BF16), vb, preferred_element_type=F32)
        cross = jnp.dot((q * q_decay).astype(BF16), s.astype(BF16), preferred_element_type=F32)
        s_ref[hh] = chunk_decay * s + lax.dot_general(
            (k * k_decay).astype(BF16), vb, (((0,), (0,)), ((), ())), preferred_element_type=F32)
        o_ref[:, sl] = _ret_finish(inner + cross, g_ref[:, sl], gain_ref[hh])


def _ret_sample_kernel(q_ref, k_ref, v_ref, g_ref, gain_ref, lg_ref, s0_ref, o_ref, s_ref, acc_ref,
                       *, dk, t_len):
    b = pl.program_id(1)
    rows = q_ref.shape[0]
    lg = lg_ref[:, :1]
    r_b = lax.broadcasted_iota(jnp.int32, (rows, 1), 0) // t_len
    r_t = (lax.broadcasted_iota(jnp.int32, (rows, 1), 0) % t_len).astype(F32)
    q = q_ref[...]
    k = k_ref[...] * (dk ** -0.5)
    vb = v_ref[...].astype(BF16)

    @pl.when(b == 0)
    def _():
        row = lax.broadcasted_iota(jnp.int32, (rows, rows), 0)
        col = lax.broadcasted_iota(jnp.int32, (rows, rows), 1)
        diff = (row % t_len - col % t_len).astype(F32)
        ok = (row // t_len == col // t_len) & (diff >= 0)
        mask = jnp.where(ok, jnp.exp(jnp.where(ok, diff, 0.0) * lg), 0.0)
        scores = lax.dot_general(q.astype(BF16), k.astype(BF16), (((1,), (1,)), ((), ())),
                                 preferred_element_type=F32) * mask
        acc_ref[...] = jnp.dot(scores.astype(BF16), vb, preferred_element_type=F32)

    sel = r_b == b
    s0 = s0_ref[...]
    cross = jnp.dot((q * jnp.exp((r_t + 1.0) * lg)).astype(BF16), s0.astype(BF16),
                    preferred_element_type=F32)
    acc_ref[...] += jnp.where(sel, cross, 0.0)
    k_decay = jnp.where(sel, jnp.exp((t_len - 1.0 - r_t) * lg), 0.0)
    s_ref[...] = jnp.exp(t_len * lg) * s0 + lax.dot_general(
        (k * k_decay).astype(BF16), vb, (((0,), (0,)), ((), ())), preferred_element_type=F32)

    @pl.when(b == pl.num_programs(1) - 1)
    def _():
        o_ref[...] = _ret_finish(acc_ref[...], g_ref[...], gain_ref[...])


def _ret_log_decay(n_heads):
    lg = jnp.log(1.0 - 2.0 ** (-5.0 - jnp.arange(n_heads, dtype=F32)))
    return jnp.broadcast_to(lg[:, None, None], (n_heads, 1, LANE))


def _retention_prompt(p, gain, n_batch, seq, n_heads, dk):
    nc = seq // RET_CHUNK
    hps = RET_HEADS_PER_STEP
    assert n_heads % hps == 0
    ng = n_heads // hps
    col = lambda grp: (lambda b, h, c: (b * nc + c, grp * ng + h))
    blk = lambda grp: pl.BlockSpec((RET_CHUNK, hps * dk), col(grp))
    return pl.pallas_call(
        functools.partial(_ret_prompt_kernel, dk=dk),
        grid=(n_batch, ng, nc),
        in_specs=[blk(0), blk(1), blk(2), blk(3),
                  pl.BlockSpec((hps, 1, dk), lambda b, h, c: (h, 0, 0)),
                  pl.BlockSpec((hps, 1, LANE), lambda b, h, c: (h, 0, 0))],
        out_specs=[pl.BlockSpec((RET_CHUNK, hps * dk), lambda b, h, c: (b * nc + c, h)),
                   pl.BlockSpec((None, hps, dk, dk), lambda b, h, c: (b, h, 0, 0))],
        out_shape=[jax.ShapeDtypeStruct((n_batch * seq, n_heads * dk), BF16),
                   jax.ShapeDtypeStruct((n_batch, n_heads, dk, dk), F32)],
        compiler_params=_cparams(3), name="retention_prompt",
    )(p, p, p, p, gain.reshape(n_heads, 1, dk), _ret_log_decay(n_heads))


def _retention_sample(p, gain, s0, row0, n_batch, t_len, n_heads, dk):
    rows = n_batch * t_len
    rb = row0 // rows
    blk = lambda grp: pl.BlockSpec((rows, dk), lambda h, b, grp=grp: (rb, grp * n_heads + h))
    return pl.pallas_call(
        functools.partial(_ret_sample_kernel, dk=dk, t_len=t_len),
        grid=(n_heads, n_batch),
        in_specs=[blk(0), blk(1), blk(2), blk(3),
                  pl.BlockSpec((None, 1, dk), lambda h, b: (h, 0, 0)),
                  pl.BlockSpec((None, 1, LANE), lambda h, b: (h, 0, 0)),
                  pl.BlockSpec((None, None, dk, dk), lambda h, b: (b, h, 0, 0))],
        out_specs=[pl.BlockSpec((rows, dk), lambda h, b: (0, h)),
                   pl.BlockSpec((None, None, dk, dk), lambda h, b: (b, h, 0, 0))],
        out_shape=[jax.ShapeDtypeStruct((rows, n_heads * dk), BF16),
                   jax.ShapeDtypeStruct((n_batch, n_heads, dk, dk), F32)],
        scratch_shapes=[pltpu.VMEM((rows, dk), F32)],
        compiler_params=_cparams(2), name="retention_sample",
    )(p, p, p, p, gain.reshape(n_heads, 1, dk), _ret_log_decay(n_heads), s0)


def _sigmoid(x):
    return 1.0 / (1.0 + jnp.exp(-x))


def _rwkv_prep_kernel(*refs, sample, seq, t_len):
    if sample:
        (r_c, k_c, v_c, t_c, r_s, k_s, v_s, t_s,
         mu_r, mu_k, mu_v, mu_t, w0, w2, a0, a2, g2,
         r_o, w_o, k_o, v_o, a_o, g_o) = refs
    else:
        (r_c, k_c, v_c, t_c, r_p, k_p, v_p, t_p,
         mu_r, mu_k, mu_v, mu_t, w0, w2, a0, a2, g2,
         r_o, w_o, k_o, v_o, a_o, g_o) = refs
    tm = r_c.shape[0]
    row = lax.broadcasted_iota(jnp.int32, (tm, 1), 0)
    seq_start = (pl.program_id(0) * tm) % seq == 0

    def mix(cur_ref, other_ref, mu_ref):
        cur = cur_ref[...]
        rolled = pltpu.roll(cur, 1, 0)
        if sample:
            prev = jnp.where(row % t_len == 0, other_ref[...], rolled)
        else:
            last = other_ref[7:8, :]
            first = jnp.where(seq_start, jnp.zeros_like(last), last)
            prev = jnp.where(row == 0, first, rolled)
        return cur + (prev - cur) * mu_ref[...]

    def store(o_ref, val):
        if len(o_ref.shape) == 2:
            o_ref[...] = val
        else:
            for c in range(o_ref.shape[1]):
                o_ref[:, c] = val[:, c * LANE:(c + 1) * LANE].reshape(o_ref.shape[0], 8, LANE)

    other = (r_s, k_s, v_s, t_s) if sample else (r_p, k_p, v_p, t_p)
    store(r_o, mix(r_c, other[0], mu_r))
    store(k_o, mix(k_c, other[1], mu_k))
    store(v_o, mix(v_c, other[2], mu_v))
    tail = mix(t_c, other[3], mu_t)
    lw, la, lgt = tail[:, :LANE], tail[:, LANE:2 * LANE], tail[:, 2 * LANE:3 * LANE]

    z = -(w0[...] + jnp.dot(jnp.tanh(lw).astype(BF16), w2[...].astype(BF16),
                            preferred_element_type=F32))
    softplus = jnp.maximum(z, 0.0) + jnp.log1p(jnp.exp(-jnp.abs(z)))
    store(w_o, jnp.exp(-jnp.exp(-softplus - 0.5)))
    store(a_o, _sigmoid(a0[...] + jnp.dot(la.astype(BF16), a2[...].astype(BF16),
                                          preferred_element_type=F32)))
    store(g_o, jnp.dot(_sigmoid(lgt).astype(BF16), g2[...].astype(BF16),
                       preferred_element_type=F32))


def _rwkv_prep(p, tail, row0, n_rows, col_block0, width, params, shift_rows, seq, t_len):
    sample = shift_rows is not None
    tm = n_rows if sample else 256
    rb0 = row0 // tm
    grid = (n_rows // tm,)
    cur = lambda grp: pl.BlockSpec((tm, width), lambda i, grp=grp: (rb0 + i, col_block0 + grp))
    tail_w = tail.shape[1]
    in_specs = [cur(0), cur(1), cur(2), pl.BlockSpec((tm, tail_w), lambda i: (rb0 + i, 0))]
    args = [p, p, p, tail]
    if sample:
        sh_main, sh_tail = shift_rows
        in_specs += [pl.BlockSpec((tm, width), lambda i, grp=grp: (0, grp)) for grp in range(3)]
        in_specs += [pl.BlockSpec((tm, tail_w), lambda i: (0, 0))]
        args += [sh_main, sh_main, sh_main, sh_tail]
    else:
        per = tm // 8
        prev_rb = lambda i: jnp.maximum((rb0 + i) * per - 1, 0)
        in_specs += [pl.BlockSpec((8, width), lambda i, grp=grp: (prev_rb(i), col_block0 + grp))
                     for grp in range(3)]
        in_specs += [pl.BlockSpec((8, tail_w), lambda i: (prev_rb(i), 0))]
        args += [p, p, p, tail]
    full = lambda a: pl.BlockSpec(a.shape, lambda i: (0,) * a.ndim)
    in_specs += [full(a) for a in params]
    args += list(params)
    if sample:
        out_spec = pl.BlockSpec((tm, width), lambda i: (i, 0))
        out_shape = jax.ShapeDtypeStruct((n_rows, width), F32)
    else:
        out_spec = pl.BlockSpec((tm // 8, width // LANE, 8, LANE), lambda i: (i, 0, 0, 0))
        out_shape = jax.ShapeDtypeStruct((n_rows // 8, width // LANE, 8, LANE), F32)
    return pl.pallas_call(
        functools.partial(_rwkv_prep_kernel, sample=sample, seq=seq, t_len=t_len),
        grid=grid, in_specs=in_specs, out_specs=[out_spec] * 6, out_shape=[out_shape] * 6,
        compiler_params=_cparams(1), name="rwkv_prep_sample" if sample else "rwkv_prep_prompt",
    )(*args)


def _rwkv_step(s_ref, ys_ref, r, w, kr, v, a, g, par_refs, between=None):
    kks_ref, ka_ref, rk_ref, lng_ref, lnb_ref = par_refs
    kks = kr * kks_ref[...]
    kk = kks / jnp.maximum(jnp.sqrt(jnp.sum(kks * kks, axis=0, keepdims=True)), 1e-12)
    kf = kr * (1.0 + (a - 1.0) * ka_ref[...])
    kka = kk * a
    for vi in range(s_ref.shape[0]):
        s = s_ref[vi]
        sa = jnp.sum(s * kk, axis=0, keepdims=True)
        s = s * w - sa * kka + v[vi:vi + 1, :] * kf
        s_ref[vi] = s
        ys_ref[vi:vi + 1, :] = jnp.sum(s * r, axis=0, keepdims=True)
        if between is not None:
            between(vi)
    y = ys_ref[...]
    mu = jnp.mean(y, axis=0, keepdims=True)
    var = jnp.mean(jnp.square(y - mu), axis=0, keepdims=True)
    y = (y - mu) * lax.rsqrt(var + RWKV_LN_EPS) * lng_ref[...] + lnb_ref[...]
    bonus = jnp.sum(r * kf * rk_ref[...], axis=0, keepdims=True) * v
    return (y + bonus) * g


def _rwkv_scan_kernel(r_ref, w_ref, k_ref, v_ref, a_ref, g_ref,
                      kks_ref, ka_ref, rk_ref, lng_ref, lnb_ref, s0_ref,
                      y_ref, s_ref, ys_ref):
    par_refs = (kks_ref, ka_ref, rk_ref, lng_ref, lnb_ref)

    @pl.when(pl.program_id(1) == 0)
    def _():
        s_ref[...] = s0_ref[...]

    def step(t, carry):
        y = _rwkv_step(s_ref, ys_ref, r_ref[t], w_ref[t], k_ref[t], v_ref[t], a_ref[t], g_ref[t],
                       par_refs)
        y_ref[t] = y.astype(y_ref.dtype)
        return carry

    lax.fori_loop(0, r_ref.shape[0], step, 0)


def _rwkv_scan_tiled_kernel(r_ref, w_ref, k_ref, v_ref, a_ref, g_ref,
                            kks_ref, ka_ref, rk_ref, lng_ref, lnb_ref, s0_ref,
                            y_ref, s_ref, ys_ref, nxt_ref):
    seq_refs = (r_ref, w_ref, k_ref, v_ref, a_ref, g_ref)
    par_refs = (kks_ref, ka_ref, rk_ref, lng_ref, lnb_ref)
    n_grp = r_ref.shape[1]
    n = s_ref.shape[0]
    every = n // (len(seq_refs) + 1)

    @pl.when(pl.program_id(0) == 0)
    def _():
        s_ref[...] = s0_ref[...]

    def fetch(i, tg, t8, slot):
        zt = seq_refs[i][:, tg, :, t8, :].reshape(-1, LANE).T
        nxt_ref[slot, i] = jnp.concatenate([zt[:n], zt[n:]], axis=1)

    for i in range(len(seq_refs)):
        fetch(i, 0, 0, 0)

    def group(tg, carry):
        for t8 in range(8):
            slot = t8 % 2
            nxt_tg = tg if t8 < 7 else jnp.minimum(tg + 1, n_grp - 1)

            def between(vi, slot=slot, nxt_tg=nxt_tg, t8=t8):
                if vi % every == every - 1 and vi // every < len(seq_refs):
                    fetch(vi // every, nxt_tg, (t8 + 1) % 8, 1 - slot)

            y = _rwkv_step(s_ref, ys_ref, *[nxt_ref[slot, i] for i in range(len(seq_refs))],
                           par_refs, between)
            y_ref[tg * 8 + t8] = y.astype(y_ref.dtype)
        return carry

    lax.fori_loop(0, n_grp, group, 0)


def _rwkv_scan(seqs, kparams, s0, n_batch, t_len, n_heads, n):
    pairs = n_batch * n_heads
    tc = SCAN_TC if t_len % SCAN_TC == 0 else t_len
    grid = (pairs // LANE, t_len // tc)
    seq_spec = pl.BlockSpec((tc, n, LANE), lambda g, t: (t, 0, g))
    par_spec = pl.BlockSpec((n, LANE), lambda g, t: (0, g))
    st_spec = pl.BlockSpec((n, n, LANE), lambda g, t: (0, 0, g))
    pairs_major = lambda a: a.reshape(n_batch, t_len, n_heads, n).transpose(1, 3, 0, 2).reshape(
        t_len, n, pairs)
    params = [jnp.tile(a.reshape(n_heads, n).T, (1, n_batch)) for a in kparams]
    y, s_t = pl.pallas_call(
        _rwkv_scan_kernel, grid=grid,
        in_specs=[seq_spec] * 6 + [par_spec] * 5 + [st_spec],
        out_specs=[seq_spec, st_spec],
        out_shape=[jax.ShapeDtypeStruct((t_len, n, pairs), BF16),
                   jax.ShapeDtypeStruct((n, n, pairs), F32)],
        scratch_shapes=[pltpu.VMEM((n, LANE), F32)],
        compiler_params=_cparams(2), name="rwkv_scan",
    )(*[pairs_major(a) for a in seqs], *params, s0.transpose(2, 3, 0, 1).reshape(n, n, pairs))
    y = y.reshape(t_len, n, n_batch, n_heads).transpose(2, 0, 3, 1)
    s_t = s_t.reshape(n, n, n_batch, n_heads).transpose(2, 3, 0, 1)
    return y.reshape(n_batch * t_len, n_heads * n), s_t


def _rwkv_scan_tiled(seqs, kparams, n_batch, t_len, n_heads, n):
    chunks = n_heads * n // LANE
    assert n_batch * n_heads == LANE and 2 * n == LANE and t_len % SCAN_TC == 0 and SCAN_TC % 16 == 0
    tc = SCAN_TC
    seq_spec = pl.BlockSpec((n_batch, tc // 8, chunks, 8, LANE), lambda t: (0, t, 0, 0, 0))
    par_spec = pl.BlockSpec((n, LANE), lambda t: (0, 0))
    st_spec = pl.BlockSpec((n, n, LANE), lambda t: (0, 0, 0))
    params = [jnp.broadcast_to(a.reshape(chunks, 2, n).transpose(2, 1, 0)[:, :, None, :],
                               (n, 2, n_batch, chunks)).reshape(n, LANE) for a in kparams]
    y, s_t = pl.pallas_call(
        _rwkv_scan_tiled_kernel, grid=(t_len // tc,),
        in_specs=[seq_spec] * 6 + [par_spec] * 5 + [st_spec],
        out_specs=[pl.BlockSpec((tc, n, LANE), lambda t: (t, 0, 0)), st_spec],
        out_shape=[jax.ShapeDtypeStruct((t_len, n, LANE), BF16),
                   jax.ShapeDtypeStruct((n, n, LANE), F32)],
        scratch_shapes=[pltpu.VMEM((n, LANE), F32), pltpu.VMEM((2, 6, n, LANE), F32)],
        compiler_params=_cparams(1), name="rwkv_scan_tiled",
    )(*[a.reshape(n_batch, t_len // 8, chunks, 8, LANE) for a in seqs], *params,
      jnp.zeros((n, n, LANE), F32))
    y = y.reshape(t_len, n, 2, n_batch, chunks).transpose(3, 0, 4, 2, 1)
    s_t = s_t.reshape(n, n, 2, n_batch, chunks).transpose(3, 4, 2, 0, 1)
    return y.reshape(n_batch * t_len, n_heads * n), s_t.reshape(n_batch, n_heads, n, n)


def _swa_kernel(sink_ref, q_ref, kc_ref, vc_ref, kp_ref, vp_ref, qg_ref, kg_ref,
                o_ref, kn_ref, *, n_kv, group, hd, rows_per_head, stacked, norm_prev,
                first_has_prev, blocks_per_seq):
    rph = rows_per_head
    win = kc_ref.shape[0]
    n_heads = n_kv * group
    has_prev = jnp.logical_or(first_has_prev, pl.program_id(0) % blocks_per_seq > 0)

    def rms(x, g):
        return x * lax.rsqrt(jnp.mean(x * x, axis=-1, keepdims=True) + RMS_EPS) * g

    qpos = lax.broadcasted_iota(jnp.int32, (rph, 2 * win), 0)
    kpos = lax.broadcasted_iota(jnp.int32, (rph, 2 * win), 1) - win
    diff = qpos - kpos
    valid = (diff >= 0) & (diff < win) & (kpos >= jnp.where(has_prev, -win, 0))
    neg_dist = jnp.where(valid, -(diff.astype(F32)), -jnp.inf)
    qg = qg_ref[...]
    kg = kg_ref[...]

    for j in range(n_kv):
        sl = slice(j * hd, (j + 1) * hd)
        kc = rms(kc_ref[:, sl], kg)
        kn_ref[:, sl] = kc
        kp = kp_ref[:, sl]
        if norm_prev:
            kp = rms(kp, kg)
        kb = jnp.concatenate([kp, kc], axis=0).astype(BF16)
        vb = jnp.concatenate([vp_ref[:, sl], vc_ref[:, sl]], axis=0).astype(BF16)
        if stacked:
            q = q_ref[:, sl]
        else:
            q = jnp.concatenate([q_ref[:, (j * group + gi) * hd:(j * group + gi + 1) * hd]
                                 for gi in range(group)], axis=0)
        s = lax.dot_general(rms(q, qg).astype(BF16), kb, (((1,), (1,)), ((), ())),
                            preferred_element_type=F32)
        es, inv = [], []
        for gi in range(group):
            h = j * group + gi
            slope = 2.0 ** (-8.0 * (h + 1) / n_heads)
            sh = s[gi * rph:(gi + 1) * rph] * (hd ** -0.5) + slope * neg_dist
            sink = sink_ref[h]
            m = jnp.maximum(jnp.max(sh, axis=-1, keepdims=True), sink)
            e = jnp.exp(sh - m)
            inv.append(1.0 / (jnp.sum(e, axis=-1, keepdims=True) + jnp.exp(sink - m)))
            es.append(e.astype(BF16))
        o = jnp.dot(jnp.concatenate(es, axis=0), vb, preferred_element_type=F32)
        o = (o * jnp.concatenate(inv, axis=0)).astype(BF16)
        if stacked:
            o_ref[:, sl] = o
        else:
            for gi in range(group):
                h = j * group + gi
                o_ref[:, h * hd:(h + 1) * hd] = o[gi * rph:(gi + 1) * rph]


def _swa(q_arr, kc_arr, vc_arr, kp_arr, vp_arr, maps, n_blocks, q_block, rows_per_head, stacked,
         win, qg, kg, sinks, n_kv, group, hd, norm_prev, first_has_prev, blocks_per_seq):
    q_map, kc_map, vc_map, kp_map, vp_map = maps
    return pl.pallas_call(
        functools.partial(_swa_kernel, n_kv=n_kv, group=group, hd=hd, rows_per_head=rows_per_head,
                          stacked=stacked, norm_prev=norm_prev, first_has_prev=first_has_prev,
                          blocks_per_seq=blocks_per_seq),
        grid=(n_blocks,),
        in_specs=[pl.BlockSpec(memory_space=pltpu.SMEM),
                  pl.BlockSpec(q_block, q_map),
                  pl.BlockSpec((win, n_kv * hd), kc_map),
                  pl.BlockSpec((win, n_kv * hd), vc_map),
                  pl.BlockSpec((win, n_kv * hd), kp_map),
                  pl.BlockSpec((win, n_kv * hd), vp_map),
                  pl.BlockSpec((1, hd), lambda i: (0, 0)),
                  pl.BlockSpec((1, hd), lambda i: (0, 0))],
        out_specs=[pl.BlockSpec(q_block, lambda i: (i, 0)),
                   pl.BlockSpec((win, n_kv * hd), lambda i: (i, 0))],
        out_shape=[jax.ShapeDtypeStruct((n_blocks * q_block[0], q_block[1]), BF16),
                   jax.ShapeDtypeStruct((n_blocks * win, n_kv * hd), F32)],
        compiler_params=_cparams(1), name="swa_stacked%d" % int(stacked),
    )(sinks, q_arr, kc_arr, vc_arr, kp_arr, vp_arr, qg.reshape(1, hd), kg.reshape(1, hd))


def _moe_up_kernel(te_ref, nu_ref, xs_ref, wg_ref, wu_ref, rw_ref, hid_ref, wgb_ref, wub_ref):
    i = pl.program_id(0)
    changed = jnp.logical_or(i == 0, te_ref[i] != te_ref[jnp.maximum(i - 1, 0)])

    @pl.when(changed)
    def _():
        wgb_ref[...] = wg_ref[...].astype(BF16)
        wub_ref[...] = wu_ref[...].astype(BF16)

    @pl.when(i < nu_ref[0])
    def _():
        x = xs_ref[...]
        g = jnp.dot(x, wgb_ref[...], preferred_element_type=F32)
        u = jnp.dot(x, wub_ref[...], preferred_element_type=F32)
        hid_ref[...] = ((g / (1.0 + jnp.exp(-g))) * u * rw_ref[...]).astype(BF16)

    @pl.when(i >= nu_ref[0])
    def _():
        hid_ref[...] = jnp.zeros_like(hid_ref)


def _moe_down_kernel(te_ref, nu_ref, hid_ref, wd_ref, out_ref, wdb_ref):
    i = pl.program_id(0)
    changed = jnp.logical_or(i == 0, te_ref[i] != te_ref[jnp.maximum(i - 1, 0)])

    @pl.when(changed)
    def _():
        wdb_ref[...] = wd_ref[...].astype(BF16)

    @pl.when(i < nu_ref[0])
    def _():
        out_ref[...] = jnp.dot(hid_ref[...], wdb_ref[...], preferred_element_type=F32)

    @pl.when(i >= nu_ref[0])
    def _():
        out_ref[...] = jnp.zeros_like(out_ref)


def _moe(x, h, logits, b_group, b_expert, w_gate, w_up, w_down, layer, next_gain=None, split=None):
    m, d = h.shape
    n_groups = b_group.shape[0]
    n_experts = b_expert.shape[0]
    per_group = n_experts // n_groups
    f = w_gate.shape[2] // n_experts
    top_k = 2

    g_logit = logits[:, :n_groups] + b_group
    g_prob = jax.nn.softmax(g_logit, axis=-1)
    g_idx = jnp.argmax(g_logit, axis=-1)
    g_gate = jnp.take_along_axis(g_prob, g_idx[:, None], axis=-1)
    e_logit = (logits[:, n_groups:n_groups + n_experts] + b_expert).reshape(m, n_groups, per_group)
    e_in = jnp.take_along_axis(e_logit, g_idx[:, None, None], axis=1)[:, 0]
    i1 = jnp.argmax(e_in, axis=-1)
    rest = jnp.where(jnp.arange(per_group)[None, :] == i1[:, None], -jnp.inf, e_in)
    i2 = jnp.argmax(rest, axis=-1)
    top_v = jnp.stack([jnp.max(e_in, axis=-1), jnp.max(rest, axis=-1)], axis=-1)
    top_i = jnp.stack([i1, i2], axis=-1)
    top_w = jax.nn.softmax(top_v, axis=-1) * g_gate
    eid = (g_idx[:, None] * per_group + top_i).astype(jnp.int32)

    n_pairs = m * top_k
    n_tiles = (n_pairs + n_experts * (MOE_TM - 1)) // MOE_TM + 1
    n_rows = n_tiles * MOE_TM
    flat_e = eid.reshape(-1)
    order = jnp.argsort(flat_e, stable=True).astype(jnp.int32)
    counts = jnp.dot(jnp.ones((8, n_pairs), BF16), jax.nn.one_hot(flat_e, n_experts, dtype=BF16),
                     preferred_element_type=F32)[0].astype(jnp.int32)
    padded = ((counts + MOE_TM - 1) // MOE_TM) * MOE_TM
    pad_end = jnp.cumsum(padded)
    pad_start = pad_end - padded
    start = jnp.cumsum(counts) - counts
    n_used = (pad_end[-1] // MOE_TM).astype(jnp.int32)
    tile_row = jnp.minimum(jnp.arange(n_tiles, dtype=jnp.int32), n_used - 1) * MOE_TM
    tile_e = jnp.minimum(jnp.sum(pad_end[None, :] <= tile_row[:, None], axis=1, dtype=jnp.int32),
                         n_experts - 1)
    row = jnp.arange(n_rows, dtype=jnp.int32)
    per_row = lambda table: jnp.repeat(table[tile_e], MOE_TM)
    in_group = row - per_row(pad_start)
    live = (in_group < per_row(counts)) & (row < n_used * MOE_TM)
    pair = order[jnp.clip(per_row(start) + in_group, 0, n_pairs - 1)]
    row_token = jnp.where(live, pair // top_k, row % m)
    row_w = jnp.where(live, top_w.reshape(-1)[pair], 0.0)
    pos = jnp.argsort(jnp.where(live, pair, n_pairs + row))[:n_pairs].astype(jnp.int32)
    pos = pos.reshape(m, top_k)
    n_used = n_used.reshape(1)

    xs = jnp.take(h, row_token, axis=0, mode="clip")
    hid = pl.pallas_call(
        _moe_up_kernel,
        grid_spec=pltpu.PrefetchScalarGridSpec(
            num_scalar_prefetch=2, grid=(n_tiles,),
            in_specs=[pl.BlockSpec((MOE_TM, d), lambda i, te, nu: (i, 0)),
                      pl.BlockSpec((None, d, f), lambda i, te, nu: (layer, 0, te[i])),
                      pl.BlockSpec((None, d, f), lambda i, te, nu: (layer, 0, te[i])),
                      pl.BlockSpec((MOE_TM, 1), lambda i, te, nu: (i, 0))],
            out_specs=pl.BlockSpec((MOE_TM, f), lambda i, te, nu: (i, 0)),
            scratch_shapes=[pltpu.VMEM((d, f), BF16), pltpu.VMEM((d, f), BF16)]),
        out_shape=jax.ShapeDtypeStruct((n_rows, f), BF16),
        compiler_params=_cparams(1), name="moe_up",
    )(tile_e, n_used, xs, w_gate, w_up, row_w.reshape(n_rows, 1))
    rows = pl.pallas_call(
        _moe_down_kernel,
        grid_spec=pltpu.PrefetchScalarGridSpec(
            num_scalar_prefetch=2, grid=(n_tiles,),
            in_specs=[pl.BlockSpec((MOE_TM, f), lambda i, te, nu: (i, 0)),
                      pl.BlockSpec((None, f, d), lambda i, te, nu: (layer, te[i], 0))],
            out_specs=pl.BlockSpec((MOE_TM, d), lambda i, te, nu: (i, 0)),
            scratch_shapes=[pltpu.VMEM((f, d), BF16)]),
        out_shape=jax.ShapeDtypeStruct((n_rows, d), F32),
        compiler_params=_cparams(1), name="moe_down",
    )(tile_e, n_used, hid, w_down)

    return _moe_combine(x, rows, pos.reshape(-1), next_gain, split)


def _combine_kernel(pos_ref, x_ref, rows_hbm, *refs, top_k, with_norm, split_tile):
    if with_norm:
        g_ref, xo_ref, h_ref, buf, sem = refs
    else:
        lo_ref, hi_ref, buf, sem = refs
    i = pl.program_id(0)
    n_tiles = pl.num_programs(0)
    tm = x_ref.shape[0]
    slot = i % 2

    def gather_copy(tile, t, k, slot_):
        r = pos_ref[(tile * tm + t) * top_k + k]
        return pltpu.make_async_copy(rows_hbm.at[pl.ds(r, 1)],
                                     buf.at[slot_, pl.ds(k * tm + t, 1)], sem.at[slot_])

    def start_tile(tile, slot_):
        def body(t, carry):
            for k in range(top_k):
                gather_copy(tile, t, k, slot_).start()
            return carry
        lax.fori_loop(0, tm, body, 0, unroll=8)

    @pl.when(i == 0)
    def _():
        start_tile(0, 0)

    @pl.when(i + 1 < n_tiles)
    def _():
        start_tile(i + 1, 1 - slot)

    pltpu.make_async_copy(rows_hbm.at[pl.ds(0, top_k * tm)], buf.at[slot], sem.at[slot]).wait()
    y = x_ref[...]
    for k in range(top_k):
        y = y + buf[slot, k * tm:(k + 1) * tm]
    if with_norm:
        xo_ref[...] = y
        h_ref[...] = (y * lax.rsqrt(jnp.mean(y * y, axis=-1, keepdims=True) + RMS_EPS)
                      * g_ref[...]).astype(BF16)
    else:
        @pl.when(i < split_tile)
        def _():
            lo_ref[...] = y

        @pl.when(i >= split_tile)
        def _():
            hi_ref[...] = y


def _moe_combine(x, rows, pos, next_gain, split):
    m, d = x.shape
    top_k = pos.shape[0] // m
    tm = LANE
    assert m % tm == 0
    with_norm = next_gain is not None
    x_spec = pl.BlockSpec((tm, d), lambda i, p: (i, 0))
    in_specs = [x_spec, pl.BlockSpec(memory_space=pl.ANY)]
    args = [x, rows]
    if with_norm:
        split_tile = 0
        in_specs.append(pl.BlockSpec((1, d), lambda i, p: (0, 0)))
        args.append(next_gain.reshape(1, d))
        out_specs = [x_spec, x_spec]
        out_shape = [jax.ShapeDtypeStruct((m, d), F32), jax.ShapeDtypeStruct((m, d), BF16)]
    else:
        assert split % tm == 0
        split_tile = split // tm
        out_specs = [pl.BlockSpec((tm, d), lambda i, p: (jnp.minimum(i, split_tile - 1), 0)),
                     pl.BlockSpec((tm, d), lambda i, p: (jnp.maximum(i - split_tile, 0), 0))]
        out_shape = [jax.ShapeDtypeStruct((split, d), F32), jax.ShapeDtypeStruct((m - split, d), F32)]
    return pl.pallas_call(
        functools.partial(_combine_kernel, top_k=top_k, with_norm=with_norm, split_tile=split_tile),
        grid_spec=pltpu.PrefetchScalarGridSpec(
            num_scalar_prefetch=1, grid=(m // tm,), in_specs=in_specs, out_specs=out_specs,
            scratch_shapes=[pltpu.VMEM((2, top_k * tm, d), F32), pltpu.SemaphoreType.DMA((2,))]),
        out_shape=out_shape, compiler_params=_cparams(1),
        name="moe_combine_norm" if with_norm else "moe_combine_split",
    )(pos, *args)


def _even_layer(x, h, i, groups, state_ret, state_rwkv, state_shift, wts):
    (a_w_in, a_w_out, ret_norm_g, rwkv_mu, rwkv_w0, rwkv_w2, rwkv_a0, rwkv_a2, rwkv_g2,
     rwkv_kk_scale, rwkv_ka, rwkv_rk, rwkv_lnx_g, rwkv_lnx_b) = [w[i] for w in wts]
    ret_heads, ret_dk = state_ret.shape[2], state_ret.shape[3]
    rw_heads, rw_n = state_rwkv.shape[2], state_rwkv.shape[3]
    ret_w = ret_heads * ret_dk
    rw_w = rw_heads * rw_n
    lora_w, lora_a, lora_g = rwkv_w2.shape[0], rwkv_a2.shape[0], rwkv_g2.shape[0]
    assert lora_w == LANE and lora_a == LANE and lora_g <= LANE and ret_w == rw_w
    n_main = 4 * ret_w + 3 * rw_w
    n_shift = 3 * rw_w + lora_w + lora_a + lora_g
    (pb, pt), (sb, st) = groups
    m_p, m_s = pb * pt, sb * st

    a_w_in_t = a_w_in.T
    p = _matmul([h], a_w_in_t, n_main, name="a_in", w_t=True)
    tail_cols = MM_TN
    assert n_main % tail_cols == 0 and n_main + tail_cols >= a_w_in.shape[1] and 3 * LANE <= tail_cols
    tail = _matmul([h], a_w_in_t, tail_cols, col_block0=n_main // tail_cols, name="a_in_tail",
                   w_t=True)

    o_ret_p, ret_p = _retention_prompt(p, ret_norm_g, pb, pt, ret_heads, ret_dk)
    o_ret_s, ret_s = _retention_sample(p, ret_norm_g, state_ret[i], m_p, sb, st, ret_heads, ret_dk)

    pad_g = lambda a: jnp.pad(a, ((0, LANE - a.shape[0]), (0, 0)))
    mu = rwkv_mu
    mu_tail = jnp.pad(mu[3 * rw_w:], (0, tail_cols - (n_shift - 3 * rw_w)))
    prep_params = [mu[:rw_w].reshape(1, -1), mu[rw_w:2 * rw_w].reshape(1, -1),
                   mu[2 * rw_w:3 * rw_w].reshape(1, -1), mu_tail.reshape(1, -1),
                   rwkv_w0.reshape(1, -1), rwkv_w2, rwkv_a0.reshape(1, -1), rwkv_a2,
                   pad_g(rwkv_g2)]
    cb0 = (4 * ret_w) // rw_w
    shift0 = state_shift[i]
    sh_main = jnp.repeat(shift0[:, :3 * rw_w], st, axis=0)
    sh_tail = jnp.repeat(jnp.pad(shift0[:, 3 * rw_w:], ((0, 0), (0, tail_cols - (n_shift - 3 * rw_w)))),
                         st, axis=0)
    seq_p = _rwkv_prep(p, tail, 0, m_p, cb0, rw_w, prep_params, None, pt, st)
    seq_s = _rwkv_prep(p, tail, m_p, m_s, cb0, rw_w, prep_params, (sh_main, sh_tail), pt, st)

    kparams = (rwkv_kk_scale, rwkv_ka, rwkv_rk, rwkv_lnx_g, rwkv_lnx_b)
    y_p, rwkv_p = _rwkv_scan_tiled(seq_p, kparams, pb, pt, rw_heads, rw_n)
    y_s, rwkv_s = _rwkv_scan(seq_s, kparams, state_rwkv[i], sb, st, rw_heads, rw_n)

    o_ret = jnp.concatenate([o_ret_p, o_ret_s], axis=0)
    y = jnp.concatenate([y_p, y_s], axis=0)
    x = _matmul([o_ret, y], a_w_out, a_w_out.shape[1], res=x, name="a_out")

    def last_ps(row0, nb, nt):
        def last(a, lo, hi):
            if nb * nt <= 1024:
                return a[row0:row0 + nb * nt, lo:hi].reshape(nb, nt, hi - lo)[:, nt - 1]
            return jnp.concatenate([a[row0 + (b + 1) * nt - 1:row0 + (b + 1) * nt, lo:hi]
                                    for b in range(nb)], axis=0)
        return jnp.concatenate([last(p, 4 * ret_w, n_main), last(tail, 0, n_shift - 3 * rw_w)],
                               axis=-1)

    return x, (ret_p, rwkv_p, last_ps(0, pb, pt)), (ret_s, rwkv_s, last_ps(m_p, sb, st))


def _odd_layer(x, h, i, groups, cache_k, cache_v, wts):
    c_w_in, c_w_out, c_q_norm_g, c_k_norm_g, c_sinks = [w[i] for w in wts]
    win, n_kv, hd = cache_k.shape[2], cache_k.shape[3], cache_k.shape[4]
    n_heads = c_sinks.shape[0]
    group = n_heads // n_kv
    mix_c = n_heads * hd
    kv_w = n_kv * hd
    (pb, pt), (sb, st) = groups
    m_p = pb * pt
    assert pt % win == 0 and mix_c % kv_w == 0
    p = _matmul([h], c_w_in, mix_c + 2 * kv_w, name="c_in")
    kcb, vcb = mix_c // kv_w, mix_c // kv_w + 1

    nb_p = m_p // win
    maps = (lambda n: (n, 0), lambda n: (n, kcb), lambda n: (n, vcb),
            lambda n: (jnp.maximum(n - 1, 0), kcb), lambda n: (jnp.maximum(n - 1, 0), vcb))
    o_p, kn_p = _swa(p, p, p, p, p, maps, nb_p, (win, mix_c), win, False, win, c_q_norm_g,
                     c_k_norm_g, c_sinks, n_kv, group, hd, True, False, pt // win)

    tq = 16
    ps = p[m_p:].reshape(sb, st, -1)
    q_s = jnp.pad(ps[:, :, :mix_c].reshape(sb, st, n_kv, group, hd),
                  ((0, 0), (0, tq - st), (0, 0), (0, 0), (0, 0)))
    q_s = q_s.transpose(0, 3, 1, 2, 4).reshape(sb * group * tq, kv_w)
    kv_pad = lambda a: jnp.pad(a, ((0, 0), (0, win - st), (0, 0))).reshape(sb * win, kv_w)
    k_s = kv_pad(ps[:, :, mix_c:mix_c + kv_w])
    v_s = kv_pad(ps[:, :, mix_c + kv_w:])
    kc0 = cache_k[i].reshape(sb * win, kv_w)
    vc0 = cache_v[i].reshape(sb * win, kv_w)
    same = lambda n: (n, 0)
    o_s, kn_s = _swa(q_s, k_s, v_s, kc0, vc0, (same,) * 5, sb, (group * tq, kv_w), tq, True, win,
                     c_q_norm_g, c_k_norm_g, c_sinks, n_kv, group, hd, False, True, 1)
    o_s = o_s.reshape(sb, group, tq, n_kv, hd)[:, :, :st].transpose(0, 2, 3, 1, 4).reshape(
        sb * st, mix_c)

    x = _matmul([jnp.concatenate([o_p, o_s], axis=0)], c_w_out, c_w_out.shape[1], res=x, name="c_out")

    new_k_p = kn_p.reshape(pb, pt, n_kv, hd)[:, -win:]
    new_v_p = p[:m_p, mix_c + kv_w:].reshape(pb, pt, n_kv, hd)[:, -win:]
    kn_s = kn_s.reshape(sb, win, n_kv, hd)[:, :st]
    new_k_s = jnp.concatenate([cache_k[i], kn_s], axis=1)[:, -win:]
    new_v_s = jnp.concatenate([cache_v[i], ps[:, :, mix_c + kv_w:].reshape(sb, st, n_kv, hd)],
                              axis=1)[:, -win:]
    return x, (new_k_p, new_v_p), (new_k_s, new_v_s)


def kernel(x_prompt, x_sample, state_ret, state_rwkv, state_rwkv_shift, cache_swa_k, cache_swa_v, norm_mix_g, norm_ffn_g, a_w_in, a_w_out, ret_norm_g, rwkv_mu, rwkv_w0, rwkv_w2, rwkv_a0, rwkv_a2, rwkv_g2, rwkv_kk_scale, rwkv_ka, rwkv_rk, rwkv_lnx_g, rwkv_lnx_b, c_w_in, c_w_out, c_q_norm_g, c_k_norm_g, c_sinks, moe_w_group, moe_b_group, moe_w_expert, moe_b_expert, moe_w_gate, moe_w_up, moe_w_down):
    pb, pt, d = x_prompt.shape
    sb, st, _ = x_sample.shape
    groups = ((pb, pt), (sb, st))
    depth = norm_mix_g.shape[0]
    a_wts = (a_w_in, a_w_out, ret_norm_g, rwkv_mu, rwkv_w0, rwkv_w2, rwkv_a0, rwkv_a2, rwkv_g2,
             rwkv_kk_scale, rwkv_ka, rwkv_rk, rwkv_lnx_g, rwkv_lnx_b)
    c_wts = (c_w_in, c_w_out, c_q_norm_g, c_k_norm_g, c_sinks)
    st_p = [[] for _ in range(5)]
    st_s = [[] for _ in range(5)]
    for l in range(depth):
        i = l // 2
        if l == 0:
            h, x = _rms_norm_first(x_prompt.reshape(pb * pt, d), x_sample.reshape(sb * st, d),
                                   norm_mix_g[l])
        else:
            h = h_next
        if l % 2 == 0:
            x, sp, ss = _even_layer(x, h, i, groups, state_ret, state_rwkv, state_rwkv_shift, a_wts)
            for j in range(3):
                st_p[j].append(sp[j])
                st_s[j].append(ss[j])
        else:
            x, sp, ss = _odd_layer(x, h, i, groups, cache_swa_k, cache_swa_v, c_wts)
            for j in range(2):
                st_p[3 + j].append(sp[j])
                st_s[3 + j].append(ss[j])
        n_router = moe_w_group.shape[2] + moe_w_expert.shape[2]
        w_router = jnp.pad(jnp.concatenate([moe_w_group[l], moe_w_expert[l]], axis=1),
                           ((0, 0), (0, LANE - n_router)))
        h, logits = _rms_norm(x, norm_ffn_g[l], w_router)
        moe_args = (x, h, logits, moe_b_group[l], moe_b_expert[l], moe_w_gate, moe_w_up, moe_w_down, l)
        if l < depth - 1:
            x, h_next = _moe(*moe_args, next_gain=norm_mix_g[l + 1])
        else:
            y_p, y_s = _moe(*moe_args, split=pb * pt)
    y_prompt = y_p.reshape(pb, pt, d)
    y_sample = y_s.reshape(sb, st, d)
    stack = lambda s: s[0][None] if len(s) == 1 else jnp.stack(s)
    return (y_prompt, y_sample) + tuple(stack(s) for s in st_p) + tuple(stack(s) for s in st_s)
```

```python
import functools

import jax
import jax.numpy as jnp
from jax import lax
from jax.experimental import pallas as pl
from jax.experimental.pallas import tpu as pltpu

F32 = jnp.float32
BF16 = jnp.bfloat16

RMS_EPS = 1e-6
RWKV_LN_EPS = 64e-5
RET_CHUNK = 128

LANE = 128
VMEM_LIMIT = 56 * 1024 * 1024

MM_TM = 1040
MM_TN = 512
NORM_TM = 320
MOE_TM = 256
SCAN_TC = 32
RET_HEADS_PER_STEP = 4


def _cparams(n_axes):
    return pltpu.CompilerParams(dimension_semantics=("arbitrary",) * n_axes,
                                vmem_limit_bytes=VMEM_LIMIT)


def _norm_kernel(x_ref, g_ref, h_ref):
    x = x_ref[...]
    y = x * lax.rsqrt(jnp.mean(x * x, axis=-1, keepdims=True) + RMS_EPS) * g_ref[...]
    h_ref[...] = y.astype(BF16)


def _norm_router_kernel(x_ref, g_ref, wr_ref, h_ref, logit_ref):
    x = x_ref[...]
    y = x * lax.rsqrt(jnp.mean(x * x, axis=-1, keepdims=True) + RMS_EPS) * g_ref[...]
    h_ref[...] = y.astype(BF16)
    logit_ref[...] = jnp.dot(y, wr_ref[...], precision=lax.Precision.HIGHEST,
                             preferred_element_type=F32)


def _norm_first_kernel(xp_ref, xs_ref, g_ref, h_ref, x_ref, *, n_prompt_tiles):
    x = jnp.where(pl.program_id(0) < n_prompt_tiles, xp_ref[...], xs_ref[...])
    x_ref[...] = x
    y = x * lax.rsqrt(jnp.mean(x * x, axis=-1, keepdims=True) + RMS_EPS) * g_ref[...]
    h_ref[...] = y.astype(BF16)


def _rms_norm_first(xp, xs, g):
    (m_p, d), m_s = xp.shape, xs.shape[0]
    tm = LANE
    assert m_p % tm == 0 and m_s % tm == 0
    npt = m_p // tm
    spec = pl.BlockSpec((tm, d), lambda i: (i, 0))
    return pl.pallas_call(
        functools.partial(_norm_first_kernel, n_prompt_tiles=npt), grid=((m_p + m_s) // tm,),
        in_specs=[pl.BlockSpec((tm, d), lambda i: (jnp.minimum(i, npt - 1), 0)),
                  pl.BlockSpec((tm, d), lambda i: (jnp.maximum(i - npt, 0), 0)),
                  pl.BlockSpec((1, d), lambda i: (0, 0))],
        out_specs=[spec, spec],
        out_shape=[jax.ShapeDtypeStruct((m_p + m_s, d), BF16),
                   jax.ShapeDtypeStruct((m_p + m_s, d), F32)],
        compiler_params=_cparams(1), name="rms_norm_first")(xp, xs, g.reshape(1, d))


def _rms_norm(x, g, w_router=None):
    m, d = x.shape
    tm = NORM_TM if m % NORM_TM == 0 else m
    grid = (m // tm,)
    x_spec = pl.BlockSpec((tm, d), lambda i: (i, 0))
    g_spec = pl.BlockSpec((1, d), lambda i: (0, 0))
    if w_router is None:
        return pl.pallas_call(
            _norm_kernel, grid=grid, in_specs=[x_spec, g_spec], out_specs=x_spec,
            out_shape=jax.ShapeDtypeStruct((m, d), BF16), compiler_params=_cparams(1),
            name="rms_norm")(x, g.reshape(1, d))
    nr = w_router.shape[1]
    return pl.pallas_call(
        _norm_router_kernel, grid=grid,
        in_specs=[x_spec, g_spec, pl.BlockSpec((d, nr), lambda i: (0, 0))],
        out_specs=[x_spec, pl.BlockSpec((tm, nr), lambda i: (i, 0))],
        out_shape=[jax.ShapeDtypeStruct((m, d), BF16), jax.ShapeDtypeStruct((m, nr), F32)],
        compiler_params=_cparams(1), name="rms_norm_router")(x, g.reshape(1, d), w_router)


def _mm_kernel(*refs, n_parts, has_res, cols_valid, w_t):
    x_refs = refs[:n_parts]
    w_refs = refs[n_parts:2 * n_parts]
    pos = 2 * n_parts
    res_ref = refs[pos] if has_res else None
    pos += int(has_res)
    o_ref = refs[pos]
    wb_refs = refs[pos + 1:]
    n_axis = 0 if w_t else 1

    @pl.when(pl.program_id(1) == 0)
    def _():
        for w_ref, wb_ref in zip(w_refs, wb_refs):
            w = w_ref[...]
            if cols_valid is not None:
                col = (lax.broadcasted_iota(jnp.int32, w.shape, n_axis)
                       + pl.program_id(0) * w.shape[n_axis])
                w = jnp.where(col < cols_valid, w, 0.0)
            wb_ref[...] = w.astype(BF16)

    acc = None
    for x_ref, wb_ref in zip(x_refs, wb_refs):
        d = lax.dot_general(x_ref[...], wb_ref[...], (((1,), (1 if w_t else 0,)), ((), ())),
                            preferred_element_type=F32)
        acc = d if acc is None else acc + d
    if has_res:
        acc = acc + res_ref[...]
    o_ref[...] = acc


def _matmul(xs, w, n_out, col_block0=0, tn=MM_TN, res=None, name="proj", w_t=False):
    k_total, n_total = (w.shape[1], w.shape[0]) if w_t else w.shape
    over = (col_block0 * tn + n_out) > n_total
    cols_valid = n_total - col_block0 * tn if over else None
    m = xs[0].shape[0]
    tm = MM_TM if m % MM_TM == 0 else m
    assert n_out % tn == 0
    grid = (n_out // tn, m // tm)
    in_specs, w_specs, scratch = [], [], []
    row = 0
    for x in xs:
        kp = x.shape[1]
        in_specs.append(pl.BlockSpec((tm, kp), lambda j, i: (i, 0)))
        assert row % kp == 0
        rb = row // kp
        if w_t:
            w_specs.append(pl.BlockSpec((tn, kp), lambda j, i, rb=rb: (j + col_block0, rb)))
            scratch.append(pltpu.VMEM((tn, kp), BF16))
        else:
            w_specs.append(pl.BlockSpec((kp, tn), lambda j, i, rb=rb: (rb, j + col_block0)))
            scratch.append(pltpu.VMEM((kp, tn), BF16))
        row += kp
    assert row == k_total
    args = list(xs) + [w] * len(xs)
    in_specs = in_specs + w_specs
    if res is not None:
        in_specs.append(pl.BlockSpec((tm, tn), lambda j, i: (i, j)))
        args.append(res)
    return pl.pallas_call(
        functools.partial(_mm_kernel, n_parts=len(xs), has_res=res is not None,
                          cols_valid=cols_valid, w_t=w_t),
        grid=grid, in_specs=in_specs,
        out_specs=pl.BlockSpec((tm, tn), lambda j, i: (i, j)),
        out_shape=jax.ShapeDtypeStruct((m, n_out), F32),
        scratch_shapes=scratch, compiler_params=_cparams(2), name=name)(*args)


def _ret_finish(o, g, gain):
    o = o * lax.rsqrt(jnp.mean(o * o, axis=-1, keepdims=True) + RMS_EPS) * gain
    return ((g / (1.0 + jnp.exp(-g))) * o).astype(BF16)


def _ret_prompt_kernel(q_ref, k_ref, v_ref, g_ref, gain_ref, lg_ref, o_ref, s_ref, *, dk):
    c = pl.program_id(2)
    chunk = q_ref.shape[0]

    @pl.when(c == 0)
    def _():
        s_ref[...] = jnp.zeros_like(s_ref)

    row = lax.broadcasted_iota(jnp.int32, (chunk, chunk), 0)
    col = lax.broadcasted_iota(jnp.int32, (chunk, chunk), 1)
    diff = (row - col).astype(F32)
    causal = diff >= 0
    dist = jnp.where(causal, diff, 0.0)
    pos = lax.broadcasted_iota(jnp.int32, (chunk, 1), 0).astype(F32)

    for hh in range(s_ref.shape[0]):
        sl = slice(hh * dk, (hh + 1) * dk)
        lg = lg_ref[hh][:, :1]
        decay_mask = jnp.where(causal, jnp.exp(dist * lg), 0.0)
        q_decay = jnp.exp((pos + 1.0) * lg)
        k_decay = jnp.exp((chunk - 1.0 - pos) * lg)
        chunk_decay = jnp.exp(chunk * lg)
        q = q_ref[:, sl]
        k = k_ref[:, sl] * (dk ** -0.5)
        vb = v_ref[:, sl].astype(BF16)
        s = s_ref[hh]
        scores = lax.dot_general(q.astype(BF16), k.astype(BF16), (((1,), (1,)), ((), ())),
                                 preferred_element_type=F32) * decay_mask
        inner = jnp.dot(scores.astype(BF16), vb, preferred_element_type=F32)
        cross = jnp.dot((q * q_decay).astype(BF16), s.astype(BF16), preferred_element_type=F32)
        s_ref[hh] = chunk_decay * s + lax.dot_general(
            (k * k_decay).astype(BF16), vb, (((0,), (0,)), ((), ())), preferred_element_type=F32)
        o_ref[:, sl] = _ret_finish(inner + cross, g_ref[:, sl], gain_ref[hh])


def _ret_sample_kernel(q_ref, k_ref, v_ref, g_ref, gain_ref, lg_ref, s0_ref, o_ref, s_ref, acc_ref,
                       *, dk, t_len):
    b = pl.program_id(1)
    rows = q_ref.shape[0]
    r_b = lax.broadcasted_iota(jnp.int32, (rows, 1), 0) // t_len
    r_t = (lax.broadcasted_iota(jnp.int32, (rows, 1), 0) % t_len).astype(F32)
    sel = r_b == b

    for hh in range(s_ref.shape[0]):
        sl = slice(hh * dk, (hh + 1) * dk)
        lg = lg_ref[hh][:, :1]
        q = q_ref[:, sl]
        k = k_ref[:, sl] * (dk ** -0.5)
        vb = v_ref[:, sl].astype(BF16)

        @pl.when(b == 0)
        def _(hh=hh, lg=lg, q=q, k=k, vb=vb):
            row = lax.broadcasted_iota(jnp.int32, (rows, rows), 0)
            col = lax.broadcasted_iota(jnp.int32, (rows, rows), 1)
            diff = (row % t_len - col % t_len).astype(F32)
            ok = (row // t_len == col // t_len) & (diff >= 0)
            mask = jnp.where(ok, jnp.exp(jnp.where(ok, diff, 0.0) * lg), 0.0)
            scores = lax.dot_general(q.astype(BF16), k.astype(BF16), (((1,), (1,)), ((), ())),
                                     preferred_element_type=F32) * mask
            acc_ref[hh] = jnp.dot(scores.astype(BF16), vb, preferred_element_type=F32)

        s0 = s0_ref[hh]
        cross = jnp.dot((q * jnp.exp((r_t + 1.0) * lg)).astype(BF16), s0.astype(BF16),
                        preferred_element_type=F32)
        acc_ref[hh] += jnp.where(sel, cross, 0.0)
        k_decay = jnp.where(sel, jnp.exp((t_len - 1.0 - r_t) * lg), 0.0)
        s_ref[hh] = jnp.exp(t_len * lg) * s0 + lax.dot_general(
            (k * k_decay).astype(BF16), vb, (((0,), (0,)), ((), ())), preferred_element_type=F32)

        @pl.when(b == pl.num_programs(1) - 1)
        def _(hh=hh, sl=sl):
            o_ref[:, sl] = _ret_finish(acc_ref[hh], g_ref[:, sl], gain_ref[hh])


def _ret_log_decay(n_heads):
    lg = jnp.log(1.0 - 2.0 ** (-5.0 - jnp.arange(n_heads, dtype=F32)))
    return jnp.broadcast_to(lg[:, None, None], (n_heads, 1, LANE))


def _retention_prompt(p, gain, n_batch, seq, n_heads, dk):
    nc = seq // RET_CHUNK
    hps = RET_HEADS_PER_STEP
    assert n_heads % hps == 0
    ng = n_heads // hps
    col = lambda grp: (lambda b, h, c: (b * nc + c, grp * ng + h))
    blk = lambda grp: pl.BlockSpec((RET_CHUNK, hps * dk), col(grp))
    return pl.pallas_call(
        functools.partial(_ret_prompt_kernel, dk=dk),
        grid=(n_batch, ng, nc),
        in_specs=[blk(0), blk(1), blk(2), blk(3),
                  pl.BlockSpec((hps, 1, dk), lambda b, h, c: (h, 0, 0)),
                  pl.BlockSpec((hps, 1, LANE), lambda b, h, c: (h, 0, 0))],
        out_specs=[pl.BlockSpec((RET_CHUNK, hps * dk), lambda b, h, c: (b * nc + c, h)),
                   pl.BlockSpec((None, hps, dk, dk), lambda b, h, c: (b, h, 0, 0))],
        out_shape=[jax.ShapeDtypeStruct((n_batch * seq, n_heads * dk), BF16),
                   jax.ShapeDtypeStruct((n_batch, n_heads, dk, dk), F32)],
        compiler_params=_cparams(3), name="retention_prompt",
    )(p, p, p, p, gain.reshape(n_heads, 1, dk), _ret_log_decay(n_heads))


def _retention_sample(p, gain, s0, row0, n_batch, t_len, n_heads, dk):
    rows = n_batch * t_len
    rb = row0 // rows
    hps = RET_HEADS_PER_STEP
    assert n_heads % hps == 0
    ng = n_heads // hps
    blk = lambda grp: pl.BlockSpec((rows, hps * dk), lambda h, b, grp=grp: (rb, grp * ng + h))
    return pl.pallas_call(
        functools.partial(_ret_sample_kernel, dk=dk, t_len=t_len),
        grid=(ng, n_batch),
        in_specs=[blk(0), blk(1), blk(2), blk(3),
                  pl.BlockSpec((hps, 1, dk), lambda h, b: (h, 0, 0)),
                  pl.BlockSpec((hps, 1, LANE), lambda h, b: (h, 0, 0)),
                  pl.BlockSpec((None, hps, dk, dk), lambda h, b: (b, h, 0, 0))],
        out_specs=[pl.BlockSpec((rows, hps * dk), lambda h, b: (0, h)),
                   pl.BlockSpec((None, hps, dk, dk), lambda h, b: (b, h, 0, 0))],
        out_shape=[jax.ShapeDtypeStruct((rows, n_heads * dk), BF16),
                   jax.ShapeDtypeStruct((n_batch, n_heads, dk, dk), F32)],
        scratch_shapes=[pltpu.VMEM((hps, rows, dk), F32)],
        compiler_params=_cparams(2), name="retention_sample",
    )(p, p, p, p, gain.reshape(n_heads, 1, dk), _ret_log_decay(n_heads), s0)


def _sigmoid(x):
    return 1.0 / (1.0 + jnp.exp(-x))


def _rwkv_prep_kernel(*refs, sample, seq, t_len):
    if sample:
        (r_c, k_c, v_c, t_c, r_s, k_s, v_s, t_s,
         mu_r, mu_k, mu_v, mu_t, w0, w2, a0, a2, g2,
         r_o, w_o, k_o, v_o, a_o, g_o) = refs
    else:
        (r_c, k_c, v_c, t_c, r_p, k_p, v_p, t_p,
         mu_r, mu_k, mu_v, mu_t, w0, w2, a0, a2, g2,
         r_o, w_o, k_o, v_o, a_o, g_o) = refs
    tm = r_c.shape[0]
    row = lax.broadcasted_iota(jnp.int32, (tm, 1), 0)
    seq_start = (pl.program_id(0) * tm) % seq == 0

    def mix(cur_ref, other_ref, mu_ref):
        cur = cur_ref[...]
        rolled = pltpu.roll(cur, 1, 0)
        if sample:
            prev = jnp.where(row % t_len == 0, other_ref[...], rolled)
        else:
            last = other_ref[7:8, :]
            first = jnp.where(seq_start, jnp.zeros_like(last), last)
            prev = jnp.where(row == 0, first, rolled)
        return cur + (prev - cur) * mu_ref[...]

    def store(o_ref, val):
        if len(o_ref.shape) == 2:
            o_ref[...] = val
        else:
            for c in range(o_ref.shape[1]):
                o_ref[:, c] = val[:, c * LANE:(c + 1) * LANE].reshape(o_ref.shape[0], 8, LANE)

    other = (r_s, k_s, v_s, t_s) if sample else (r_p, k_p, v_p, t_p)
    store(r_o, mix(r_c, other[0], mu_r))
    store(k_o, mix(k_c, other[1], mu_k))
    store(v_o, mix(v_c, other[2], mu_v))
    tail = mix(t_c, other[3], mu_t)
    lw, la, lgt = tail[:, :LANE], tail[:, LANE:2 * LANE], tail[:, 2 * LANE:3 * LANE]

    z = -(w0[...] + jnp.dot(jnp.tanh(lw).astype(BF16), w2[...].astype(BF16),
                            preferred_element_type=F32))
    softplus = jnp.maximum(z, 0.0) + jnp.log1p(jnp.exp(-jnp.abs(z)))
    store(w_o, jnp.exp(-jnp.exp(-softplus - 0.5)))
    store(a_o, _sigmoid(a0[...] + jnp.dot(la.astype(BF16), a2[...].astype(BF16),
                                          preferred_element_type=F32)))
    store(g_o, jnp.dot(_sigmoid(lgt).astype(BF16), g2[...].astype(BF16),
                       preferred_element_type=F32))


def _rwkv_prep(p, tail, row0, n_rows, col_block0, width, params, shift_rows, seq, t_len):
    sample = shift_rows is not None
    tm = n_rows if sample else 256
    rb0 = row0 // tm
    grid = (n_rows // tm,)
    cur = lambda grp: pl.BlockSpec((tm, width), lambda i, grp=grp: (rb0 + i, col_block0 + grp))
    tail_w = tail.shape[1]
    in_specs = [cur(0), cur(1), cur(2), pl.BlockSpec((tm, tail_w), lambda i: (rb0 + i, 0))]
    args = [p, p, p, tail]
    if sample:
        sh_main, sh_tail = shift_rows
        in_specs += [pl.BlockSpec((tm, width), lambda i, grp=grp: (0, grp)) for grp in range(3)]
        in_specs += [pl.BlockSpec((tm, tail_w), lambda i: (0, 0))]
        args += [sh_main, sh_main, sh_main, sh_tail]
    else:
        per = tm // 8
        prev_rb = lambda i: jnp.maximum((rb0 + i) * per - 1, 0)
        in_specs += [pl.BlockSpec((8, width), lambda i, grp=grp: (prev_rb(i), col_block0 + grp))
                     for grp in range(3)]
        in_specs += [pl.BlockSpec((8, tail_w), lambda i: (prev_rb(i), 0))]
        args += [p, p, p, tail]
    full = lambda a: pl.BlockSpec(a.shape, lambda i: (0,) * a.ndim)
    in_specs += [full(a) for a in params]
    args += list(params)
    if sample:
        out_spec = pl.BlockSpec((tm, width), lambda i: (i, 0))
        out_shape = jax.ShapeDtypeStruct((n_rows, width), F32)
    else:
        out_spec = pl.BlockSpec((tm // 8, width // LANE, 8, LANE), lambda i: (i, 0, 0, 0))
        out_shape = jax.ShapeDtypeStruct((n_rows // 8, width // LANE, 8, LANE), F32)
    return pl.pallas_call(
        functools.partial(_rwkv_prep_kernel, sample=sample, seq=seq, t_len=t_len),
        grid=grid, in_specs=in_specs, out_specs=[out_spec] * 6, out_shape=[out_shape] * 6,
        compiler_params=_cparams(1), name="rwkv_prep_sample" if sample else "rwkv_prep_prompt",
    )(*args)


def _rwkv_step(s_ref, ys_ref, r, w, kr, v, a, g, par_refs, between=None):
    kks_ref, ka_ref, rk_ref, lng_ref, lnb_ref = par_refs
    kks = kr * kks_ref[...]
    kk = kks / jnp.maximum(jnp.sqrt(jnp.sum(kks * kks, axis=0, keepdims=True)), 1e-12)
    kf = kr * (1.0 + (a - 1.0) * ka_ref[...])
    kka = kk * a
    for vi in range(s_ref.shape[0]):
        s = s_ref[vi]
        sa = jnp.sum(s * kk, axis=0, keepdims=True)
        s = s * w - sa * kka + v[vi:vi + 1, :] * kf
        s_ref[vi] = s
        ys_ref[vi:vi + 1, :] = jnp.sum(s * r, axis=0, keepdims=True)
        if between is not None:
            between(vi)
    y = ys_ref[...]
    mu = jnp.mean(y, axis=0, keepdims=True)
    var = jnp.mean(jnp.square(y - mu), axis=0, keepdims=True)
    y = (y - mu) * lax.rsqrt(var + RWKV_LN_EPS) * lng_ref[...] + lnb_ref[...]
    bonus = jnp.sum(r * kf * rk_ref[...], axis=0, keepdims=True) * v
    return (y + bonus) * g


def _rwkv_scan_kernel(r_ref, w_ref, k_ref, v_ref, a_ref, g_ref,
                      kks_ref, ka_ref, rk_ref, lng_ref, lnb_ref, s0_ref,
                      y_ref, s_ref, ys_ref):
    par_refs = (kks_ref, ka_ref, rk_ref, lng_ref, lnb_ref)

    @pl.when(pl.program_id(1) == 0)
    def _():
        s_ref[...] = s0_ref[...]

    def step(t, carry):
        y = _rwkv_step(s_ref, ys_ref, r_ref[t], w_ref[t], k_ref[t], v_ref[t], a_ref[t], g_ref[t],
                       par_refs)
        y_ref[t] = y.astype(y_ref.dtype)
        return carry

    lax.fori_loop(0, r_ref.shape[0], step, 0)


def _rwkv_scan_tiled_kernel(r_ref, w_ref, k_ref, v_ref, a_ref, g_ref,
                            kks_ref, ka_ref, rk_ref, lng_ref, lnb_ref, s0_ref,
                            y_ref, s_ref, ys_ref, nxt_ref):
    seq_refs = (r_ref, w_ref, k_ref, v_ref, a_ref, g_ref)
    par_refs = (kks_ref, ka_ref, rk_ref, lng_ref, lnb_ref)
    n_grp = r_ref.shape[1]
    n = s_ref.shape[0]
    every = n // (len(seq_refs) + 1)

    @pl.when(pl.program_id(0) == 0)
    def _():
        s_ref[...] = s0_ref[...]

    def fetch(i, tg, t8, slot):
        zt = seq_refs[i][:, tg, :, t8, :].reshape(-1, LANE).T
        nxt_ref[slot, i] = jnp.concatenate([zt[:n], zt[n:]], axis=1)

    for i in range(len(seq_refs)):
        fetch(i, 0, 0, 0)

    def group(tg, carry):
        for t8 in range(8):
            slot = t8 % 2
            nxt_tg = tg if t8 < 7 else jnp.minimum(tg + 1, n_grp - 1)

            def between(vi, slot=slot, nxt_tg=nxt_tg, t8=t8):
                if vi % every == every - 1 and vi // every < len(seq_refs):
                    fetch(vi // every, nxt_tg, (t8 + 1) % 8, 1 - slot)

            y = _rwkv_step(s_ref, ys_ref, *[nxt_ref[slot, i] for i in range(len(seq_refs))],
                           par_refs, between)
            y_ref[tg * 8 + t8] = y.astype(y_ref.dtype)
        return carry

    lax.fori_loop(0, n_grp, group, 0)


def _rwkv_scan(seqs, kparams, s0, n_batch, t_len, n_heads, n):
    pairs = n_batch * n_heads
    tc = SCAN_TC if t_len % SCAN_TC == 0 else t_len
    grid = (pairs // LANE, t_len // tc)
    seq_spec = pl.BlockSpec((tc, n, LANE), lambda g, t: (t, 0, g))
    par_spec = pl.BlockSpec((n, LANE), lambda g, t: (0, g))
    st_spec = pl.BlockSpec((n, n, LANE), lambda g, t: (0, 0, g))
    pairs_major = lambda a: a.reshape(n_batch, t_len, n_heads, n).transpose(1, 3, 0, 2).reshape(
        t_len, n, pairs)
    params = [jnp.tile(a.reshape(n_heads, n).T, (1, n_batch)) for a in kparams]
    y, s_t = pl.pallas_call(
        _rwkv_scan_kernel, grid=grid,
        in_specs=[seq_spec] * 6 + [par_spec] * 5 + [st_spec],
        out_specs=[seq_spec, st_spec],
        out_shape=[jax.ShapeDtypeStruct((t_len, n, pairs), BF16),
                   jax.ShapeDtypeStruct((n, n, pairs), F32)],
        scratch_shapes=[pltpu.VMEM((n, LANE), F32)],
        compiler_params=_cparams(2), name="rwkv_scan",
    )(*[pairs_major(a) for a in seqs], *params, s0.transpose(2, 3, 0, 1).reshape(n, n, pairs))
    y = y.reshape(t_len, n, n_batch, n_heads).transpose(2, 0, 3, 1)
    s_t = s_t.reshape(n, n, n_batch, n_heads).transpose(2, 3, 0, 1)
    return y.reshape(n_batch * t_len, n_heads * n), s_t


def _rwkv_scan_tiled(seqs, kparams, n_batch, t_len, n_heads, n):
    chunks = n_heads * n // LANE
    assert n_batch * n_heads == LANE and 2 * n == LANE and t_len % SCAN_TC == 0 and SCAN_TC % 16 == 0
    tc = SCAN_TC
    seq_spec = pl.BlockSpec((n_batch, tc // 8, chunks, 8, LANE), lambda t: (0, t, 0, 0, 0))
    par_spec = pl.BlockSpec((n, LANE), lambda t: (0, 0))
    st_spec = pl.BlockSpec((n, n, LANE), lambda t: (0, 0, 0))
    params = [jnp.broadcast_to(a.reshape(chunks, 2, n).transpose(2, 1, 0)[:, :, None, :],
                               (n, 2, n_batch, chunks)).reshape(n, LANE) for a in kparams]
    y, s_t = pl.pallas_call(
        _rwkv_scan_tiled_kernel, grid=(t_len // tc,),
        in_specs=[seq_spec] * 6 + [par_spec] * 5 + [st_spec],
        out_specs=[pl.BlockSpec((tc, n, LANE), lambda t: (t, 0, 0)), st_spec],
        out_shape=[jax.ShapeDtypeStruct((t_len, n, LANE), BF16),
                   jax.ShapeDtypeStruct((n, n, LANE), F32)],
        scratch_shapes=[pltpu.VMEM((n, LANE), F32), pltpu.VMEM((2, 6, n, LANE), F32)],
        compiler_params=_cparams(1), name="rwkv_scan_tiled",
    )(*[a.reshape(n_batch, t_len // 8, chunks, 8, LANE) for a in seqs], *params,
      jnp.zeros((n, n, LANE), F32))
    y = y.reshape(t_len, n, 2, n_batch, chunks).transpose(3, 0, 4, 2, 1)
    s_t = s_t.reshape(n, n, 2, n_batch, chunks).transpose(3, 4, 2, 0, 1)
    return y.reshape(n_batch * t_len, n_heads * n), s_t.reshape(n_batch, n_heads, n, n)


def _swa_kernel(sink_ref, q_ref, kc_ref, vc_ref, kp_ref, vp_ref, qg_ref, kg_ref,
                o_ref, kn_ref, s_scr, e_scr, inv_scr, *, n_kv, group, hd, rows_per_head, stacked,
                norm_prev, first_has_prev, blocks_per_seq):
    rph = rows_per_head
    win = kc_ref.shape[0]
    n_heads = n_kv * group
    parts = 1 if stacked else 2
    has_prev = jnp.logical_or(first_has_prev, pl.program_id(0) % blocks_per_seq > 0)
    nt = (((1,), (1,)), ((), ()))

    def rms(x, g):
        return x * lax.rsqrt(jnp.mean(x * x, axis=-1, keepdims=True) + RMS_EPS) * g

    qpos = lax.broadcasted_iota(jnp.int32, (rph, 2 * win), 0)
    kpos = lax.broadcasted_iota(jnp.int32, (rph, 2 * win), 1) - win
    diff = qpos - kpos
    valid = (diff >= 0) & (diff < win) & (kpos >= jnp.where(has_prev, -win, 0))
    neg_dist = jnp.where(valid, -(diff.astype(F32)), -jnp.inf)
    qg = qg_ref[...]
    kg = kg_ref[...]
    qg2 = jnp.concatenate([qg, qg], axis=1)
    left = lax.broadcasted_iota(jnp.int32, (rph, 2 * hd), 1) < hd

    def keys_values(j):
        sl = slice(j * hd, (j + 1) * hd)
        kc = rms(kc_ref[:, sl], kg)
        kp = kp_ref[:, sl]
        if norm_prev:
            kp = rms(kp, kg)
        kb = jnp.concatenate([kp, kc], axis=0)
        vb = jnp.concatenate([vp_ref[:, sl], vc_ref[:, sl]], axis=0)
        return kc, kb, vb

    def halves(x):
        z = jnp.zeros_like(x)
        return (jnp.concatenate([x, z], axis=1).astype(BF16),
                jnp.concatenate([z, x], axis=1).astype(BF16))

    for j in range(n_kv):
        kc, kb, _ = keys_values(j)
        kn_ref[:, j * hd:(j + 1) * hd] = kc
        if stacked:
            q = rms(q_ref[:, j * hd:(j + 1) * hd], qg).astype(BF16)
            s_scr[j, 0] = lax.dot_general(q, kb.astype(BF16), nt, preferred_element_type=F32)
        else:
            qs = []
            for pp in range(group // 2):
                lo = (j * group + 2 * pp) * hd
                qp = q_ref[:, lo:lo + 2 * hd]
                sq = qp * qp
                tot = jnp.sum(sq, axis=-1, keepdims=True)
                lsum = jnp.sum(jnp.where(left, sq, 0.0), axis=-1, keepdims=True)
                r_l = lax.rsqrt(lsum / hd + RMS_EPS)
                r_r = lax.rsqrt((tot - lsum) / hd + RMS_EPS)
                qs.append((qp * jnp.where(left, r_l, r_r) * qg2).astype(BF16))
            q = jnp.concatenate(qs, axis=0)
            kb0, kb1 = halves(kb)
            s_scr[j, 0] = lax.dot_general(q, kb0, nt, preferred_element_type=F32)
            s_scr[j, 1] = lax.dot_general(q, kb1, nt, preferred_element_type=F32)

    for j in range(n_kv):
        for blk in range(group // parts):
            inv = []
            for part in range(parts):
                h = j * group + blk * parts + part
                rows = slice(blk * rph, (blk + 1) * rph)
                slope = 2.0 ** (-8.0 * (h + 1) / n_heads)
                sh = s_scr[j, part, rows, :] * (hd ** -0.5) + slope * neg_dist
                sink = sink_ref[h]
                m = jnp.maximum(jnp.max(sh, axis=-1, keepdims=True), sink)
                e = jnp.exp(sh - m)
                inv.append(1.0 / (jnp.sum(e, axis=-1, keepdims=True) + jnp.exp(sink - m)))
                e_scr[j, part, rows, :] = e.astype(BF16)
            if stacked:
                inv_scr[j, rows, :] = jnp.broadcast_to(inv[0], (rph, 2 * hd))
            else:
                inv_scr[j, rows, :] = jnp.where(left, inv[0], inv[1])

    for j in range(n_kv):
        _, _, vb = keys_values(j)
        if stacked:
            o = jnp.dot(e_scr[j, 0], vb.astype(BF16), preferred_element_type=F32)
            o_ref[:, j * hd:(j + 1) * hd] = (o * inv_scr[j][:, :hd]).astype(BF16)
        else:
            vb0, vb1 = halves(vb)
            o = (jnp.dot(e_scr[j, 0], vb0, preferred_element_type=F32)
                 + jnp.dot(e_scr[j, 1], vb1, preferred_element_type=F32))
            o = (o * inv_scr[j]).astype(BF16)
            for pp in range(group // 2):
                lo = (j * group + 2 * pp) * hd
                o_ref[:, lo:lo + 2 * hd] = o[pp * rph:(pp + 1) * rph]


def _swa(q_arr, kc_arr, vc_arr, kp_arr, vp_arr, maps, n_blocks, q_block, rows_per_head, stacked,
         win, qg, kg, sinks, n_kv, group, hd, norm_prev, first_has_prev, blocks_per_seq):
    q_map, kc_map, vc_map, kp_map, vp_map = maps
    parts = 1 if stacked else 2
    rows = group // parts * rows_per_head
    assert 2 * hd == LANE and group % 2 == 0
    return pl.pallas_call(
        functools.partial(_swa_kernel, n_kv=n_kv, group=group, hd=hd, rows_per_head=rows_per_head,
                          stacked=stacked, norm_prev=norm_prev, first_has_prev=first_has_prev,
                          blocks_per_seq=blocks_per_seq),
        grid=(n_blocks,),
        in_specs=[pl.BlockSpec(memory_space=pltpu.SMEM),
                  pl.BlockSpec(q_block, q_map),
                  pl.BlockSpec((win, n_kv * hd), kc_map),
                  pl.BlockSpec((win, n_kv * hd), vc_map),
                  pl.BlockSpec((win, n_kv * hd), kp_map),
                  pl.BlockSpec((win, n_kv * hd), vp_map),
                  pl.BlockSpec((1, hd), lambda i: (0, 0)),
                  pl.BlockSpec((1, hd), lambda i: (0, 0))],
        out_specs=[pl.BlockSpec(q_block, lambda i: (i, 0)),
                   pl.BlockSpec((win, n_kv * hd), lambda i: (i, 0))],
        out_shape=[jax.ShapeDtypeStruct((n_blocks * q_block[0], q_block[1]), BF16),
                   jax.ShapeDtypeStruct((n_blocks * win, n_kv * hd), F32)],
        scratch_shapes=[pltpu.VMEM((n_kv, parts, rows, 2 * win), F32),
                        pltpu.VMEM((n_kv, parts, rows, 2 * win), BF16),
                        pltpu.VMEM((n_kv, rows, LANE), F32)],
        compiler_params=_cparams(1), name="swa_stacked%d" % int(stacked),
    )(sinks, q_arr, kc_arr, vc_arr, kp_arr, vp_arr, qg.reshape(1, hd), kg.reshape(1, hd))


def _moe_up_kernel(te_ref, nu_ref, xs_ref, wg_ref, wu_ref, rw_ref, hid_ref, wgb_ref, wub_ref):
    i = pl.program_id(0)
    changed = jnp.logical_or(i == 0, te_ref[i] != te_ref[jnp.maximum(i - 1, 0)])

    @pl.when(changed)
    def _():
        wgb_ref[...] = wg_ref[...].astype(BF16)
        wub_ref[...] = wu_ref[...].astype(BF16)

    @pl.when(i < nu_ref[0])
    def _():
        x = xs_ref[...]
        g = jnp.dot(x, wgb_ref[...], preferred_element_type=F32)
        u = jnp.dot(x, wub_ref[...], preferred_element_type=F32)
        hid_ref[...] = ((g / (1.0 + jnp.exp(-g))) * u * rw_ref[...]).astype(BF16)

    @pl.when(i >= nu_ref[0])
    def _():
        hid_ref[...] = jnp.zeros_like(hid_ref)


def _moe_down_kernel(te_ref, nu_ref, hid_ref, wd_ref, out_ref, wdb_ref):
    i = pl.program_id(0)
    changed = jnp.logical_or(i == 0, te_ref[i] != te_ref[jnp.maximum(i - 1, 0)])

    @pl.when(changed)
    def _():
        wdb_ref[...] = wd_ref[...].astype(BF16)

    @pl.when(i < nu_ref[0])
    def _():
        out_ref[...] = jnp.dot(hid_ref[...], wdb_ref[...], preferred_element_type=F32)

    @pl.when(i >= nu_ref[0])
    def _():
        out_ref[...] = jnp.zeros_like(out_ref)


def _moe(x, h, logits, b_group, b_expert, w_gate, w_up, w_down, layer, next_gain=None, split=None):
    m, d = h.shape
    n_groups = b_group.shape[0]
    n_experts = b_expert.shape[0]
    per_group = n_experts // n_groups
    f = w_gate.shape[2] // n_experts
    top_k = 2

    g_logit = logits[:, :n_groups] + b_group
    g_prob = jax.nn.softmax(g_logit, axis=-1)
    g_idx = jnp.argmax(g_logit, axis=-1)
    g_gate = jnp.take_along_axis(g_prob, g_idx[:, None], axis=-1)
    e_logit = (logits[:, n_groups:n_groups + n_experts] + b_expert).reshape(m, n_groups, per_group)
    e_in = jnp.take_along_axis(e_logit, g_idx[:, None, None], axis=1)[:, 0]
    i1 = jnp.argmax(e_in, axis=-1)
    rest = jnp.where(jnp.arange(per_group)[None, :] == i1[:, None], -jnp.inf, e_in)
    i2 = jnp.argmax(rest, axis=-1)
    top_v = jnp.stack([jnp.max(e_in, axis=-1), jnp.max(rest, axis=-1)], axis=-1)
    top_i = jnp.stack([i1, i2], axis=-1)
    top_w = jax.nn.softmax(top_v, axis=-1) * g_gate
    eid = (g_idx[:, None] * per_group + top_i).astype(jnp.int32)

    n_pairs = m * top_k
    n_tiles = (n_pairs + n_experts * (MOE_TM - 1)) // MOE_TM + 1
    n_rows = n_tiles * MOE_TM
    flat_e = eid.reshape(-1)
    order = jnp.argsort(flat_e, stable=True).astype(jnp.int32)
    counts = jnp.dot(jnp.ones((8, n_pairs), BF16), jax.nn.one_hot(flat_e, n_experts, dtype=BF16),
                     preferred_element_type=F32)[0].astype(jnp.int32)
    padded = ((counts + MOE_TM - 1) // MOE_TM) * MOE_TM
    pad_end = jnp.cumsum(padded)
    pad_start = pad_end - padded
    start = jnp.cumsum(counts) - counts
    n_used = (pad_end[-1] // MOE_TM).astype(jnp.int32)
    tile_row = jnp.minimum(jnp.arange(n_tiles, dtype=jnp.int32), n_used - 1) * MOE_TM
    tile_e = jnp.minimum(jnp.sum(pad_end[None, :] <= tile_row[:, None], axis=1, dtype=jnp.int32),
                         n_experts - 1)
    row = jnp.arange(n_rows, dtype=jnp.int32)
    per_row = lambda table: jnp.repeat(table[tile_e], MOE_TM)
    in_group = row - per_row(pad_start)
    live = (in_group < per_row(counts)) & (row < n_used * MOE_TM)
    pair = order[jnp.clip(per_row(start) + in_group, 0, n_pairs - 1)]
    row_token = jnp.where(live, pair // top_k, row % m)
    row_w = jnp.where(live, top_w.reshape(-1)[pair], 0.0)
    pos = jnp.argsort(jnp.where(live, pair, n_pairs + row))[:n_pairs].astype(jnp.int32)
    pos = pos.reshape(m, top_k)
    n_used = n_used.reshape(1)

    xs = jnp.take(h, row_token, axis=0, mode="clip")
    hid = pl.pallas_call(
        _moe_up_kernel,
        grid_spec=pltpu.PrefetchScalarGridSpec(
            num_scalar_prefetch=2, grid=(n_tiles,),
            in_specs=[pl.BlockSpec((MOE_TM, d), lambda i, te, nu: (i, 0)),
                      pl.BlockSpec((None, d, f), lambda i, te, nu: (layer, 0, te[i])),
                      pl.BlockSpec((None, d, f), lambda i, te, nu: (layer, 0, te[i])),
                      pl.BlockSpec((MOE_TM, 1), lambda i, te, nu: (i, 0))],
            out_specs=pl.BlockSpec((MOE_TM, f), lambda i, te, nu: (i, 0)),
            scratch_shapes=[pltpu.VMEM((d, f), BF16), pltpu.VMEM((d, f), BF16)]),
        out_shape=jax.ShapeDtypeStruct((n_rows, f), BF16),
        compiler_params=_cparams(1), name="moe_up",
    )(tile_e, n_used, xs, w_gate, w_up, row_w.reshape(n_rows, 1))
    rows = pl.pallas_call(
        _moe_down_kernel,
        grid_spec=pltpu.PrefetchScalarGridSpec(
            num_scalar_prefetch=2, grid=(n_tiles,),
            in_specs=[pl.BlockSpec((MOE_TM, f), lambda i, te, nu: (i, 0)),
                      pl.BlockSpec((None, f, d), lambda i, te, nu: (layer, te[i], 0))],
            out_specs=pl.BlockSpec((MOE_TM, d), lambda i, te, nu: (i, 0)),
            scratch_shapes=[pltpu.VMEM((f, d), BF16)]),
        out_shape=jax.ShapeDtypeStruct((n_rows, d), F32),
        compiler_params=_cparams(1), name="moe_down",
    )(tile_e, n_used, hid, w_down)

    return _moe_combine(x, rows, pos.reshape(-1), next_gain, split)


def _combine_kernel(pos_ref, x_ref, rows_hbm, *refs, top_k, with_norm, split_tile):
    if with_norm:
        g_ref, xo_ref, h_ref, buf, sem = refs
    else:
        lo_ref, hi_ref, buf, sem = refs
    i = pl.program_id(0)
    n_tiles = pl.num_programs(0)
    tm = x_ref.shape[0]
    slot = i % 2

    def gather_copy(tile, t, k, slot_):
        r = pos_ref[(tile * tm + t) * top_k + k]
        return pltpu.make_async_copy(rows_hbm.at[pl.ds(r, 1)],
                                     buf.at[slot_, pl.ds(k * tm + t, 1)], sem.at[slot_])

    def start_tile(tile, slot_):
        def body(t, carry):
            for k in range(top_k):
                gather_copy(tile, t, k, slot_).start()
            return carry
        lax.fori_loop(0, tm, body, 0, unroll=8)

    @pl.when(i == 0)
    def _():
        start_tile(0, 0)

    @pl.when(i + 1 < n_tiles)
    def _():
        start_tile(i + 1, 1 - slot)

    pltpu.make_async_copy(rows_hbm.at[pl.ds(0, top_k * tm)], buf.at[slot], sem.at[slot]).wait()
    y = x_ref[...]
    for k in range(top_k):
        y = y + buf[slot, k * tm:(k + 1) * tm]
    if with_norm:
        xo_ref[...] = y
        h_ref[...] = (y * lax.rsqrt(jnp.mean(y * y, axis=-1, keepdims=True) + RMS_EPS)
                      * g_ref[...]).astype(BF16)
    else:
        @pl.when(i < split_tile)
        def _():
            lo_ref[...] = y

        @pl.when(i >= split_tile)
        def _():
            hi_ref[...] = y


def _moe_combine(x, rows, pos, next_gain, split):
    m, d = x.shape
    top_k = pos.shape[0] // m
    tm = LANE
    assert m % tm == 0
    with_norm = next_gain is not None
    x_spec = pl.BlockSpec((tm, d), lambda i, p: (i, 0))
    in_specs = [x_spec, pl.BlockSpec(memory_space=pl.ANY)]
    args = [x, rows]
    if with_norm:
        split_tile = 0
        in_specs.append(pl.BlockSpec((1, d), lambda i, p: (0, 0)))
        args.append(next_gain.reshape(1, d))
        out_specs = [x_spec, x_spec]
        out_shape = [jax.ShapeDtypeStruct((m, d), F32), jax.ShapeDtypeStruct((m, d), BF16)]
    else:
        assert split % tm == 0
        split_tile = split // tm
        out_specs = [pl.BlockSpec((tm, d), lambda i, p: (jnp.minimum(i, split_tile - 1), 0)),
                     pl.BlockSpec((tm, d), lambda i, p: (jnp.maximum(i - split_tile, 0), 0))]
        out_shape = [jax.ShapeDtypeStruct((split, d), F32), jax.ShapeDtypeStruct((m - split, d), F32)]
    return pl.pallas_call(
        functools.partial(_combine_kernel, top_k=top_k, with_norm=with_norm, split_tile=split_tile),
        grid_spec=pltpu.PrefetchScalarGridSpec(
            num_scalar_prefetch=1, grid=(m // tm,), in_specs=in_specs, out_specs=out_specs,
            scratch_shapes=[pltpu.VMEM((2, top_k * tm, d), F32), pltpu.SemaphoreType.DMA((2,))]),
        out_shape=out_shape, compiler_params=_cparams(1),
        name="moe_combine_norm" if with_norm else "moe_combine_split",
    )(pos, *args)


def _even_layer(x, h, i, groups, state_ret, state_rwkv, state_shift, wts):
    (a_w_in, a_w_out, ret_norm_g, rwkv_mu, rwkv_w0, rwkv_w2, rwkv_a0, rwkv_a2, rwkv_g2,
     rwkv_kk_scale, rwkv_ka, rwkv_rk, rwkv_lnx_g, rwkv_lnx_b) = [w[i] for w in wts]
    ret_heads, ret_dk = state_ret.shape[2], state_ret.shape[3]
    rw_heads, rw_n = state_rwkv.shape[2], state_rwkv.shape[3]
    ret_w = ret_heads * ret_dk
    rw_w = rw_heads * rw_n
    lora_w, lora_a, lora_g = rwkv_w2.shape[0], rwkv_a2.shape[0], rwkv_g2.shape[0]
    assert lora_w == LANE and lora_a == LANE and lora_g <= LANE and ret_w == rw_w
    n_main = 4 * ret_w + 3 * rw_w
    n_shift = 3 * rw_w + lora_w + lora_a + lora_g
    (pb, pt), (sb, st) = groups
    m_p, m_s = pb * pt, sb * st

    a_w_in_t = a_w_in.T
    p = _matmul([h], a_w_in_t, n_main, name="a_in", w_t=True)
    tail_cols = MM_TN
    assert n_main % tail_cols == 0 and n_main + tail_cols >= a_w_in.shape[1] and 3 * LANE <= tail_cols
    tail = _matmul([h], a_w_in_t, tail_cols, col_block0=n_main // tail_cols, name="a_in_tail",
                   w_t=True)

    o_ret_p, ret_p = _retention_prompt(p, ret_norm_g, pb, pt, ret_heads, ret_dk)
    o_ret_s, ret_s = _retention_sample(p, ret_norm_g, state_ret[i], m_p, sb, st, ret_heads, ret_dk)

    pad_g = lambda a: jnp.pad(a, ((0, LANE - a.shape[0]), (0, 0)))
    mu = rwkv_mu
    mu_tail = jnp.pad(mu[3 * rw_w:], (0, tail_cols - (n_shift - 3 * rw_w)))
    prep_params = [mu[:rw_w].reshape(1, -1), mu[rw_w:2 * rw_w].reshape(1, -1),
                   mu[2 * rw_w:3 * rw_w].reshape(1, -1), mu_tail.reshape(1, -1),
                   rwkv_w0.reshape(1, -1), rwkv_w2, rwkv_a0.reshape(1, -1), rwkv_a2,
                   pad_g(rwkv_g2)]
    cb0 = (4 * ret_w) // rw_w
    shift0 = state_shift[i]
    sh_main = jnp.repeat(shift0[:, :3 * rw_w], st, axis=0)
    sh_tail = jnp.repeat(jnp.pad(shift0[:, 3 * rw_w:], ((0, 0), (0, tail_cols - (n_shift - 3 * rw_w)))),
                         st, axis=0)
    seq_p = _rwkv_prep(p, tail, 0, m_p, cb0, rw_w, prep_params, None, pt, st)
    seq_s = _rwkv_prep(p, tail, m_p, m_s, cb0, rw_w, prep_params, (sh_main, sh_tail), pt, st)

    kparams = (rwkv_kk_scale, rwkv_ka, rwkv_rk, rwkv_lnx_g, rwkv_lnx_b)
    y_p, rwkv_p = _rwkv_scan_tiled(seq_p, kparams, pb, pt, rw_heads, rw_n)
    y_s, rwkv_s = _rwkv_scan(seq_s, kparams, state_rwkv[i], sb, st, rw_heads, rw_n)

    o_ret = jnp.concatenate([o_ret_p, o_ret_s], axis=0)
    y = jnp.concatenate([y_p, y_s], axis=0)
    x = _matmul([o_ret, y], a_w_out, a_w_out.shape[1], res=x, name="a_out")

    def last_ps(row0, nb, nt):
        def last(a, lo, hi):
            if nb * nt <= 1024:
                return a[row0:row0 + nb * nt, lo:hi].reshape(nb, nt, hi - lo)[:, nt - 1]
            return jnp.concatenate([a[row0 + (b + 1) * nt - 1:row0 + (b + 1) * nt, lo:hi]
                                    for b in range(nb)], axis=0)
        return jnp.concatenate([last(p, 4 * ret_w, n_main), last(tail, 0, n_shift - 3 * rw_w)],
                               axis=-1)

    return x, (ret_p, rwkv_p, last_ps(0, pb, pt)), (ret_s, rwkv_s, last_ps(m_p, sb, st))


def _odd_layer(x, h, i, groups, cache_k, cache_v, wts):
    c_w_in, c_w_out, c_q_norm_g, c_k_norm_g, c_sinks = [w[i] for w in wts]
    win, n_kv, hd = cache_k.shape[2], cache_k.shape[3], cache_k.shape[4]
    n_heads = c_sinks.shape[0]
    group = n_heads // n_kv
    mix_c = n_heads * hd
    kv_w = n_kv * hd
    (pb, pt), (sb, st) = groups
    m_p = pb * pt
    assert pt % win == 0 and mix_c % kv_w == 0
    p = _matmul([h], c_w_in, mix_c + 2 * kv_w, name="c_in")
    kcb, vcb = mix_c // kv_w, mix_c // kv_w + 1

    nb_p = m_p // win
    maps = (lambda n: (n, 0), lambda n: (n, kcb), lambda n: (n, vcb),
            lambda n: (jnp.maximum(n - 1, 0), kcb), lambda n: (jnp.maximum(n - 1, 0), vcb))
    o_p, kn_p = _swa(p, p, p, p, p, maps, nb_p, (win, mix_c), win, False, win, c_q_norm_g,
                     c_k_norm_g, c_sinks, n_kv, group, hd, True, False, pt // win)

    tq = 16
    ps = p[m_p:].reshape(sb, st, -1)
    q_s = jnp.pad(ps[:, :, :mix_c].reshape(sb, st, n_kv, group, hd),
                  ((0, 0), (0, tq - st), (0, 0), (0, 0), (0, 0)))
    q_s = q_s.transpose(0, 3, 1, 2, 4).reshape(sb * group * tq, kv_w)
    kv_pad = lambda a: jnp.pad(a, ((0, 0), (0, win - st), (0, 0))).reshape(sb * win, kv_w)
    k_s = kv_pad(ps[:, :, mix_c:mix_c + kv_w])
    v_s = kv_pad(ps[:, :, mix_c + kv_w:])
    kc0 = cache_k[i].reshape(sb * win, kv_w)
    vc0 = cache_v[i].reshape(sb * win, kv_w)
    same = lambda n: (n, 0)
    o_s, kn_s = _swa(q_s, k_s, v_s, kc0, vc0, (same,) * 5, sb, (group * tq, kv_w), tq, True, win,
                     c_q_norm_g, c_k_norm_g, c_sinks, n_kv, group, hd, False, True, 1)
    o_s = o_s.reshape(sb, group, tq, n_kv, hd)[:, :, :st].transpose(0, 2, 3, 1, 4).reshape(
        sb * st, mix_c)

    x = _matmul([jnp.concatenate([o_p, o_s], axis=0)], c_w_out, c_w_out.shape[1], res=x, name="c_out")

    new_k_p = kn_p.reshape(pb, pt, n_kv, hd)[:, -win:]
    new_v_p = p[:m_p, mix_c + kv_w:].reshape(pb, pt, n_kv, hd)[:, -win:]
    kn_s = kn_s.reshape(sb, win, n_kv, hd)[:, :st]
    new_k_s = jnp.concatenate([cache_k[i], kn_s], axis=1)[:, -win:]
    new_v_s = jnp.concatenate([cache_v[i], ps[:, :, mix_c + kv_w:].reshape(sb, st, n_kv, hd)],
                              axis=1)[:, -win:]
    return x, (new_k_p, new_v_p), (new_k_s, new_v_s)


def kernel(x_prompt, x_sample, state_ret, state_rwkv, state_rwkv_shift, cache_swa_k, cache_swa_v, norm_mix_g, norm_ffn_g, a_w_in, a_w_out, ret_norm_g, rwkv_mu, rwkv_w0, rwkv_w2, rwkv_a0, rwkv_a2, rwkv_g2, rwkv_kk_scale, rwkv_ka, rwkv_rk, rwkv_lnx_g, rwkv_lnx_b, c_w_in, c_w_out, c_q_norm_g, c_k_norm_g, c_sinks, moe_w_group, moe_b_group, moe_w_expert, moe_b_expert, moe_w_gate, moe_w_up, moe_w_down):
    pb, pt, d = x_prompt.shape
    sb, st, _ = x_sample.shape
    groups = ((pb, pt), (sb, st))
    depth = norm_mix_g.shape[0]
    a_wts = (a_w_in, a_w_out, ret_norm_g, rwkv_mu, rwkv_w0, rwkv_w2, rwkv_a0, rwkv_a2, rwkv_g2,
             rwkv_kk_scale, rwkv_ka, rwkv_rk, rwkv_lnx_g, rwkv_lnx_b)
    c_wts = (c_w_in, c_w_out, c_q_norm_g, c_k_norm_g, c_sinks)
    st_p = [[] for _ in range(5)]
    st_s = [[] for _ in range(5)]
    for l in range(depth):
        i = l // 2
        if l == 0:
            h, x = _rms_norm_first(x_prompt.reshape(pb * pt, d), x_sample.reshape(sb * st, d),
                                   norm_mix_g[l])
        else:
            h = h_next
        if l % 2 == 0:
            x, sp, ss = _even_layer(x, h, i, groups, state_ret, state_rwkv, state_rwkv_shift, a_wts)
            for j in range(3):
                st_p[j].append(sp[j])
                st_s[j].append(ss[j])
        else:
            x, sp, ss = _odd_layer(x, h, i, groups, cache_swa_k, cache_swa_v, c_wts)
            for j in range(2):
                st_p[3 + j].append(sp[j])
                st_s[3 + j].append(ss[j])
        n_router = moe_w_group.shape[2] + moe_w_expert.shape[2]
        w_router = jnp.pad(jnp.concatenate([moe_w_group[l], moe_w_expert[l]], axis=1),
                           ((0, 0), (0, LANE - n_router)))
        h, logits = _rms_norm(x, norm_ffn_g[l], w_router)
        moe_args = (x, h, logits, moe_b_group[l], moe_b_expert[l], moe_w_gate, moe_w_up, moe_w_down, l)
        if l < depth - 1:
            x, h_next = _moe(*moe_args, next_gain=norm_mix_g[l + 1])
        else:
            y_p, y_s = _moe(*moe_args, split=pb * pt)
    y_prompt = y_p.reshape(pb, pt, d)
    y_sample = y_s.reshape(sb, st, d)
    stack = lambda s: s[0][None] if len(s) == 1 else jnp.stack(s)
    return (y_prompt, y_sample) + tuple(stack(s) for s in st_p) + tuple(stack(s) for s in st_s)
```

```python
import functools

import jax
import jax.numpy as jnp
from jax import lax
from jax.experimental import pallas as pl
from jax.experimental.pallas import tpu as pltpu

F32 = jnp.float32
BF16 = jnp.bfloat16

RMS_EPS = 1e-6
RWKV_LN_EPS = 64e-5
RET_CHUNK = 128

LANE = 128
VMEM_LIMIT = 56 * 1024 * 1024

MM_TM = 1040
MM_TN = 512
NORM_TM = 320
MOE_TM = 256
SCAN_TC = 32
RET_HEADS_PER_STEP = 4


def _cparams(n_axes):
    return pltpu.CompilerParams(dimension_semantics=("arbitrary",) * n_axes,
                                vmem_limit_bytes=VMEM_LIMIT)


def _norm_router_kernel(x_ref, g_ref, wr_hi_ref, wr_lo_ref, h_ref, logit_ref):
    x = x_ref[...]
    y = x * lax.rsqrt(jnp.mean(x * x, axis=-1, keepdims=True) + RMS_EPS) * g_ref[...]
    y_hi = y.astype(BF16)
    h_ref[...] = y_hi
    y_lo = (y - y_hi.astype(F32)).astype(BF16)
    w_hi = wr_hi_ref[...]
    logit_ref[...] = (jnp.dot(y_hi, w_hi, preferred_element_type=F32)
                      + jnp.dot(y_hi, wr_lo_ref[...], preferred_element_type=F32)
                      + jnp.dot(y_lo, w_hi, preferred_element_type=F32))


def _norm_first_kernel(xp_ref, xs_ref, g_ref, h_ref, x_ref, *, n_prompt_tiles):
    x = jnp.where(pl.program_id(0) < n_prompt_tiles, xp_ref[...], xs_ref[...])
    x_ref[...] = x
    y = x * lax.rsqrt(jnp.mean(x * x, axis=-1, keepdims=True) + RMS_EPS) * g_ref[...]
    h_ref[...] = y.astype(BF16)


def _rms_norm_first(xp, xs, g):
    (m_p, d), m_s = xp.shape, xs.shape[0]
    tm = LANE
    assert m_p % tm == 0 and m_s % tm == 0
    npt = m_p // tm
    spec = pl.BlockSpec((tm, d), lambda i: (i, 0))
    return pl.pallas_call(
        functools.partial(_norm_first_kernel, n_prompt_tiles=npt), grid=((m_p + m_s) // tm,),
        in_specs=[pl.BlockSpec((tm, d), lambda i: (jnp.minimum(i, npt - 1), 0)),
                  pl.BlockSpec((tm, d), lambda i: (jnp.maximum(i - npt, 0), 0)),
                  pl.BlockSpec((1, d), lambda i: (0, 0))],
        out_specs=[spec, spec],
        out_shape=[jax.ShapeDtypeStruct((m_p + m_s, d), BF16),
                   jax.ShapeDtypeStruct((m_p + m_s, d), F32)],
        compiler_params=_cparams(1), name="rms_norm_first")(xp, xs, g.reshape(1, d))


def _rms_norm_router(x, g, w_router):
    m, d = x.shape
    tm = NORM_TM if m % NORM_TM == 0 else m
    x_spec = pl.BlockSpec((tm, d), lambda i: (i, 0))
    nr = w_router.shape[1]
    w_spec = pl.BlockSpec((d, nr), lambda i: (0, 0))
    w_hi = w_router.astype(BF16)
    w_lo = (w_router - w_hi.astype(F32)).astype(BF16)
    return pl.pallas_call(
        _norm_router_kernel, grid=(m // tm,),
        in_specs=[x_spec, pl.BlockSpec((1, d), lambda i: (0, 0)), w_spec, w_spec],
        out_specs=[x_spec, pl.BlockSpec((tm, nr), lambda i: (i, 0))],
        out_shape=[jax.ShapeDtypeStruct((m, d), BF16), jax.ShapeDtypeStruct((m, nr), F32)],
        compiler_params=_cparams(1), name="rms_norm_router")(x, g.reshape(1, d), w_hi, w_lo)


def _mm_kernel(*refs, n_parts, has_res, cols_valid, w_t):
    x_refs = refs[:n_parts]
    w_refs = refs[n_parts:2 * n_parts]
    pos = 2 * n_parts
    res_ref = refs[pos] if has_res else None
    pos += int(has_res)
    o_ref = refs[pos]
    wb_refs = refs[pos + 1:]
    n_axis = 0 if w_t else 1

    @pl.when(pl.program_id(1) == 0)
    def _():
        for w_ref, wb_ref in zip(w_refs, wb_refs):
            w = w_ref[...]
            if cols_valid is not None:
                col = (lax.broadcasted_iota(jnp.int32, w.shape, n_axis)
                       + pl.program_id(0) * w.shape[n_axis])
                w = jnp.where(col < cols_valid, w, 0.0)
            wb_ref[...] = w.astype(BF16)

    acc = None
    for x_ref, wb_ref in zip(x_refs, wb_refs):
        d = lax.dot_general(x_ref[...], wb_ref[...], (((1,), (1 if w_t else 0,)), ((), ())),
                            preferred_element_type=F32)
        acc = d if acc is None else acc + d
    if has_res:
        acc = acc + res_ref[...]
    o_ref[...] = acc


def _matmul(xs, w, n_out, col_block0=0, tn=MM_TN, res=None, name="proj", w_t=False):
    k_total, n_total = (w.shape[1], w.shape[0]) if w_t else w.shape
    over = (col_block0 * tn + n_out) > n_total
    cols_valid = n_total - col_block0 * tn if over else None
    m = xs[0].shape[0]
    tm = MM_TM if m % MM_TM == 0 else m
    assert n_out % tn == 0
    grid = (n_out // tn, m // tm)
    in_specs, w_specs, scratch = [], [], []
    row = 0
    for x in xs:
        kp = x.shape[1]
        in_specs.append(pl.BlockSpec((tm, kp), lambda j, i: (i, 0)))
        assert row % kp == 0
        rb = row // kp
        if w_t:
            w_specs.append(pl.BlockSpec((tn, kp), lambda j, i, rb=rb: (j + col_block0, rb)))
            scratch.append(pltpu.VMEM((tn, kp), BF16))
        else:
            w_specs.append(pl.BlockSpec((kp, tn), lambda j, i, rb=rb: (rb, j + col_block0)))
            scratch.append(pltpu.VMEM((kp, tn), BF16))
        row += kp
    assert row == k_total
    args = list(xs) + [w] * len(xs)
    in_specs = in_specs + w_specs
    if res is not None:
        in_specs.append(pl.BlockSpec((tm, tn), lambda j, i: (i, j)))
        args.append(res)
    return pl.pallas_call(
        functools.partial(_mm_kernel, n_parts=len(xs), has_res=res is not None,
                          cols_valid=cols_valid, w_t=w_t),
        grid=grid, in_specs=in_specs,
        out_specs=pl.BlockSpec((tm, tn), lambda j, i: (i, j)),
        out_shape=jax.ShapeDtypeStruct((m, n_out), F32),
        scratch_shapes=scratch, compiler_params=_cparams(2), name=name)(*args)


def _ret_finish(o, g, gain):
    o = o * lax.rsqrt(jnp.mean(o * o, axis=-1, keepdims=True) + RMS_EPS) * gain
    return ((g / (1.0 + jnp.exp(-g))) * o).astype(BF16)


def _ret_prompt_kernel(q_ref, k_ref, v_ref, g_ref, gain_ref, lg_ref, o_ref, s_ref, *, dk):
    c = pl.program_id(2)
    chunk = q_ref.shape[0]

    @pl.when(c == 0)
    def _():
        s_ref[...] = jnp.zeros_like(s_ref)

    row = lax.broadcasted_iota(jnp.int32, (chunk, chunk), 0)
    col = lax.broadcasted_iota(jnp.int32, (chunk, chunk), 1)
    diff = (row - col).astype(F32)
    causal = diff >= 0
    dist = jnp.where(causal, diff, 0.0)
    pos = lax.broadcasted_iota(jnp.int32, (chunk, 1), 0).astype(F32)

    for hh in range(s_ref.shape[0]):
        sl = slice(hh * dk, (hh + 1) * dk)
        lg = lg_ref[hh][:, :1]
        decay_mask = jnp.where(causal, jnp.exp(dist * lg), 0.0)
        q_decay = jnp.exp((pos + 1.0) * lg)
        k_decay = jnp.exp((chunk - 1.0 - pos) * lg)
        chunk_decay = jnp.exp(chunk * lg)
        q = q_ref[:, sl]
        k = k_ref[:, sl] * (dk ** -0.5)
        vb = v_ref[:, sl].astype(BF16)
        s = s_ref[hh]
        scores = lax.dot_general(q.astype(BF16), k.astype(BF16), (((1,), (1,)), ((), ())),
                                 preferred_element_type=F32) * decay_mask
        inner = jnp.dot(scores.astype(BF16), vb, preferred_element_type=F32)
        cross = jnp.dot((q * q_decay).astype(BF16), s.astype(BF16), preferred_element_type=F32)
        s_ref[hh] = chunk_decay * s + lax.dot_general(
            (k * k_decay).astype(BF16), vb, (((0,), (0,)), ((), ())), preferred_element_type=F32)
        o_ref[:, sl] = _ret_finish(inner + cross, g_ref[:, sl], gain_ref[hh])


def _ret_sample_kernel(q_ref, k_ref, v_ref, g_ref, gain_ref, lg_ref, s0_ref, o_ref, s_ref, acc_ref,
                       *, dk, t_len):
    b = pl.program_id(1)
    rows = q_ref.shape[0]
    r_b = lax.broadcasted_iota(jnp.int32, (rows, 1), 0) // t_len
    r_t = (lax.broadcasted_iota(jnp.int32, (rows, 1), 0) % t_len).astype(F32)
    sel = r_b == b

    for hh in range(s_ref.shape[0]):
        sl = slice(hh * dk, (hh + 1) * dk)
        lg = lg_ref[hh][:, :1]
        q = q_ref[:, sl]
        k = k_ref[:, sl] * (dk ** -0.5)
        vb = v_ref[:, sl].astype(BF16)

        @pl.when(b == 0)
        def _(hh=hh, lg=lg, q=q, k=k, vb=vb):
            row = lax.broadcasted_iota(jnp.int32, (rows, rows), 0)
            col = lax.broadcasted_iota(jnp.int32, (rows, rows), 1)
            diff = (row % t_len - col % t_len).astype(F32)
            ok = (row // t_len == col // t_len) & (diff >= 0)
            mask = jnp.where(ok, jnp.exp(jnp.where(ok, diff, 0.0) * lg), 0.0)
            scores = lax.dot_general(q.astype(BF16), k.astype(BF16), (((1,), (1,)), ((), ())),
                                     preferred_element_type=F32) * mask
            acc_ref[hh] = jnp.dot(scores.astype(BF16), vb, preferred_element_type=F32)

        s0 = s0_ref[hh]
        cross = jnp.dot((q * jnp.exp((r_t + 1.0) * lg)).astype(BF16), s0.astype(BF16),
                        preferred_element_type=F32)
        acc_ref[hh] += jnp.where(sel, cross, 0.0)
        k_decay = jnp.where(sel, jnp.exp((t_len - 1.0 - r_t) * lg), 0.0)
        s_ref[hh] = jnp.exp(t_len * lg) * s0 + lax.dot_general(
            (k * k_decay).astype(BF16), vb, (((0,), (0,)), ((), ())), preferred_element_type=F32)

        @pl.when(b == pl.num_programs(1) - 1)
        def _(hh=hh, sl=sl):
            o_ref[:, sl] = _ret_finish(acc_ref[hh], g_ref[:, sl], gain_ref[hh])


def _ret_log_decay(n_heads):
    lg = jnp.log(1.0 - 2.0 ** (-5.0 - jnp.arange(n_heads, dtype=F32)))
    return jnp.broadcast_to(lg[:, None, None], (n_heads, 1, LANE))


def _retention_prompt(p, gain, n_batch, seq, n_heads, dk):
    nc = seq // RET_CHUNK
    hps = RET_HEADS_PER_STEP
    assert n_heads % hps == 0
    ng = n_heads // hps
    col = lambda grp: (lambda b, h, c: (b * nc + c, grp * ng + h))
    blk = lambda grp: pl.BlockSpec((RET_CHUNK, hps * dk), col(grp))
    return pl.pallas_call(
        functools.partial(_ret_prompt_kernel, dk=dk),
        grid=(n_batch, ng, nc),
        in_specs=[blk(0), blk(1), blk(2), blk(3),
                  pl.BlockSpec((hps, 1, dk), lambda b, h, c: (h, 0, 0)),
                  pl.BlockSpec((hps, 1, LANE), lambda b, h, c: (h, 0, 0))],
        out_specs=[pl.BlockSpec((RET_CHUNK, hps * dk), lambda b, h, c: (b * nc + c, h)),
                   pl.BlockSpec((None, hps, dk, dk), lambda b, h, c: (b, h, 0, 0))],
        out_shape=[jax.ShapeDtypeStruct((n_batch * seq, n_heads * dk), BF16),
                   jax.ShapeDtypeStruct((n_batch, n_heads, dk, dk), F32)],
        compiler_params=_cparams(3), name="retention_prompt",
    )(p, p, p, p, gain.reshape(n_heads, 1, dk), _ret_log_decay(n_heads))


def _retention_sample(p, gain, s0, row0, n_batch, t_len, n_heads, dk):
    rows = n_batch * t_len
    rb = row0 // rows
    hps = RET_HEADS_PER_STEP
    assert n_heads % hps == 0
    ng = n_heads // hps
    blk = lambda grp: pl.BlockSpec((rows, hps * dk), lambda h, b, grp=grp: (rb, grp * ng + h))
    return pl.pallas_call(
        functools.partial(_ret_sample_kernel, dk=dk, t_len=t_len),
        grid=(ng, n_batch),
        in_specs=[blk(0), blk(1), blk(2), blk(3),
                  pl.BlockSpec((hps, 1, dk), lambda h, b: (h, 0, 0)),
                  pl.BlockSpec((hps, 1, LANE), lambda h, b: (h, 0, 0)),
                  pl.BlockSpec((None, hps, dk, dk), lambda h, b: (b, h, 0, 0))],
        out_specs=[pl.BlockSpec((rows, hps * dk), lambda h, b: (0, h)),
                   pl.BlockSpec((None, hps, dk, dk), lambda h, b: (b, h, 0, 0))],
        out_shape=[jax.ShapeDtypeStruct((rows, n_heads * dk), BF16),
                   jax.ShapeDtypeStruct((n_batch, n_heads, dk, dk), F32)],
        scratch_shapes=[pltpu.VMEM((hps, rows, dk), F32)],
        compiler_params=_cparams(2), name="retention_sample",
    )(p, p, p, p, gain.reshape(n_heads, 1, dk), _ret_log_decay(n_heads), s0)


def _sigmoid(x):
    return 1.0 / (1.0 + jnp.exp(-x))


def _rwkv_prep_kernel(*refs, sample, seq, t_len):
    if sample:
        (r_c, k_c, v_c, t_c, r_s, k_s, v_s, t_s,
         mu_r, mu_k, mu_v, mu_t, w0, w2, a0, a2, g2,
         r_o, w_o, k_o, v_o, a_o, g_o) = refs
    else:
        (r_c, k_c, v_c, t_c, r_p, k_p, v_p, t_p,
         mu_r, mu_k, mu_v, mu_t, w0, w2, a0, a2, g2,
         r_o, w_o, k_o, v_o, a_o, g_o) = refs
    tm = r_c.shape[0]
    row = lax.broadcasted_iota(jnp.int32, (tm, 1), 0)
    seq_start = (pl.program_id(0) * tm) % seq == 0

    def mix(cur_ref, other_ref, mu_ref):
        cur = cur_ref[...]
        rolled = pltpu.roll(cur, 1, 0)
        if sample:
            prev = jnp.where(row % t_len == 0, other_ref[...], rolled)
        else:
            last = other_ref[7:8, :]
            first = jnp.where(seq_start, jnp.zeros_like(last), last)
            prev = jnp.where(row == 0, first, rolled)
        return cur + (prev - cur) * mu_ref[...]

    def store(o_ref, val):
        if len(o_ref.shape) == 2:
            o_ref[...] = val
        else:
            for c in range(o_ref.shape[1]):
                o_ref[:, c] = val[:, c * LANE:(c + 1) * LANE].reshape(o_ref.shape[0], 8, LANE)

    other = (r_s, k_s, v_s, t_s) if sample else (r_p, k_p, v_p, t_p)
    store(r_o, mix(r_c, other[0], mu_r))
    store(k_o, mix(k_c, other[1], mu_k))
    store(v_o, mix(v_c, other[2], mu_v))
    tail = mix(t_c, other[3], mu_t)
    lw, la, lgt = tail[:, :LANE], tail[:, LANE:2 * LANE], tail[:, 2 * LANE:3 * LANE]

    z = -(w0[...] + jnp.dot(jnp.tanh(lw).astype(BF16), w2[...].astype(BF16),
                            preferred_element_type=F32))
    softplus = jnp.maximum(z, 0.0) + jnp.log1p(jnp.exp(-jnp.abs(z)))
    store(w_o, jnp.exp(-jnp.exp(-softplus - 0.5)))
    store(a_o, _sigmoid(a0[...] + jnp.dot(la.astype(BF16), a2[...].astype(BF16),
                                          preferred_element_type=F32)))
    store(g_o, jnp.dot(_sigmoid(lgt).astype(BF16), g2[...].astype(BF16),
                       preferred_element_type=F32))


def _rwkv_prep(p, tail, row0, n_rows, col_block0, width, params, shift_rows, seq, t_len):
    sample = shift_rows is not None
    tm = n_rows if sample else 256
    rb0 = row0 // tm
    grid = (n_rows // tm,)
    cur = lambda grp: pl.BlockSpec((tm, width), lambda i, grp=grp: (rb0 + i, col_block0 + grp))
    tail_w = tail.shape[1]
    in_specs = [cur(0), cur(1), cur(2), pl.BlockSpec((tm, tail_w), lambda i: (rb0 + i, 0))]
    args = [p, p, p, tail]
    if sample:
        sh_main, sh_tail = shift_rows
        in_specs += [pl.BlockSpec((tm, width), lambda i, grp=grp: (0, grp)) for grp in range(3)]
        in_specs += [pl.BlockSpec((tm, tail_w), lambda i: (0, 0))]
        args += [sh_main, sh_main, sh_main, sh_tail]
    else:
        per = tm // 8
        prev_rb = lambda i: jnp.maximum((rb0 + i) * per - 1, 0)
        in_specs += [pl.BlockSpec((8, width), lambda i, grp=grp: (prev_rb(i), col_block0 + grp))
                     for grp in range(3)]
        in_specs += [pl.BlockSpec((8, tail_w), lambda i: (prev_rb(i), 0))]
        args += [p, p, p, tail]
    full = lambda a: pl.BlockSpec(a.shape, lambda i: (0,) * a.ndim)
    in_specs += [full(a) for a in params]
    args += list(params)
    if sample:
        out_spec = pl.BlockSpec((tm, width), lambda i: (i, 0))
        out_shape = jax.ShapeDtypeStruct((n_rows, width), F32)
    else:
        out_spec = pl.BlockSpec((tm // 8, width // LANE, 8, LANE), lambda i: (i, 0, 0, 0))
        out_shape = jax.ShapeDtypeStruct((n_rows // 8, width // LANE, 8, LANE), F32)
    return pl.pallas_call(
        functools.partial(_rwkv_prep_kernel, sample=sample, seq=seq, t_len=t_len),
        grid=grid, in_specs=in_specs, out_specs=[out_spec] * 6, out_shape=[out_shape] * 6,
        compiler_params=_cparams(1), name="rwkv_prep_sample" if sample else "rwkv_prep_prompt",
    )(*args)


def _rwkv_step(s_ref, ys_ref, r, w, kr, v, a, g, par_refs, between=None):
    kks_ref, ka_ref, rk_ref, lng_ref, lnb_ref = par_refs
    kks = kr * kks_ref[...]
    kk = kks / jnp.maximum(jnp.sqrt(jnp.sum(kks * kks, axis=0, keepdims=True)), 1e-12)
    kf = kr * (1.0 + (a - 1.0) * ka_ref[...])
    kka = kk * a
    for vi in range(s_ref.shape[0]):
        s = s_ref[vi]
        sa = jnp.sum(s * kk, axis=0, keepdims=True)
        s = s * w - sa * kka + v[vi:vi + 1, :] * kf
        s_ref[vi] = s
        ys_ref[vi:vi + 1, :] = jnp.sum(s * r, axis=0, keepdims=True)
        if between is not None:
            between(vi)
    y = ys_ref[...]
    mu = jnp.mean(y, axis=0, keepdims=True)
    var = jnp.mean(jnp.square(y - mu), axis=0, keepdims=True)
    y = (y - mu) * lax.rsqrt(var + RWKV_LN_EPS) * lng_ref[...] + lnb_ref[...]
    bonus = jnp.sum(r * kf * rk_ref[...], axis=0, keepdims=True) * v
    return (y + bonus) * g


def _rwkv_scan_kernel(r_ref, w_ref, k_ref, v_ref, a_ref, g_ref,
                      kks_ref, ka_ref, rk_ref, lng_ref, lnb_ref, s0_ref,
                      y_ref, s_ref, ys_ref):
    par_refs = (kks_ref, ka_ref, rk_ref, lng_ref, lnb_ref)

    @pl.when(pl.program_id(1) == 0)
    def _():
        s_ref[...] = s0_ref[...]

    def step(t, carry):
        y = _rwkv_step(s_ref, ys_ref, r_ref[t], w_ref[t], k_ref[t], v_ref[t], a_ref[t], g_ref[t],
                       par_refs)
        y_ref[t] = y.astype(y_ref.dtype)
        return carry

    lax.fori_loop(0, r_ref.shape[0], step, 0)


def _rwkv_scan_tiled_kernel(r_ref, w_ref, k_ref, v_ref, a_ref, g_ref,
                            kks_ref, ka_ref, rk_ref, lng_ref, lnb_ref, s0_ref,
                            y_ref, s_ref, ys_ref, nxt_ref):
    seq_refs = (r_ref, w_ref, k_ref, v_ref, a_ref, g_ref)
    par_refs = (kks_ref, ka_ref, rk_ref, lng_ref, lnb_ref)
    n_grp = r_ref.shape[1]
    n = s_ref.shape[0]
    every = n // (len(seq_refs) + 1)

    @pl.when(pl.program_id(0) == 0)
    def _():
        s_ref[...] = s0_ref[...]

    def fetch(i, tg, t8, slot):
        zt = seq_refs[i][:, tg, :, t8, :].reshape(-1, LANE).T
        nxt_ref[slot, i] = jnp.concatenate([zt[:n], zt[n:]], axis=1)

    for i in range(len(seq_refs)):
        fetch(i, 0, 0, 0)

    def group(tg, carry):
        for t8 in range(8):
            slot = t8 % 2
            nxt_tg = tg if t8 < 7 else jnp.minimum(tg + 1, n_grp - 1)

            def between(vi, slot=slot, nxt_tg=nxt_tg, t8=t8):
                if vi % every == every - 1 and vi // every < len(seq_refs):
                    fetch(vi // every, nxt_tg, (t8 + 1) % 8, 1 - slot)

            y = _rwkv_step(s_ref, ys_ref, *[nxt_ref[slot, i] for i in range(len(seq_refs))],
                           par_refs, between)
            y_ref[tg * 8 + t8] = y.astype(y_ref.dtype)
        return carry

    lax.fori_loop(0, n_grp, group, 0)


def _rwkv_scan(seqs, kparams, s0, n_batch, t_len, n_heads, n):
    pairs = n_batch * n_heads
    tc = SCAN_TC if t_len % SCAN_TC == 0 else t_len
    grid = (pairs // LANE, t_len // tc)
    seq_spec = pl.BlockSpec((tc, n, LANE), lambda g, t: (t, 0, g))
    par_spec = pl.BlockSpec((n, LANE), lambda g, t: (0, g))
    st_spec = pl.BlockSpec((n, n, LANE), lambda g, t: (0, 0, g))
    pairs_major = lambda a: a.reshape(n_batch, t_len, n_heads, n).transpose(1, 3, 0, 2).reshape(
        t_len, n, pairs)
    params = [jnp.tile(a.reshape(n_heads, n).T, (1, n_batch)) for a in kparams]
    y, s_t = pl.pallas_call(
        _rwkv_scan_kernel, grid=grid,
        in_specs=[seq_spec] * 6 + [par_spec] * 5 + [st_spec],
        out_specs=[seq_spec, st_spec],
        out_shape=[jax.ShapeDtypeStruct((t_len, n, pairs), BF16),
                   jax.ShapeDtypeStruct((n, n, pairs), F32)],
        scratch_shapes=[pltpu.VMEM((n, LANE), F32)],
        compiler_params=_cparams(2), name="rwkv_scan",
    )(*[pairs_major(a) for a in seqs], *params, s0.transpose(2, 3, 0, 1).reshape(n, n, pairs))
    y = y.reshape(t_len, n, n_batch, n_heads).transpose(2, 0, 3, 1)
    s_t = s_t.reshape(n, n, n_batch, n_heads).transpose(2, 3, 0, 1)
    return y.reshape(n_batch * t_len, n_heads * n), s_t


def _rwkv_scan_tiled(seqs, kparams, n_batch, t_len, n_heads, n):
    chunks = n_heads * n // LANE
    assert n_batch * n_heads == LANE and 2 * n == LANE and t_len % SCAN_TC == 0 and SCAN_TC % 16 == 0
    tc = SCAN_TC
    seq_spec = pl.BlockSpec((n_batch, tc // 8, chunks, 8, LANE), lambda t: (0, t, 0, 0, 0))
    par_spec = pl.BlockSpec((n, LANE), lambda t: (0, 0))
    st_spec = pl.BlockSpec((n, n, LANE), lambda t: (0, 0, 0))
    params = [jnp.broadcast_to(a.reshape(chunks, 2, n).transpose(2, 1, 0)[:, :, None, :],
                               (n, 2, n_batch, chunks)).reshape(n, LANE) for a in kparams]
    y, s_t = pl.pallas_call(
        _rwkv_scan_tiled_kernel, grid=(t_len // tc,),
        in_specs=[seq_spec] * 6 + [par_spec] * 5 + [st_spec],
        out_specs=[pl.BlockSpec((tc, n, LANE), lambda t: (t, 0, 0)), st_spec],
        out_shape=[jax.ShapeDtypeStruct((t_len, n, LANE), BF16),
                   jax.ShapeDtypeStruct((n, n, LANE), F32)],
        scratch_shapes=[pltpu.VMEM((n, LANE), F32), pltpu.VMEM((2, 6, n, LANE), F32)],
        compiler_params=_cparams(1), name="rwkv_scan_tiled",
    )(*[a.reshape(n_batch, t_len // 8, chunks, 8, LANE) for a in seqs], *params,
      jnp.zeros((n, n, LANE), F32))
    y = y.reshape(t_len, n, 2, n_batch, chunks).transpose(0, 1, 3, 4, 2)
    y = y.reshape(t_len, n, n_batch, n_heads).transpose(2, 0, 3, 1)
    s_t = s_t.reshape(n, n, 2, n_batch, chunks).transpose(3, 4, 2, 0, 1)
    return y.reshape(n_batch * t_len, n_heads * n), s_t.reshape(n_batch, n_heads, n, n)


def _swa_kernel(sink_ref, q_ref, kc_ref, vc_ref, kp_ref, vp_ref, qg_ref, kg_ref,
                o_ref, kn_ref, s_scr, e_scr, inv_scr, *, n_kv, group, hd, rows_per_head, stacked,
                norm_prev, first_has_prev, blocks_per_seq):
    rph = rows_per_head
    win = kc_ref.shape[0]
    n_heads = n_kv * group
    parts = 1 if stacked else 2
    has_prev = jnp.logical_or(first_has_prev, pl.program_id(0) % blocks_per_seq > 0)
    nt = (((1,), (1,)), ((), ()))

    def rms(x, g):
        return x * lax.rsqrt(jnp.mean(x * x, axis=-1, keepdims=True) + RMS_EPS) * g

    qpos = lax.broadcasted_iota(jnp.int32, (rph, 2 * win), 0)
    kpos = lax.broadcasted_iota(jnp.int32, (rph, 2 * win), 1) - win
    diff = qpos - kpos
    valid = (diff >= 0) & (diff < win) & (kpos >= jnp.where(has_prev, -win, 0))
    neg_dist = jnp.where(valid, -(diff.astype(F32)), -jnp.inf)
    qg = qg_ref[...]
    kg = kg_ref[...]
    qg2 = jnp.concatenate([qg, qg], axis=1)
    left = lax.broadcasted_iota(jnp.int32, (rph, 2 * hd), 1) < hd

    def keys_values(j):
        sl = slice(j * hd, (j + 1) * hd)
        kc = rms(kc_ref[:, sl], kg)
        kp = kp_ref[:, sl]
        if norm_prev:
            kp = rms(kp, kg)
        kb = jnp.concatenate([kp, kc], axis=0)
        vb = jnp.concatenate([vp_ref[:, sl], vc_ref[:, sl]], axis=0)
        return kc, kb, vb

    def halves(x):
        z = jnp.zeros_like(x)
        return (jnp.concatenate([x, z], axis=1).astype(BF16),
                jnp.concatenate([z, x], axis=1).astype(BF16))

    for j in range(n_kv):
        kc, kb, _ = keys_values(j)
        kn_ref[:, j * hd:(j + 1) * hd] = kc
        if stacked:
            q = rms(q_ref[:, j * hd:(j + 1) * hd], qg).astype(BF16)
            s_scr[j, 0] = lax.dot_general(q, kb.astype(BF16), nt, preferred_element_type=F32)
        else:
            qs = []
            for pp in range(group // 2):
                lo = (j * group + 2 * pp) * hd
                qp = q_ref[:, lo:lo + 2 * hd]
                sq = qp * qp
                tot = jnp.sum(sq, axis=-1, keepdims=True)
                lsum = jnp.sum(jnp.where(left, sq, 0.0), axis=-1, keepdims=True)
                r_l = lax.rsqrt(lsum / hd + RMS_EPS)
                r_r = lax.rsqrt((tot - lsum) / hd + RMS_EPS)
                qs.append((qp * jnp.where(left, r_l, r_r) * qg2).astype(BF16))
            q = jnp.concatenate(qs, axis=0)
            kb0, kb1 = halves(kb)
            s_scr[j, 0] = lax.dot_general(q, kb0, nt, preferred_element_type=F32)
            s_scr[j, 1] = lax.dot_general(q, kb1, nt, preferred_element_type=F32)

    for j in range(n_kv):
        for blk in range(group // parts):
            inv = []
            for part in range(parts):
                h = j * group + blk * parts + part
                rows = slice(blk * rph, (blk + 1) * rph)
                slope = 2.0 ** (-8.0 * (h + 1) / n_heads)
                sh = s_scr[j, part, rows, :] * (hd ** -0.5) + slope * neg_dist
                sink = sink_ref[h]
                m = jnp.maximum(jnp.max(sh, axis=-1, keepdims=True), sink)
                e = jnp.exp(sh - m)
                inv.append(1.0 / (jnp.sum(e, axis=-1, keepdims=True) + jnp.exp(sink - m)))
                e_scr[j, part, rows, :] = e.astype(BF16)
            if stacked:
                inv_scr[j, rows, :] = jnp.broadcast_to(inv[0], (rph, 2 * hd))
            else:
                inv_scr[j, rows, :] = jnp.where(left, inv[0], inv[1])

    for j in range(n_kv):
        _, _, vb = keys_values(j)
        if stacked:
            o = jnp.dot(e_scr[j, 0], vb.astype(BF16), preferred_element_type=F32)
            o_ref[:, j * hd:(j + 1) * hd] = (o * inv_scr[j][:, :hd]).astype(BF16)
        else:
            vb0, vb1 = halves(vb)
            o = (jnp.dot(e_scr[j, 0], vb0, preferred_element_type=F32)
                 + jnp.dot(e_scr[j, 1], vb1, preferred_element_type=F32))
            o = (o * inv_scr[j]).astype(BF16)
            for pp in range(group // 2):
                lo = (j * group + 2 * pp) * hd
                o_ref[:, lo:lo + 2 * hd] = o[pp * rph:(pp + 1) * rph]


def _swa(q_arr, kc_arr, vc_arr, kp_arr, vp_arr, maps, n_blocks, q_block, rows_per_head, stacked,
         win, qg, kg, sinks, n_kv, group, hd, norm_prev, first_has_prev, blocks_per_seq):
    q_map, kc_map, vc_map, kp_map, vp_map = maps
    parts = 1 if stacked else 2
    rows = group // parts * rows_per_head
    assert 2 * hd == LANE and group % 2 == 0
    return pl.pallas_call(
        functools.partial(_swa_kernel, n_kv=n_kv, group=group, hd=hd, rows_per_head=rows_per_head,
                          stacked=stacked, norm_prev=norm_prev, first_has_prev=first_has_prev,
                          blocks_per_seq=blocks_per_seq),
        grid=(n_blocks,),
        in_specs=[pl.BlockSpec(memory_space=pltpu.SMEM),
                  pl.BlockSpec(q_block, q_map),
                  pl.BlockSpec((win, n_kv * hd), kc_map),
                  pl.BlockSpec((win, n_kv * hd), vc_map),
                  pl.BlockSpec((win, n_kv * hd), kp_map),
                  pl.BlockSpec((win, n_kv * hd), vp_map),
                  pl.BlockSpec((1, hd), lambda i: (0, 0)),
                  pl.BlockSpec((1, hd), lambda i: (0, 0))],
        out_specs=[pl.BlockSpec(q_block, lambda i: (i, 0)),
                   pl.BlockSpec((win, n_kv * hd), lambda i: (i, 0))],
        out_shape=[jax.ShapeDtypeStruct((n_blocks * q_block[0], q_block[1]), BF16),
                   jax.ShapeDtypeStruct((n_blocks * win, n_kv * hd), F32)],
        scratch_shapes=[pltpu.VMEM((n_kv, parts, rows, 2 * win), F32),
                        pltpu.VMEM((n_kv, parts, rows, 2 * win), BF16),
                        pltpu.VMEM((n_kv, rows, LANE), F32)],
        compiler_params=_cparams(1), name="swa_stacked%d" % int(stacked),
    )(sinks, q_arr, kc_arr, vc_arr, kp_arr, vp_arr, qg.reshape(1, hd), kg.reshape(1, hd))


def _moe_up_kernel(te_ref, nu_ref, xs_ref, wg_ref, wu_ref, rw_ref, hid_ref, wgb_ref, wub_ref):
    i = pl.program_id(0)
    changed = jnp.logical_or(i == 0, te_ref[i] != te_ref[jnp.maximum(i - 1, 0)])

    @pl.when(changed)
    def _():
        wgb_ref[...] = wg_ref[...].astype(BF16)
        wub_ref[...] = wu_ref[...].astype(BF16)

    @pl.when(i < nu_ref[0])
    def _():
        x = xs_ref[...]
        g = jnp.dot(x, wgb_ref[...], preferred_element_type=F32)
        u = jnp.dot(x, wub_ref[...], preferred_element_type=F32)
        hid_ref[...] = ((g / (1.0 + jnp.exp(-g))) * u * rw_ref[...]).astype(BF16)

    @pl.when(i >= nu_ref[0])
    def _():
        hid_ref[...] = jnp.zeros_like(hid_ref)


def _moe_down_kernel(te_ref, nu_ref, hid_ref, wd_ref, out_ref, wdb_ref):
    i = pl.program_id(0)
    changed = jnp.logical_or(i == 0, te_ref[i] != te_ref[jnp.maximum(i - 1, 0)])

    @pl.when(changed)
    def _():
        wdb_ref[...] = wd_ref[...].astype(BF16)

    @pl.when(i < nu_ref[0])
    def _():
        out_ref[...] = jnp.dot(hid_ref[...], wdb_ref[...], preferred_element_type=F32)

    @pl.when(i >= nu_ref[0])
    def _():
        out_ref[...] = jnp.zeros_like(out_ref)


def _moe(x, h, logits, b_group, b_expert, w_gate, w_up, w_down, layer, next_gain=None, split=None):
    m, d = h.shape
    n_groups = b_group.shape[0]
    n_experts = b_expert.shape[0]
    per_group = n_experts // n_groups
    f = w_gate.shape[2] // n_experts
    top_k = 2

    g_logit = logits[:, :n_groups] + b_group
    g_prob = jax.nn.softmax(g_logit, axis=-1)
    g_idx = jnp.argmax(g_logit, axis=-1)
    g_gate = jnp.take_along_axis(g_prob, g_idx[:, None], axis=-1)
    e_logit = (logits[:, n_groups:n_groups + n_experts] + b_expert).reshape(m, n_groups, per_group)
    e_in = jnp.take_along_axis(e_logit, g_idx[:, None, None], axis=1)[:, 0]
    i1 = jnp.argmax(e_in, axis=-1)
    rest = jnp.where(jnp.arange(per_group)[None, :] == i1[:, None], -jnp.inf, e_in)
    i2 = jnp.argmax(rest, axis=-1)
    top_v = jnp.stack([jnp.max(e_in, axis=-1), jnp.max(rest, axis=-1)], axis=-1)
    top_i = jnp.stack([i1, i2], axis=-1)
    top_w = jax.nn.softmax(top_v, axis=-1) * g_gate
    eid = (g_idx[:, None] * per_group + top_i).astype(jnp.int32)

    n_pairs = m * top_k
    n_tiles = (n_pairs + n_experts * (MOE_TM - 1)) // MOE_TM + 1
    n_rows = n_tiles * MOE_TM
    flat_e = eid.reshape(-1)
    order = jnp.argsort(flat_e, stable=True).astype(jnp.int32)
    counts = jnp.dot(jnp.ones((8, n_pairs), BF16), jax.nn.one_hot(flat_e, n_experts, dtype=BF16),
                     preferred_element_type=F32)[0].astype(jnp.int32)
    padded = ((counts + MOE_TM - 1) // MOE_TM) * MOE_TM
    pad_end = jnp.cumsum(padded)
    pad_start = pad_end - padded
    start = jnp.cumsum(counts) - counts
    n_used = (pad_end[-1] // MOE_TM).astype(jnp.int32)
    tile_row = jnp.minimum(jnp.arange(n_tiles, dtype=jnp.int32), n_used - 1) * MOE_TM
    tile_e = jnp.minimum(jnp.sum(pad_end[None, :] <= tile_row[:, None], axis=1, dtype=jnp.int32),
                         n_experts - 1)
    row = jnp.arange(n_rows, dtype=jnp.int32)
    per_row = lambda table: jnp.repeat(table[tile_e], MOE_TM)
    in_group = row - per_row(pad_start)
    live = (in_group < per_row(counts)) & (row < n_used * MOE_TM)
    pair = order[jnp.clip(per_row(start) + in_group, 0, n_pairs - 1)]
    row_token = jnp.where(live, pair // top_k, row % m)
    row_w = jnp.where(live, top_w.reshape(-1)[pair], 0.0)
    pos = jnp.argsort(jnp.where(live, pair, n_pairs + row))[:n_pairs].astype(jnp.int32)
    pos = pos.reshape(m, top_k)
    n_used = n_used.reshape(1)

    xs = jnp.take(h, row_token, axis=0, mode="clip")
    hid = pl.pallas_call(
        _moe_up_kernel,
        grid_spec=pltpu.PrefetchScalarGridSpec(
            num_scalar_prefetch=2, grid=(n_tiles,),
            in_specs=[pl.BlockSpec((MOE_TM, d), lambda i, te, nu: (i, 0)),
                      pl.BlockSpec((None, d, f), lambda i, te, nu: (layer, 0, te[i])),
                      pl.BlockSpec((None, d, f), lambda i, te, nu: (layer, 0, te[i])),
                      pl.BlockSpec((MOE_TM, 1), lambda i, te, nu: (i, 0))],
            out_specs=pl.BlockSpec((MOE_TM, f), lambda i, te, nu: (i, 0)),
            scratch_shapes=[pltpu.VMEM((d, f), BF16), pltpu.VMEM((d, f), BF16)]),
        out_shape=jax.ShapeDtypeStruct((n_rows, f), BF16),
        compiler_params=_cparams(1), name="moe_up",
    )(tile_e, n_used, xs, w_gate, w_up, row_w.reshape(n_rows, 1))
    rows = pl.pallas_call(
        _moe_down_kernel,
        grid_spec=pltpu.PrefetchScalarGridSpec(
            num_scalar_prefetch=2, grid=(n_tiles,),
            in_specs=[pl.BlockSpec((MOE_TM, f), lambda i, te, nu: (i, 0)),
                      pl.BlockSpec((None, f, d), lambda i, te, nu: (layer, te[i], 0))],
            out_specs=pl.BlockSpec((MOE_TM, d), lambda i, te, nu: (i, 0)),
            scratch_shapes=[pltpu.VMEM((f, d), BF16)]),
        out_shape=jax.ShapeDtypeStruct((n_rows, d), F32),
        compiler_params=_cparams(1), name="moe_down",
    )(tile_e, n_used, hid, w_down)

    return _moe_combine(x, rows, pos.reshape(-1), next_gain, split)


def _combine_kernel(pos_ref, x_ref, rows_hbm, *refs, top_k, with_norm, split_tile):
    if with_norm:
        g_ref, xo_ref, h_ref, buf, sem = refs
    else:
        lo_ref, hi_ref, buf, sem = refs
    i = pl.program_id(0)
    n_tiles = pl.num_programs(0)
    tm = x_ref.shape[0]
    slot = i % 2

    def gather_copy(tile, t, k, slot_):
        r = pos_ref[(tile * tm + t) * top_k + k]
        return pltpu.make_async_copy(rows_hbm.at[pl.ds(r, 1)],
                                     buf.at[slot_, pl.ds(k * tm + t, 1)], sem.at[slot_])

    def start_tile(tile, slot_):
        def body(t, carry):
            for k in range(top_k):
                gather_copy(tile, t, k, slot_).start()
            return carry
        lax.fori_loop(0, tm, body, 0, unroll=8)

    @pl.when(i == 0)
    def _():
        start_tile(0, 0)

    @pl.when(i + 1 < n_tiles)
    def _():
        start_tile(i + 1, 1 - slot)

    pltpu.make_async_copy(rows_hbm.at[pl.ds(0, top_k * tm)], buf.at[slot], sem.at[slot]).wait()
    y = x_ref[...]
    for k in range(top_k):
        y = y + buf[slot, k * tm:(k + 1) * tm]
    if with_norm:
        xo_ref[...] = y
        h_ref[...] = (y * lax.rsqrt(jnp.mean(y * y, axis=-1, keepdims=True) + RMS_EPS)
                      * g_ref[...]).astype(BF16)
    else:
        @pl.when(i < split_tile)
        def _():
            lo_ref[...] = y

        @pl.when(i >= split_tile)
        def _():
            hi_ref[...] = y


def _moe_combine(x, rows, pos, next_gain, split):
    m, d = x.shape
    top_k = pos.shape[0] // m
    tm = LANE
    assert m % tm == 0
    with_norm = next_gain is not None
    x_spec = pl.BlockSpec((tm, d), lambda i, p: (i, 0))
    in_specs = [x_spec, pl.BlockSpec(memory_space=pl.ANY)]
    args = [x, rows]
    if with_norm:
        split_tile = 0
        in_specs.append(pl.BlockSpec((1, d), lambda i, p: (0, 0)))
        args.append(next_gain.reshape(1, d))
        out_specs = [x_spec, x_spec]
        out_shape = [jax.ShapeDtypeStruct((m, d), F32), jax.ShapeDtypeStruct((m, d), BF16)]
    else:
        assert split % tm == 0
        split_tile = split // tm
        out_specs = [pl.BlockSpec((tm, d), lambda i, p: (jnp.minimum(i, split_tile - 1), 0)),
                     pl.BlockSpec((tm, d), lambda i, p: (jnp.maximum(i - split_tile, 0), 0))]
        out_shape = [jax.ShapeDtypeStruct((split, d), F32), jax.ShapeDtypeStruct((m - split, d), F32)]
    return pl.pallas_call(
        functools.partial(_combine_kernel, top_k=top_k, with_norm=with_norm, split_tile=split_tile),
        grid_spec=pltpu.PrefetchScalarGridSpec(
            num_scalar_prefetch=1, grid=(m // tm,), in_specs=in_specs, out_specs=out_specs,
            scratch_shapes=[pltpu.VMEM((2, top_k * tm, d), F32), pltpu.SemaphoreType.DMA((2,))]),
        out_shape=out_shape, compiler_params=_cparams(1),
        name="moe_combine_norm" if with_norm else "moe_combine_split",
    )(pos, *args)


def _even_layer(x, h, i, groups, state_ret, state_rwkv, state_shift, wts):
    (a_w_in, a_w_out, ret_norm_g, rwkv_mu, rwkv_w0, rwkv_w2, rwkv_a0, rwkv_a2, rwkv_g2,
     rwkv_kk_scale, rwkv_ka, rwkv_rk, rwkv_lnx_g, rwkv_lnx_b) = [w[i] for w in wts]
    ret_heads, ret_dk = state_ret.shape[2], state_ret.shape[3]
    rw_heads, rw_n = state_rwkv.shape[2], state_rwkv.shape[3]
    ret_w = ret_heads * ret_dk
    rw_w = rw_heads * rw_n
    lora_w, lora_a, lora_g = rwkv_w2.shape[0], rwkv_a2.shape[0], rwkv_g2.shape[0]
    assert lora_w == LANE and lora_a == LANE and lora_g <= LANE and ret_w == rw_w
    n_main = 4 * ret_w + 3 * rw_w
    n_shift = 3 * rw_w + lora_w + lora_a + lora_g
    (pb, pt), (sb, st) = groups
    m_p, m_s = pb * pt, sb * st

    a_w_in_t = a_w_in.T
    p = _matmul([h], a_w_in_t, n_main, name="a_in", w_t=True)
    tail_cols = MM_TN
    assert n_main % tail_cols == 0 and n_main + tail_cols >= a_w_in.shape[1] and 3 * LANE <= tail_cols
    tail = _matmul([h], a_w_in_t, tail_cols, col_block0=n_main // tail_cols, name="a_in_tail",
                   w_t=True)

    o_ret_p, ret_p = _retention_prompt(p, ret_norm_g, pb, pt, ret_heads, ret_dk)
    o_ret_s, ret_s = _retention_sample(p, ret_norm_g, state_ret[i], m_p, sb, st, ret_heads, ret_dk)

    pad_g = lambda a: jnp.pad(a, ((0, LANE - a.shape[0]), (0, 0)))
    mu = rwkv_mu
    mu_tail = jnp.pad(mu[3 * rw_w:], (0, tail_cols - (n_shift - 3 * rw_w)))
    prep_params = [mu[:rw_w].reshape(1, -1), mu[rw_w:2 * rw_w].reshape(1, -1),
                   mu[2 * rw_w:3 * rw_w].reshape(1, -1), mu_tail.reshape(1, -1),
                   rwkv_w0.reshape(1, -1), rwkv_w2, rwkv_a0.reshape(1, -1), rwkv_a2,
                   pad_g(rwkv_g2)]
    cb0 = (4 * ret_w) // rw_w
    shift0 = state_shift[i]
    sh_main = jnp.repeat(shift0[:, :3 * rw_w], st, axis=0)
    sh_tail = jnp.repeat(jnp.pad(shift0[:, 3 * rw_w:], ((0, 0), (0, tail_cols - (n_shift - 3 * rw_w)))),
                         st, axis=0)
    seq_p = _rwkv_prep(p, tail, 0, m_p, cb0, rw_w, prep_params, None, pt, st)
    seq_s = _rwkv_prep(p, tail, m_p, m_s, cb0, rw_w, prep_params, (sh_main, sh_tail), pt, st)

    kparams = (rwkv_kk_scale, rwkv_ka, rwkv_rk, rwkv_lnx_g, rwkv_lnx_b)
    y_p, rwkv_p = _rwkv_scan_tiled(seq_p, kparams, pb, pt, rw_heads, rw_n)
    y_s, rwkv_s = _rwkv_scan(seq_s, kparams, state_rwkv[i], sb, st, rw_heads, rw_n)

    o_ret = jnp.concatenate([o_ret_p, o_ret_s], axis=0)
    y = jnp.concatenate([y_p, y_s], axis=0)
    x = _matmul([o_ret, y], a_w_out, a_w_out.shape[1], res=x, name="a_out")

    def last_ps(row0, nb, nt):
        def last(a, lo, hi):
            if nb * nt <= 1024:
                return a[row0:row0 + nb * nt, lo:hi].reshape(nb, nt, hi - lo)[:, nt - 1]
            return jnp.concatenate([a[row0 + (b + 1) * nt - 1:row0 + (b + 1) * nt, lo:hi]
                                    for b in range(nb)], axis=0)
        return jnp.concatenate([last(p, 4 * ret_w, n_main), last(tail, 0, n_shift - 3 * rw_w)],
                               axis=-1)

    return x, (ret_p, rwkv_p, last_ps(0, pb, pt)), (ret_s, rwkv_s, last_ps(m_p, sb, st))


def _odd_layer(x, h, i, groups, cache_k, cache_v, wts):
    c_w_in, c_w_out, c_q_norm_g, c_k_norm_g, c_sinks = [w[i] for w in wts]
    win, n_kv, hd = cache_k.shape[2], cache_k.shape[3], cache_k.shape[4]
    n_heads = c_sinks.shape[0]
    group = n_heads // n_kv
    mix_c = n_heads * hd
    kv_w = n_kv * hd
    (pb, pt), (sb, st) = groups
    m_p = pb * pt
    assert pt % win == 0 and mix_c % kv_w == 0
    p = _matmul([h], c_w_in, mix_c + 2 * kv_w, name="c_in")
    kcb, vcb = mix_c // kv_w, mix_c // kv_w + 1

    nb_p = m_p // win
    maps = (lambda n: (n, 0), lambda n: (n, kcb), lambda n: (n, vcb),
            lambda n: (jnp.maximum(n - 1, 0), kcb), lambda n: (jnp.maximum(n - 1, 0), vcb))
    o_p, kn_p = _swa(p, p, p, p, p, maps, nb_p, (win, mix_c), win, False, win, c_q_norm_g,
                     c_k_norm_g, c_sinks, n_kv, group, hd, True, False, pt // win)

    tq = 16
    ps = p[m_p:].reshape(sb, st, -1)
    q_s = jnp.pad(ps[:, :, :mix_c].reshape(sb, st, n_kv, group, hd),
                  ((0, 0), (0, tq - st), (0, 0), (0, 0), (0, 0)))
    q_s = q_s.transpose(0, 3, 1, 2, 4).reshape(sb * group * tq, kv_w)
    kv_pad = lambda a: jnp.pad(a, ((0, 0), (0, win - st), (0, 0))).reshape(sb * win, kv_w)
    k_s = kv_pad(ps[:, :, mix_c:mix_c + kv_w])
    v_s = kv_pad(ps[:, :, mix_c + kv_w:])
    kc0 = cache_k[i].reshape(sb * win, kv_w)
    vc0 = cache_v[i].reshape(sb * win, kv_w)
    same = lambda n: (n, 0)
    o_s, kn_s = _swa(q_s, k_s, v_s, kc0, vc0, (same,) * 5, sb, (group * tq, kv_w), tq, True, win,
                     c_q_norm_g, c_k_norm_g, c_sinks, n_kv, group, hd, False, True, 1)
    o_s = o_s.reshape(sb, group, tq, n_kv, hd)[:, :, :st].transpose(0, 2, 3, 1, 4).reshape(
        sb * st, mix_c)

    x = _matmul([jnp.concatenate([o_p, o_s], axis=0)], c_w_out, c_w_out.shape[1], res=x, name="c_out")

    new_k_p = kn_p.reshape(pb, pt, n_kv, hd)[:, -win:]
    new_v_p = p[:m_p, mix_c + kv_w:].reshape(pb, pt, n_kv, hd)[:, -win:]
    kn_s = kn_s.reshape(sb, win, n_kv, hd)[:, :st]
    new_k_s = jnp.concatenate([cache_k[i], kn_s], axis=1)[:, -win:]
    new_v_s = jnp.concatenate([cache_v[i], ps[:, :, mix_c + kv_w:].reshape(sb, st, n_kv, hd)],
                              axis=1)[:, -win:]
    return x, (new_k_p, new_v_p), (new_k_s, new_v_s)


def kernel(x_prompt, x_sample, state_ret, state_rwkv, state_rwkv_shift, cache_swa_k, cache_swa_v, norm_mix_g, norm_ffn_g, a_w_in, a_w_out, ret_norm_g, rwkv_mu, rwkv_w0, rwkv_w2, rwkv_a0, rwkv_a2, rwkv_g2, rwkv_kk_scale, rwkv_ka, rwkv_rk, rwkv_lnx_g, rwkv_lnx_b, c_w_in, c_w_out, c_q_norm_g, c_k_norm_g, c_sinks, moe_w_group, moe_b_group, moe_w_expert, moe_b_expert, moe_w_gate, moe_w_up, moe_w_down):
    pb, pt, d = x_prompt.shape
    sb, st, _ = x_sample.shape
    groups = ((pb, pt), (sb, st))
    depth = norm_mix_g.shape[0]
    a_wts = (a_w_in, a_w_out, ret_norm_g, rwkv_mu, rwkv_w0, rwkv_w2, rwkv_a0, rwkv_a2, rwkv_g2,
             rwkv_kk_scale, rwkv_ka, rwkv_rk, rwkv_lnx_g, rwkv_lnx_b)
    c_wts = (c_w_in, c_w_out, c_q_norm_g, c_k_norm_g, c_sinks)
    st_p = [[] for _ in range(5)]
    st_s = [[] for _ in range(5)]
    for l in range(depth):
        i = l // 2
        if l == 0:
            h, x = _rms_norm_first(x_prompt.reshape(pb * pt, d), x_sample.reshape(sb * st, d),
                                   norm_mix_g[l])
        else:
            h = h_next
        if l % 2 == 0:
            x, sp, ss = _even_layer(x, h, i, groups, state_ret, state_rwkv, state_rwkv_shift, a_wts)
            for j in range(3):
                st_p[j].append(sp[j])
                st_s[j].append(ss[j])
        else:
            x, sp, ss = _odd_layer(x, h, i, groups, cache_swa_k, cache_swa_v, c_wts)
            for j in range(2):
                st_p[3 + j].append(sp[j])
                st_s[3 + j].append(ss[j])
        n_router = moe_w_group.shape[2] + moe_w_expert.shape[2]
        w_router = jnp.pad(jnp.concatenate([moe_w_group[l], moe_w_expert[l]], axis=1),
                           ((0, 0), (0, LANE - n_router)))
        h, logits = _rms_norm_router(x, norm_ffn_g[l], w_router)
        moe_args = (x, h, logits, moe_b_group[l], moe_b_expert[l], moe_w_gate, moe_w_up, moe_w_down, l)
        if l < depth - 1:
            x, h_next = _moe(*moe_args, next_gain=norm_mix_g[l + 1])
        else:
            y_p, y_s = _moe(*moe_args, split=pb * pt)
    y_prompt = y_p.reshape(pb, pt, d)
    y_sample = y_s.reshape(sb, st, d)
    stack = lambda s: s[0][None] if len(s) == 1 else jnp.stack(s)
    return (y_prompt, y_sample) + tuple(stack(s) for s in st_p) + tuple(stack(s) for s in st_s)
```

```python
import functools

import jax
import jax.numpy as jnp
from jax import lax
from jax.experimental import pallas as pl
from jax.experimental.pallas import tpu as pltpu

F32 = jnp.float32
BF16 = jnp.bfloat16

RMS_EPS = 1e-6
RWKV_LN_EPS = 64e-5
RET_CHUNK = 128

LANE = 128
VMEM_LIMIT = 56 * 1024 * 1024

MM_TM = 1040
MM_TN = 512
NORM_TM = 320
MOE_TM = 256
SCAN_TC = 32
RET_HEADS_PER_STEP = 4


def _cparams(n_axes):
    return pltpu.CompilerParams(dimension_semantics=("arbitrary",) * n_axes,
                                vmem_limit_bytes=VMEM_LIMIT)


def _norm_router_kernel(x_ref, g_ref, wr_hi_ref, wr_lo_ref, h_ref, logit_ref):
    x = x_ref[...]
    y = x * lax.rsqrt(jnp.mean(x * x, axis=-1, keepdims=True) + RMS_EPS) * g_ref[...]
    y_hi = y.astype(BF16)
    h_ref[...] = y_hi
    y_lo = (y - y_hi.astype(F32)).astype(BF16)
    w_hi = wr_hi_ref[...]
    logit_ref[...] = (jnp.dot(y_hi, w_hi, preferred_element_type=F32)
                      + jnp.dot(y_hi, wr_lo_ref[...], preferred_element_type=F32)
                      + jnp.dot(y_lo, w_hi, preferred_element_type=F32))


def _norm_first_kernel(xp_ref, xs_ref, g_ref, h_ref, x_ref, *, n_prompt_tiles):
    x = jnp.where(pl.program_id(0) < n_prompt_tiles, xp_ref[...], xs_ref[...])
    x_ref[...] = x
    y = x * lax.rsqrt(jnp.mean(x * x, axis=-1, keepdims=True) + RMS_EPS) * g_ref[...]
    h_ref[...] = y.astype(BF16)


def _rms_norm_first(xp, xs, g):
    (m_p, d), m_s = xp.shape, xs.shape[0]
    tm = LANE
    assert m_p % tm == 0 and m_s % tm == 0
    npt = m_p // tm
    spec = pl.BlockSpec((tm, d), lambda i: (i, 0))
    return pl.pallas_call(
        functools.partial(_norm_first_kernel, n_prompt_tiles=npt), grid=((m_p + m_s) // tm,),
        in_specs=[pl.BlockSpec((tm, d), lambda i: (jnp.minimum(i, npt - 1), 0)),
                  pl.BlockSpec((tm, d), lambda i: (jnp.maximum(i - npt, 0), 0)),
                  pl.BlockSpec((1, d), lambda i: (0, 0))],
        out_specs=[spec, spec],
        out_shape=[jax.ShapeDtypeStruct((m_p + m_s, d), BF16),
                   jax.ShapeDtypeStruct((m_p + m_s, d), F32)],
        compiler_params=_cparams(1), name="rms_norm_first")(xp, xs, g.reshape(1, d))


def _rms_norm_router(x, g, w_router):
    m, d = x.shape
    tm = NORM_TM if m % NORM_TM == 0 else m
    x_spec = pl.BlockSpec((tm, d), lambda i: (i, 0))
    nr = w_router.shape[1]
    w_spec = pl.BlockSpec((d, nr), lambda i: (0, 0))
    w_hi = w_router.astype(BF16)
    w_lo = (w_router - w_hi.astype(F32)).astype(BF16)
    return pl.pallas_call(
        _norm_router_kernel, grid=(m // tm,),
        in_specs=[x_spec, pl.BlockSpec((1, d), lambda i: (0, 0)), w_spec, w_spec],
        out_specs=[x_spec, pl.BlockSpec((tm, nr), lambda i: (i, 0))],
        out_shape=[jax.ShapeDtypeStruct((m, d), BF16), jax.ShapeDtypeStruct((m, nr), F32)],
        compiler_params=_cparams(1), name="rms_norm_router")(x, g.reshape(1, d), w_hi, w_lo)


def _mm_kernel(*refs, n_parts, has_res, cols_valid, w_t):
    x_refs = refs[:n_parts]
    w_refs = refs[n_parts:2 * n_parts]
    pos = 2 * n_parts
    res_ref = refs[pos] if has_res else None
    pos += int(has_res)
    o_ref = refs[pos]
    wb_refs = refs[pos + 1:]
    n_axis = 0 if w_t else 1

    @pl.when(pl.program_id(1) == 0)
    def _():
        for w_ref, wb_ref in zip(w_refs, wb_refs):
            w = w_ref[...]
            if cols_valid is not None:
                col = (lax.broadcasted_iota(jnp.int32, w.shape, n_axis)
                       + pl.program_id(0) * w.shape[n_axis])
                w = jnp.where(col < cols_valid, w, 0.0)
            wb_ref[...] = w.astype(BF16)

    acc = None
    for x_ref, wb_ref in zip(x_refs, wb_refs):
        d = lax.dot_general(x_ref[...], wb_ref[...], (((1,), (1 if w_t else 0,)), ((), ())),
                            preferred_element_type=F32)
        acc = d if acc is None else acc + d
    if has_res:
        acc = acc + res_ref[...]
    o_ref[...] = acc


def _matmul(xs, w, n_out, col_block0=0, tn=MM_TN, res=None, name="proj", w_t=False):
    k_total, n_total = (w.shape[1], w.shape[0]) if w_t else w.shape
    over = (col_block0 * tn + n_out) > n_total
    cols_valid = n_total - col_block0 * tn if over else None
    m = xs[0].shape[0]
    tm = MM_TM if m % MM_TM == 0 else m
    assert n_out % tn == 0
    grid = (n_out // tn, m // tm)
    in_specs, w_specs, scratch = [], [], []
    row = 0
    for x in xs:
        kp = x.shape[1]
        in_specs.append(pl.BlockSpec((tm, kp), lambda j, i: (i, 0)))
        assert row % kp == 0
        rb = row // kp
        if w_t:
            w_specs.append(pl.BlockSpec((tn, kp), lambda j, i, rb=rb: (j + col_block0, rb)))
            scratch.append(pltpu.VMEM((tn, kp), BF16))
        else:
            w_specs.append(pl.BlockSpec((kp, tn), lambda j, i, rb=rb: (rb, j + col_block0)))
            scratch.append(pltpu.VMEM((kp, tn), BF16))
        row += kp
    assert row == k_total
    args = list(xs) + [w] * len(xs)
    in_specs = in_specs + w_specs
    if res is not None:
        in_specs.append(pl.BlockSpec((tm, tn), lambda j, i: (i, j)))
        args.append(res)
    return pl.pallas_call(
        functools.partial(_mm_kernel, n_parts=len(xs), has_res=res is not None,
                          cols_valid=cols_valid, w_t=w_t),
        grid=grid, in_specs=in_specs,
        out_specs=pl.BlockSpec((tm, tn), lambda j, i: (i, j)),
        out_shape=jax.ShapeDtypeStruct((m, n_out), F32),
        scratch_shapes=scratch, compiler_params=_cparams(2), name=name)(*args)


def _ret_finish(o, g, gain):
    o = o * lax.rsqrt(jnp.mean(o * o, axis=-1, keepdims=True) + RMS_EPS) * gain
    return ((g / (1.0 + jnp.exp(-g))) * o).astype(BF16)


def _ret_prompt_kernel(q_ref, k_ref, v_ref, g_ref, gain_ref, lg_ref, o_ref, s_ref, *, dk):
    c = pl.program_id(2)
    chunk = q_ref.shape[0]

    @pl.when(c == 0)
    def _():
        s_ref[...] = jnp.zeros_like(s_ref)

    row = lax.broadcasted_iota(jnp.int32, (chunk, chunk), 0)
    col = lax.broadcasted_iota(jnp.int32, (chunk, chunk), 1)
    diff = (row - col).astype(F32)
    causal = diff >= 0
    dist = jnp.where(causal, diff, 0.0)
    pos = lax.broadcasted_iota(jnp.int32, (chunk, 1), 0).astype(F32)

    for hh in range(s_ref.shape[0]):
        sl = slice(hh * dk, (hh + 1) * dk)
        lg = lg_ref[hh][:, :1]
        decay_mask = jnp.where(causal, jnp.exp(dist * lg), 0.0)
        q_decay = jnp.exp((pos + 1.0) * lg)
        k_decay = jnp.exp((chunk - 1.0 - pos) * lg)
        chunk_decay = jnp.exp(chunk * lg)
        q = q_ref[:, sl]
        k = k_ref[:, sl] * (dk ** -0.5)
        vb = v_ref[:, sl].astype(BF16)
        s = s_ref[hh]
        scores = lax.dot_general(q.astype(BF16), k.astype(BF16), (((1,), (1,)), ((), ())),
                                 preferred_element_type=F32) * decay_mask
        inner = jnp.dot(scores.astype(BF16), vb, preferred_element_type=F32)
        cross = jnp.dot((q * q_decay).astype(BF16), s.astype(BF16), preferred_element_type=F32)
        s_ref[hh] = chunk_decay * s + lax.dot_general(
            (k * k_decay).astype(BF16), vb, (((0,), (0,)), ((), ())), preferred_element_type=F32)
        o_ref[:, sl] = _ret_finish(inner + cross, g_ref[:, sl], gain_ref[hh])


def _ret_sample_kernel(q_ref, k_ref, v_ref, g_ref, gain_ref, lg_ref, s0_ref, o_ref, s_ref, acc_ref,
                       *, dk, t_len):
    b = pl.program_id(1)
    rows = q_ref.shape[0]
    r_b = lax.broadcasted_iota(jnp.int32, (rows, 1), 0) // t_len
    r_t = (lax.broadcasted_iota(jnp.int32, (rows, 1), 0) % t_len).astype(F32)
    sel = r_b == b

    for hh in range(s_ref.shape[0]):
        sl = slice(hh * dk, (hh + 1) * dk)
        lg = lg_ref[hh][:, :1]
        q = q_ref[:, sl]
        k = k_ref[:, sl] * (dk ** -0.5)
        vb = v_ref[:, sl].astype(BF16)

        @pl.when(b == 0)
        def _(hh=hh, lg=lg, q=q, k=k, vb=vb):
            row = lax.broadcasted_iota(jnp.int32, (rows, rows), 0)
            col = lax.broadcasted_iota(jnp.int32, (rows, rows), 1)
            diff = (row % t_len - col % t_len).astype(F32)
            ok = (row // t_len == col // t_len) & (diff >= 0)
            mask = jnp.where(ok, jnp.exp(jnp.where(ok, diff, 0.0) * lg), 0.0)
            scores = lax.dot_general(q.astype(BF16), k.astype(BF16), (((1,), (1,)), ((), ())),
                                     preferred_element_type=F32) * mask
            acc_ref[hh] = jnp.dot(scores.astype(BF16), vb, preferred_element_type=F32)

        s0 = s0_ref[hh]
        cross = jnp.dot((q * jnp.exp((r_t + 1.0) * lg)).astype(BF16), s0.astype(BF16),
                        preferred_element_type=F32)
        acc_ref[hh] += jnp.where(sel, cross, 0.0)
        k_decay = jnp.where(sel, jnp.exp((t_len - 1.0 - r_t) * lg), 0.0)
        s_ref[hh] = jnp.exp(t_len * lg) * s0 + lax.dot_general(
            (k * k_decay).astype(BF16), vb, (((0,), (0,)), ((), ())), preferred_element_type=F32)

        @pl.when(b == pl.num_programs(1) - 1)
        def _(hh=hh, sl=sl):
            o_ref[:, sl] = _ret_finish(acc_ref[hh], g_ref[:, sl], gain_ref[hh])


def _ret_log_decay(n_heads):
    lg = jnp.log(1.0 - 2.0 ** (-5.0 - jnp.arange(n_heads, dtype=F32)))
    return jnp.broadcast_to(lg[:, None, None], (n_heads, 1, LANE))


def _retention_prompt(p, gain, n_batch, seq, n_heads, dk):
    nc = seq // RET_CHUNK
    hps = RET_HEADS_PER_STEP
    assert n_heads % hps == 0
    ng = n_heads // hps
    col = lambda grp: (lambda b, h, c: (b * nc + c, grp * ng + h))
    blk = lambda grp: pl.BlockSpec((RET_CHUNK, hps * dk), col(grp))
    return pl.pallas_call(
        functools.partial(_ret_prompt_kernel, dk=dk),
        grid=(n_batch, ng, nc),
        in_specs=[blk(0), blk(1), blk(2), blk(3),
                  pl.BlockSpec((hps, 1, dk), lambda b, h, c: (h, 0, 0)),
                  pl.BlockSpec((hps, 1, LANE), lambda b, h, c: (h, 0, 0))],
        out_specs=[pl.BlockSpec((RET_CHUNK, hps * dk), lambda b, h, c: (b * nc + c, h)),
                   pl.BlockSpec((None, hps, dk, dk), lambda b, h, c: (b, h, 0, 0))],
        out_shape=[jax.ShapeDtypeStruct((n_batch * seq, n_heads * dk), BF16),
                   jax.ShapeDtypeStruct((n_batch, n_heads, dk, dk), F32)],
        compiler_params=_cparams(3), name="retention_prompt",
    )(p, p, p, p, gain.reshape(n_heads, 1, dk), _ret_log_decay(n_heads))


def _retention_sample(p, gain, s0, row0, n_batch, t_len, n_heads, dk):
    rows = n_batch * t_len
    rb = row0 // rows
    hps = RET_HEADS_PER_STEP
    assert n_heads % hps == 0
    ng = n_heads // hps
    blk = lambda grp: pl.BlockSpec((rows, hps * dk), lambda h, b, grp=grp: (rb, grp * ng + h))
    return pl.pallas_call(
        functools.partial(_ret_sample_kernel, dk=dk, t_len=t_len),
        grid=(ng, n_batch),
        in_specs=[blk(0), blk(1), blk(2), blk(3),
                  pl.BlockSpec((hps, 1, dk), lambda h, b: (h, 0, 0)),
                  pl.BlockSpec((hps, 1, LANE), lambda h, b: (h, 0, 0)),
                  pl.BlockSpec((None, hps, dk, dk), lambda h, b: (b, h, 0, 0))],
        out_specs=[pl.BlockSpec((rows, hps * dk), lambda h, b: (0, h)),
                   pl.BlockSpec((None, hps, dk, dk), lambda h, b: (b, h, 0, 0))],
        out_shape=[jax.ShapeDtypeStruct((rows, n_heads * dk), BF16),
                   jax.ShapeDtypeStruct((n_batch, n_heads, dk, dk), F32)],
        scratch_shapes=[pltpu.VMEM((hps, rows, dk), F32)],
        compiler_params=_cparams(2), name="retention_sample",
    )(p, p, p, p, gain.reshape(n_heads, 1, dk), _ret_log_decay(n_heads), s0)


def _sigmoid(x):
    return 1.0 / (1.0 + jnp.exp(-x))


def _rwkv_prep_kernel(*refs, sample, seq, t_len):
    if sample:
        (r_c, k_c, v_c, t_c, r_s, k_s, v_s, t_s,
         mu_r, mu_k, mu_v, mu_t, w0, w2, a0, a2, g2,
         r_o, w_o, k_o, v_o, a_o, g_o) = refs
    else:
        (r_c, k_c, v_c, t_c, r_p, k_p, v_p, t_p,
         mu_r, mu_k, mu_v, mu_t, w0, w2, a0, a2, g2,
         r_o, w_o, k_o, v_o, a_o, g_o) = refs
    tm = r_c.shape[0]
    row = lax.broadcasted_iota(jnp.int32, (tm, 1), 0)
    seq_start = (pl.program_id(0) * tm) % seq == 0

    def mix(cur_ref, other_ref, mu_ref):
        cur = cur_ref[...]
        rolled = pltpu.roll(cur, 1, 0)
        if sample:
            prev = jnp.where(row % t_len == 0, other_ref[...], rolled)
        else:
            last = other_ref[7:8, :]
            first = jnp.where(seq_start, jnp.zeros_like(last), last)
            prev = jnp.where(row == 0, first, rolled)
        return cur + (prev - cur) * mu_ref[...]

    def store(o_ref, val):
        if len(o_ref.shape) == 2:
            o_ref[...] = val
        else:
            for c in range(o_ref.shape[1]):
                o_ref[:, c] = val[:, c * LANE:(c + 1) * LANE].reshape(o_ref.shape[0], 8, LANE)

    other = (r_s, k_s, v_s, t_s) if sample else (r_p, k_p, v_p, t_p)
    store(r_o, mix(r_c, other[0], mu_r))
    store(k_o, mix(k_c, other[1], mu_k))
    store(v_o, mix(v_c, other[2], mu_v))
    tail = mix(t_c, other[3], mu_t)
    lw, la, lgt = tail[:, :LANE], tail[:, LANE:2 * LANE], tail[:, 2 * LANE:3 * LANE]

    z = -(w0[...] + jnp.dot(jnp.tanh(lw).astype(BF16), w2[...].astype(BF16),
                            preferred_element_type=F32))
    softplus = jnp.maximum(z, 0.0) + jnp.log1p(jnp.exp(-jnp.abs(z)))
    store(w_o, jnp.exp(-jnp.exp(-softplus - 0.5)))
    store(a_o, _sigmoid(a0[...] + jnp.dot(la.astype(BF16), a2[...].astype(BF16),
                                          preferred_element_type=F32)))
    store(g_o, jnp.dot(_sigmoid(lgt).astype(BF16), g2[...].astype(BF16),
                       preferred_element_type=F32))


def _rwkv_prep(p, tail, row0, n_rows, col_block0, width, params, shift_rows, seq, t_len):
    sample = shift_rows is not None
    tm = n_rows if sample else 256
    rb0 = row0 // tm
    grid = (n_rows // tm,)
    cur = lambda grp: pl.BlockSpec((tm, width), lambda i, grp=grp: (rb0 + i, col_block0 + grp))
    tail_w = tail.shape[1]
    in_specs = [cur(0), cur(1), cur(2), pl.BlockSpec((tm, tail_w), lambda i: (rb0 + i, 0))]
    args = [p, p, p, tail]
    if sample:
        sh_main, sh_tail = shift_rows
        in_specs += [pl.BlockSpec((tm, width), lambda i, grp=grp: (0, grp)) for grp in range(3)]
        in_specs += [pl.BlockSpec((tm, tail_w), lambda i: (0, 0))]
        args += [sh_main, sh_main, sh_main, sh_tail]
    else:
        per = tm // 8
        prev_rb = lambda i: jnp.maximum((rb0 + i) * per - 1, 0)
        in_specs += [pl.BlockSpec((8, width), lambda i, grp=grp: (prev_rb(i), col_block0 + grp))
                     for grp in range(3)]
        in_specs += [pl.BlockSpec((8, tail_w), lambda i: (prev_rb(i), 0))]
        args += [p, p, p, tail]
    full = lambda a: pl.BlockSpec(a.shape, lambda i: (0,) * a.ndim)
    in_specs += [full(a) for a in params]
    args += list(params)
    if sample:
        out_spec = pl.BlockSpec((tm, width), lambda i: (i, 0))
        out_shape = jax.ShapeDtypeStruct((n_rows, width), F32)
    else:
        out_spec = pl.BlockSpec((tm // 8, width // LANE, 8, LANE), lambda i: (i, 0, 0, 0))
        out_shape = jax.ShapeDtypeStruct((n_rows // 8, width // LANE, 8, LANE), F32)
    return pl.pallas_call(
        functools.partial(_rwkv_prep_kernel, sample=sample, seq=seq, t_len=t_len),
        grid=grid, in_specs=in_specs, out_specs=[out_spec] * 6, out_shape=[out_shape] * 6,
        compiler_params=_cparams(1), name="rwkv_prep_sample" if sample else "rwkv_prep_prompt",
    )(*args)


def _rwkv_step(s_ref, ys_ref, r, w, kr, v, a, g, par_refs, between=None):
    kks_ref, ka_ref, rk_ref, lng_ref, lnb_ref = par_refs
    kks = kr * kks_ref[...]
    kk = kks / jnp.maximum(jnp.sqrt(jnp.sum(kks * kks, axis=0, keepdims=True)), 1e-12)
    kf = kr * (1.0 + (a - 1.0) * ka_ref[...])
    kka = kk * a
    for vi in range(s_ref.shape[0]):
        s = s_ref[vi]
        sa = jnp.sum(s * kk, axis=0, keepdims=True)
        s = s * w - sa * kka + v[vi:vi + 1, :] * kf
        s_ref[vi] = s
        ys_ref[vi:vi + 1, :] = jnp.sum(s * r, axis=0, keepdims=True)
        if between is not None:
            between(vi)
    y = ys_ref[...]
    mu = jnp.mean(y, axis=0, keepdims=True)
    var = jnp.mean(jnp.square(y - mu), axis=0, keepdims=True)
    y = (y - mu) * lax.rsqrt(var + RWKV_LN_EPS) * lng_ref[...] + lnb_ref[...]
    bonus = jnp.sum(r * kf * rk_ref[...], axis=0, keepdims=True) * v
    return (y + bonus) * g


def _rwkv_scan_kernel(r_ref, w_ref, k_ref, v_ref, a_ref, g_ref,
                      kks_ref, ka_ref, rk_ref, lng_ref, lnb_ref, s0_ref,
                      y_ref, s_ref, ys_ref):
    par_refs = (kks_ref, ka_ref, rk_ref, lng_ref, lnb_ref)

    @pl.when(pl.program_id(1) == 0)
    def _():
        s_ref[...] = s0_ref[...]

    def step(t, carry):
        y = _rwkv_step(s_ref, ys_ref, r_ref[t], w_ref[t], k_ref[t], v_ref[t], a_ref[t], g_ref[t],
                       par_refs)
        y_ref[t] = y.astype(y_ref.dtype)
        return carry

    lax.fori_loop(0, r_ref.shape[0], step, 0)


def _rwkv_scan_tiled_kernel(r_ref, w_ref, k_ref, v_ref, a_ref, g_ref,
                            kks_ref, ka_ref, rk_ref, lng_ref, lnb_ref, s0_ref,
                            y_ref, s_ref, ys_ref, nxt_ref):
    seq_refs = (r_ref, w_ref, k_ref, v_ref, a_ref, g_ref)
    par_refs = (kks_ref, ka_ref, rk_ref, lng_ref, lnb_ref)
    n_grp = r_ref.shape[1]
    n = s_ref.shape[0]
    every = n // (len(seq_refs) + 1)

    @pl.when(pl.program_id(0) == 0)
    def _():
        s_ref[...] = s0_ref[...]

    def fetch(i, tg, t8, slot):
        zt = seq_refs[i][:, tg, :, t8, :].reshape(-1, LANE).T
        nxt_ref[slot, i] = jnp.concatenate([zt[:n], zt[n:]], axis=1)

    for i in range(len(seq_refs)):
        fetch(i, 0, 0, 0)

    def group(tg, carry):
        for t8 in range(8):
            slot = t8 % 2
            nxt_tg = tg if t8 < 7 else jnp.minimum(tg + 1, n_grp - 1)

            def between(vi, slot=slot, nxt_tg=nxt_tg, t8=t8):
                if vi % every == every - 1 and vi // every < len(seq_refs):
                    fetch(vi // every, nxt_tg, (t8 + 1) % 8, 1 - slot)

            y = _rwkv_step(s_ref, ys_ref, *[nxt_ref[slot, i] for i in range(len(seq_refs))],
                           par_refs, between)
            y_ref[tg * 8 + t8] = y.astype(y_ref.dtype)
        return carry

    lax.fori_loop(0, n_grp, group, 0)


def _rwkv_scan(seqs, kparams, s0, n_batch, t_len, n_heads, n):
    pairs = n_batch * n_heads
    tc = SCAN_TC if t_len % SCAN_TC == 0 else t_len
    grid = (pairs // LANE, t_len // tc)
    seq_spec = pl.BlockSpec((tc, n, LANE), lambda g, t: (t, 0, g))
    par_spec = pl.BlockSpec((n, LANE), lambda g, t: (0, g))
    st_spec = pl.BlockSpec((n, n, LANE), lambda g, t: (0, 0, g))
    pairs_major = lambda a: a.reshape(n_batch, t_len, n_heads, n).transpose(1, 3, 0, 2).reshape(
        t_len, n, pairs)
    params = [jnp.tile(a.reshape(n_heads, n).T, (1, n_batch)) for a in kparams]
    y, s_t = pl.pallas_call(
        _rwkv_scan_kernel, grid=grid,
        in_specs=[seq_spec] * 6 + [par_spec] * 5 + [st_spec],
        out_specs=[seq_spec, st_spec],
        out_shape=[jax.ShapeDtypeStruct((t_len, n, pairs), BF16),
                   jax.ShapeDtypeStruct((n, n, pairs), F32)],
        scratch_shapes=[pltpu.VMEM((n, LANE), F32)],
        compiler_params=_cparams(2), name="rwkv_scan",
    )(*[pairs_major(a) for a in seqs], *params, s0.transpose(2, 3, 0, 1).reshape(n, n, pairs))
    y = y.reshape(t_len, n, n_batch, n_heads).transpose(2, 0, 3, 1)
    s_t = s_t.reshape(n, n, n_batch, n_heads).transpose(2, 3, 0, 1)
    return y.reshape(n_batch * t_len, n_heads * n), s_t


def _rwkv_scan_tiled(seqs, kparams, n_batch, t_len, n_heads, n):
    chunks = n_heads * n // LANE
    assert n_batch * n_heads == LANE and 2 * n == LANE and t_len % SCAN_TC == 0 and SCAN_TC % 16 == 0
    tc = SCAN_TC
    seq_spec = pl.BlockSpec((n_batch, tc // 8, chunks, 8, LANE), lambda t: (0, t, 0, 0, 0))
    par_spec = pl.BlockSpec((n, LANE), lambda t: (0, 0))
    st_spec = pl.BlockSpec((n, n, LANE), lambda t: (0, 0, 0))
    params = [jnp.broadcast_to(a.reshape(chunks, 2, n).transpose(2, 1, 0)[:, :, None, :],
                               (n, 2, n_batch, chunks)).reshape(n, LANE) for a in kparams]
    y, s_t = pl.pallas_call(
        _rwkv_scan_tiled_kernel, grid=(t_len // tc,),
        in_specs=[seq_spec] * 6 + [par_spec] * 5 + [st_spec],
        out_specs=[pl.BlockSpec((tc, n, LANE), lambda t: (t, 0, 0)), st_spec],
        out_shape=[jax.ShapeDtypeStruct((t_len, n, LANE), BF16),
                   jax.ShapeDtypeStruct((n, n, LANE), F32)],
        scratch_shapes=[pltpu.VMEM((n, LANE), F32), pltpu.VMEM((2, 6, n, LANE), F32)],
        compiler_params=_cparams(1), name="rwkv_scan_tiled",
    )(*[a.reshape(n_batch, t_len // 8, chunks, 8, LANE) for a in seqs], *params,
      jnp.zeros((n, n, LANE), F32))
    y = y.reshape(t_len, n, 2, n_batch, chunks).transpose(0, 1, 3, 4, 2)
    y = y.reshape(t_len, n, n_batch, n_heads).transpose(2, 0, 3, 1)
    s_t = s_t.reshape(n, n, 2, n_batch, chunks).transpose(3, 4, 2, 0, 1)
    return y.reshape(n_batch * t_len, n_heads * n), s_t.reshape(n_batch, n_heads, n, n)


def _swa_kernel(sink_ref, q_ref, kc_ref, vc_ref, kp_ref, vp_ref, qg_ref, kg_ref,
                o_ref, kn_ref, s_scr, e_scr, inv_scr, *, n_kv, group, hd, rows_per_head, stacked,
                norm_prev, first_has_prev, blocks_per_seq):
    rph = rows_per_head
    win = kc_ref.shape[0]
    n_heads = n_kv * group
    parts = 1 if stacked else 2
    has_prev = jnp.logical_or(first_has_prev, pl.program_id(0) % blocks_per_seq > 0)
    nt = (((1,), (1,)), ((), ()))

    def rms(x, g):
        return x * lax.rsqrt(jnp.mean(x * x, axis=-1, keepdims=True) + RMS_EPS) * g

    qpos = lax.broadcasted_iota(jnp.int32, (rph, 2 * win), 0)
    kpos = lax.broadcasted_iota(jnp.int32, (rph, 2 * win), 1) - win
    diff = qpos - kpos
    valid = (diff >= 0) & (diff < win) & (kpos >= jnp.where(has_prev, -win, 0))
    neg_dist = jnp.where(valid, -(diff.astype(F32)), -jnp.inf)
    qg = qg_ref[...]
    kg = kg_ref[...]
    qg2 = jnp.concatenate([qg, qg], axis=1)
    left = lax.broadcasted_iota(jnp.int32, (rph, 2 * hd), 1) < hd

    def keys_values(j):
        sl = slice(j * hd, (j + 1) * hd)
        kc = rms(kc_ref[:, sl], kg)
        kp = kp_ref[:, sl]
        if norm_prev:
            kp = rms(kp, kg)
        kb = jnp.concatenate([kp, kc], axis=0)
        vb = jnp.concatenate([vp_ref[:, sl], vc_ref[:, sl]], axis=0)
        return kc, kb, vb

    def halves(x):
        z = jnp.zeros_like(x)
        return (jnp.concatenate([x, z], axis=1).astype(BF16),
                jnp.concatenate([z, x], axis=1).astype(BF16))

    for j in range(n_kv):
        kc, kb, _ = keys_values(j)
        kn_ref[:, j * hd:(j + 1) * hd] = kc
        if stacked:
            q = rms(q_ref[:, j * hd:(j + 1) * hd], qg).astype(BF16)
            s_scr[j, 0] = lax.dot_general(q, kb.astype(BF16), nt, preferred_element_type=F32)
        else:
            qs = []
            for pp in range(group // 2):
                lo = (j * group + 2 * pp) * hd
                qp = q_ref[:, lo:lo + 2 * hd]
                sq = qp * qp
                tot = jnp.sum(sq, axis=-1, keepdims=True)
                lsum = jnp.sum(jnp.where(left, sq, 0.0), axis=-1, keepdims=True)
                r_l = lax.rsqrt(lsum / hd + RMS_EPS)
                r_r = lax.rsqrt((tot - lsum) / hd + RMS_EPS)
                qs.append((qp * jnp.where(left, r_l, r_r) * qg2).astype(BF16))
            q = jnp.concatenate(qs, axis=0)
            kb0, kb1 = halves(kb)
            s_scr[j, 0] = lax.dot_general(q, kb0, nt, preferred_element_type=F32)
            s_scr[j, 1] = lax.dot_general(q, kb1, nt, preferred_element_type=F32)

    for j in range(n_kv):
        for blk in range(group // parts):
            inv = []
            for part in range(parts):
                h = j * group + blk * parts + part
                rows = slice(blk * rph, (blk + 1) * rph)
                slope = 2.0 ** (-8.0 * (h + 1) / n_heads)
                sh = s_scr[j, part, rows, :] * (hd ** -0.5) + slope * neg_dist
                sink = sink_ref[h]
                m = jnp.maximum(jnp.max(sh, axis=-1, keepdims=True), sink)
                e = jnp.exp(sh - m)
                inv.append(1.0 / (jnp.sum(e, axis=-1, keepdims=True) + jnp.exp(sink - m)))
                e_scr[j, part, rows, :] = e.astype(BF16)
            if stacked:
                inv_scr[j, rows, :] = jnp.broadcast_to(inv[0], (rph, 2 * hd))
            else:
                inv_scr[j, rows, :] = jnp.where(left, inv[0], inv[1])

    for j in range(n_kv):
        _, _, vb = keys_values(j)
        if stacked:
            o = jnp.dot(e_scr[j, 0], vb.astype(BF16), preferred_element_type=F32)
            o_ref[:, j * hd:(j + 1) * hd] = (o * inv_scr[j][:, :hd]).astype(BF16)
        else:
            vb0, vb1 = halves(vb)
            o = (jnp.dot(e_scr[j, 0], vb0, preferred_element_type=F32)
                 + jnp.dot(e_scr[j, 1], vb1, preferred_element_type=F32))
            o = (o * inv_scr[j]).astype(BF16)
            for pp in range(group // 2):
                lo = (j * group + 2 * pp) * hd
                o_ref[:, lo:lo + 2 * hd] = o[pp * rph:(pp + 1) * rph]


def _swa(q_arr, kc_arr, vc_arr, kp_arr, vp_arr, maps, n_blocks, q_block, rows_per_head, stacked,
         win, qg, kg, sinks, n_kv, group, hd, norm_prev, first_has_prev, blocks_per_seq):
    q_map, kc_map, vc_map, kp_map, vp_map = maps
    parts = 1 if stacked else 2
    rows = group // parts * rows_per_head
    assert 2 * hd == LANE and group % 2 == 0
    return pl.pallas_call(
        functools.partial(_swa_kernel, n_kv=n_kv, group=group, hd=hd, rows_per_head=rows_per_head,
                          stacked=stacked, norm_prev=norm_prev, first_has_prev=first_has_prev,
                          blocks_per_seq=blocks_per_seq),
        grid=(n_blocks,),
        in_specs=[pl.BlockSpec(memory_space=pltpu.SMEM),
                  pl.BlockSpec(q_block, q_map),
                  pl.BlockSpec((win, n_kv * hd), kc_map),
                  pl.BlockSpec((win, n_kv * hd), vc_map),
                  pl.BlockSpec((win, n_kv * hd), kp_map),
                  pl.BlockSpec((win, n_kv * hd), vp_map),
                  pl.BlockSpec((1, hd), lambda i: (0, 0)),
                  pl.BlockSpec((1, hd), lambda i: (0, 0))],
        out_specs=[pl.BlockSpec(q_block, lambda i: (i, 0)),
                   pl.BlockSpec((win, n_kv * hd), lambda i: (i, 0))],
        out_shape=[jax.ShapeDtypeStruct((n_blocks * q_block[0], q_block[1]), BF16),
                   jax.ShapeDtypeStruct((n_blocks * win, n_kv * hd), F32)],
        scratch_shapes=[pltpu.VMEM((n_kv, parts, rows, 2 * win), F32),
                        pltpu.VMEM((n_kv, parts, rows, 2 * win), BF16),
                        pltpu.VMEM((n_kv, rows, LANE), F32)],
        compiler_params=_cparams(1), name="swa_stacked%d" % int(stacked),
    )(sinks, q_arr, kc_arr, vc_arr, kp_arr, vp_arr, qg.reshape(1, hd), kg.reshape(1, hd))


def _moe_up_kernel(te_ref, nu_ref, xs_ref, wg_ref, wu_ref, rw_ref, hid_ref, wgb_ref, wub_ref):
    i = pl.program_id(0)
    changed = jnp.logical_or(i == 0, te_ref[i] != te_ref[jnp.maximum(i - 1, 0)])

    @pl.when(changed)
    def _():
        wgb_ref[...] = wg_ref[...].astype(BF16)
        wub_ref[...] = wu_ref[...].astype(BF16)

    @pl.when(i < nu_ref[0])
    def _():
        x = xs_ref[...]
        g = jnp.dot(x, wgb_ref[...], preferred_element_type=F32)
        u = jnp.dot(x, wub_ref[...], preferred_element_type=F32)
        hid_ref[...] = ((g / (1.0 + jnp.exp(-g))) * u * rw_ref[...]).astype(BF16)

    @pl.when(i >= nu_ref[0])
    def _():
        hid_ref[...] = jnp.zeros_like(hid_ref)


def _moe_down_kernel(te_ref, nu_ref, hid_ref, wd_ref, out_ref, wdb_ref):
    i = pl.program_id(0)
    changed = jnp.logical_or(i == 0, te_ref[i] != te_ref[jnp.maximum(i - 1, 0)])

    @pl.when(changed)
    def _():
        wdb_ref[...] = wd_ref[...].astype(BF16)

    @pl.when(i < nu_ref[0])
    def _():
        out_ref[...] = jnp.dot(hid_ref[...], wdb_ref[...], preferred_element_type=F32)

    @pl.when(i >= nu_ref[0])
    def _():
        out_ref[...] = jnp.zeros_like(out_ref)


def _moe(x, h, logits, b_group, b_expert, w_gate, w_up, w_down, layer, next_gain=None, split=None):
    m, d = h.shape
    n_groups = b_group.shape[0]
    n_experts = b_expert.shape[0]
    per_group = n_experts // n_groups
    f = w_gate.shape[2] // n_experts
    top_k = 2

    g_logit = logits[:, :n_groups] + b_group
    g_prob = jax.nn.softmax(g_logit, axis=-1)
    g_idx = jnp.argmax(g_logit, axis=-1)
    g_gate = jnp.take_along_axis(g_prob, g_idx[:, None], axis=-1)
    e_logit = (logits[:, n_groups:n_groups + n_experts] + b_expert).reshape(m, n_groups, per_group)
    e_in = jnp.take_along_axis(e_logit, g_idx[:, None, None], axis=1)[:, 0]
    i1 = jnp.argmax(e_in, axis=-1)
    rest = jnp.where(jnp.arange(per_group)[None, :] == i1[:, None], -jnp.inf, e_in)
    i2 = jnp.argmax(rest, axis=-1)
    top_v = jnp.stack([jnp.max(e_in, axis=-1), jnp.max(rest, axis=-1)], axis=-1)
    top_i = jnp.stack([i1, i2], axis=-1)
    top_w = jax.nn.softmax(top_v, axis=-1) * g_gate
    eid = (g_idx[:, None] * per_group + top_i).astype(jnp.int32)

    n_pairs = m * top_k
    n_tiles = (n_pairs + n_experts * (MOE_TM - 1)) // MOE_TM + 1
    n_rows = n_tiles * MOE_TM
    flat_e = eid.reshape(-1)
    order = jnp.argsort(flat_e, stable=True).astype(jnp.int32)
    counts = jnp.dot(jnp.ones((8, n_pairs), BF16), jax.nn.one_hot(flat_e, n_experts, dtype=BF16),
                     preferred_element_type=F32)[0].astype(jnp.int32)
    padded = ((counts + MOE_TM - 1) // MOE_TM) * MOE_TM
    pad_end = jnp.cumsum(padded)
    pad_start = pad_end - padded
    start = jnp.cumsum(counts) - counts
    n_used = (pad_end[-1] // MOE_TM).astype(jnp.int32)
    tile_row = jnp.minimum(jnp.arange(n_tiles, dtype=jnp.int32), n_used - 1) * MOE_TM
    tile_e = jnp.minimum(jnp.sum(pad_end[None, :] <= tile_row[:, None], axis=1, dtype=jnp.int32),
                         n_experts - 1)
    row = jnp.arange(n_rows, dtype=jnp.int32)
    per_row = lambda table: jnp.repeat(table[tile_e], MOE_TM)
    in_group = row - per_row(pad_start)
    live = (in_group < per_row(counts)) & (row < n_used * MOE_TM)
    pair = order[jnp.clip(per_row(start) + in_group, 0, n_pairs - 1)]
    row_token = jnp.where(live, pair // top_k, row % m)
    row_w = jnp.where(live, top_w.reshape(-1)[pair], 0.0)
    pos = jnp.argsort(jnp.where(live, pair, n_pairs + row))[:n_pairs].astype(jnp.int32)
    pos = pos.reshape(m, top_k)
    n_used = n_used.reshape(1)

    def up(t0, t1):
        r0, r1 = t0 * MOE_TM, t1 * MOE_TM
        xs = jnp.take(h, row_token[r0:r1], axis=0, mode="clip")
        return pl.pallas_call(
            _moe_up_kernel,
            grid_spec=pltpu.PrefetchScalarGridSpec(
                num_scalar_prefetch=2, grid=(t1 - t0,),
                in_specs=[pl.BlockSpec((MOE_TM, d), lambda i, te, nu: (i, 0)),
                          pl.BlockSpec((None, d, f), lambda i, te, nu: (layer, 0, te[i])),
                          pl.BlockSpec((None, d, f), lambda i, te, nu: (layer, 0, te[i])),
                          pl.BlockSpec((MOE_TM, 1), lambda i, te, nu: (i, 0))],
                out_specs=pl.BlockSpec((MOE_TM, f), lambda i, te, nu: (i, 0)),
                scratch_shapes=[pltpu.VMEM((d, f), BF16), pltpu.VMEM((d, f), BF16)]),
            out_shape=jax.ShapeDtypeStruct((r1 - r0, f), BF16),
            compiler_params=_cparams(1), name="moe_up",
        )(tile_e[t0:t1], jnp.clip(n_used - t0, 0, t1 - t0), xs, w_gate, w_up,
          row_w[r0:r1].reshape(r1 - r0, 1))

    hid = jnp.concatenate([up(0, n_tiles // 2), up(n_tiles // 2, n_tiles)], axis=0)
    rows = pl.pallas_call(
        _moe_down_kernel,
        grid_spec=pltpu.PrefetchScalarGridSpec(
            num_scalar_prefetch=2, grid=(n_tiles,),
            in_specs=[pl.BlockSpec((MOE_TM, f), lambda i, te, nu: (i, 0)),
                      pl.BlockSpec((None, f, d), lambda i, te, nu: (layer, te[i], 0))],
            out_specs=pl.BlockSpec((MOE_TM, d), lambda i, te, nu: (i, 0)),
            scratch_shapes=[pltpu.VMEM((f, d), BF16)]),
        out_shape=jax.ShapeDtypeStruct((n_rows, d), F32),
        compiler_params=_cparams(1), name="moe_down",
    )(tile_e, n_used, hid, w_down)

    return _moe_combine(x, rows, pos.reshape(-1), next_gain, split)


def _combine_kernel(pos_ref, x_ref, rows_hbm, *refs, top_k, with_norm, split_tile):
    if with_norm:
        g_ref, xo_ref, h_ref, buf, sem = refs
    else:
        lo_ref, hi_ref, buf, sem = refs
    i = pl.program_id(0)
    n_tiles = pl.num_programs(0)
    tm = x_ref.shape[0]
    slot = i % 2

    def gather_copy(tile, t, k, slot_):
        r = pos_ref[(tile * tm + t) * top_k + k]
        return pltpu.make_async_copy(rows_hbm.at[pl.ds(r, 1)],
                                     buf.at[slot_, pl.ds(k * tm + t, 1)], sem.at[slot_])

    def start_tile(tile, slot_):
        def body(t, carry):
            for k in range(top_k):
                gather_copy(tile, t, k, slot_).start()
            return carry
        lax.fori_loop(0, tm, body, 0, unroll=8)

    @pl.when(i == 0)
    def _():
        start_tile(0, 0)

    @pl.when(i + 1 < n_tiles)
    def _():
        start_tile(i + 1, 1 - slot)

    pltpu.make_async_copy(rows_hbm.at[pl.ds(0, top_k * tm)], buf.at[slot], sem.at[slot]).wait()
    y = x_ref[...]
    for k in range(top_k):
        y = y + buf[slot, k * tm:(k + 1) * tm]
    if with_norm:
        xo_ref[...] = y
        h_ref[...] = (y * lax.rsqrt(jnp.mean(y * y, axis=-1, keepdims=True) + RMS_EPS)
                      * g_ref[...]).astype(BF16)
    else:
        @pl.when(i < split_tile)
        def _():
            lo_ref[...] = y

        @pl.when(i >= split_tile)
        def _():
            hi_ref[...] = y


def _moe_combine(x, rows, pos, next_gain, split):
    m, d = x.shape
    top_k = pos.shape[0] // m
    tm = LANE
    assert m % tm == 0
    with_norm = next_gain is not None
    x_spec = pl.BlockSpec((tm, d), lambda i, p: (i, 0))
    in_specs = [x_spec, pl.BlockSpec(memory_space=pl.ANY)]
    args = [x, rows]
    if with_norm:
        split_tile = 0
        in_specs.append(pl.BlockSpec((1, d), lambda i, p: (0, 0)))
        args.append(next_gain.reshape(1, d))
        out_specs = [x_spec, x_spec]
        out_shape = [jax.ShapeDtypeStruct((m, d), F32), jax.ShapeDtypeStruct((m, d), BF16)]
    else:
        assert split % tm == 0
        split_tile = split // tm
        out_specs = [pl.BlockSpec((tm, d), lambda i, p: (jnp.minimum(i, split_tile - 1), 0)),
                     pl.BlockSpec((tm, d), lambda i, p: (jnp.maximum(i - split_tile, 0), 0))]
        out_shape = [jax.ShapeDtypeStruct((split, d), F32), jax.ShapeDtypeStruct((m - split, d), F32)]
    return pl.pallas_call(
        functools.partial(_combine_kernel, top_k=top_k, with_norm=with_norm, split_tile=split_tile),
        grid_spec=pltpu.PrefetchScalarGridSpec(
            num_scalar_prefetch=1, grid=(m // tm,), in_specs=in_specs, out_specs=out_specs,
            scratch_shapes=[pltpu.VMEM((2, top_k * tm, d), F32), pltpu.SemaphoreType.DMA((2,))]),
        out_shape=out_shape, compiler_params=_cparams(1),
        name="moe_combine_norm" if with_norm else "moe_combine_split",
    )(pos, *args)


def _even_layer(x, h, i, groups, state_ret, state_rwkv, state_shift, wts):
    (a_w_in, a_w_out, ret_norm_g, rwkv_mu, rwkv_w0, rwkv_w2, rwkv_a0, rwkv_a2, rwkv_g2,
     rwkv_kk_scale, rwkv_ka, rwkv_rk, rwkv_lnx_g, rwkv_lnx_b) = [w[i] for w in wts]
    ret_heads, ret_dk = state_ret.shape[2], state_ret.shape[3]
    rw_heads, rw_n = state_rwkv.shape[2], state_rwkv.shape[3]
    ret_w = ret_heads * ret_dk
    rw_w = rw_heads * rw_n
    lora_w, lora_a, lora_g = rwkv_w2.shape[0], rwkv_a2.shape[0], rwkv_g2.shape[0]
    assert lora_w == LANE and lora_a == LANE and lora_g <= LANE and ret_w == rw_w
    n_main = 4 * ret_w + 3 * rw_w
    n_shift = 3 * rw_w + lora_w + lora_a + lora_g
    (pb, pt), (sb, st) = groups
    m_p, m_s = pb * pt, sb * st

    a_w_in_t = a_w_in.T
    p = _matmul([h], a_w_in_t, n_main, name="a_in", w_t=True)
    tail_cols = MM_TN
    assert n_main % tail_cols == 0 and n_main + tail_cols >= a_w_in.shape[1] and 3 * LANE <= tail_cols
    tail = _matmul([h], a_w_in_t, tail_cols, col_block0=n_main // tail_cols, name="a_in_tail",
                   w_t=True)

    o_ret_p, ret_p = _retention_prompt(p, ret_norm_g, pb, pt, ret_heads, ret_dk)
    o_ret_s, ret_s = _retention_sample(p, ret_norm_g, state_ret[i], m_p, sb, st, ret_heads, ret_dk)

    pad_g = lambda a: jnp.pad(a, ((0, LANE - a.shape[0]), (0, 0)))
    mu = rwkv_mu
    mu_tail = jnp.pad(mu[3 * rw_w:], (0, tail_cols - (n_shift - 3 * rw_w)))
    prep_params = [mu[:rw_w].reshape(1, -1), mu[rw_w:2 * rw_w].reshape(1, -1),
                   mu[2 * rw_w:3 * rw_w].reshape(1, -1), mu_tail.reshape(1, -1),
                   rwkv_w0.reshape(1, -1), rwkv_w2, rwkv_a0.reshape(1, -1), rwkv_a2,
                   pad_g(rwkv_g2)]
    cb0 = (4 * ret_w) // rw_w
    shift0 = state_shift[i]
    sh_main = jnp.repeat(shift0[:, :3 * rw_w], st, axis=0)
    sh_tail = jnp.repeat(jnp.pad(shift0[:, 3 * rw_w:], ((0, 0), (0, tail_cols - (n_shift - 3 * rw_w)))),
                         st, axis=0)
    seq_p = _rwkv_prep(p, tail, 0, m_p, cb0, rw_w, prep_params, None, pt, st)
    seq_s = _rwkv_prep(p, tail, m_p, m_s, cb0, rw_w, prep_params, (sh_main, sh_tail), pt, st)

    kparams = (rwkv_kk_scale, rwkv_ka, rwkv_rk, rwkv_lnx_g, rwkv_lnx_b)
    y_p, rwkv_p = _rwkv_scan_tiled(seq_p, kparams, pb, pt, rw_heads, rw_n)
    y_s, rwkv_s = _rwkv_scan(seq_s, kparams, state_rwkv[i], sb, st, rw_heads, rw_n)

    o_ret = jnp.concatenate([o_ret_p, o_ret_s], axis=0)
    y = jnp.concatenate([y_p, y_s], axis=0)
    x = _matmul([o_ret, y], a_w_out, a_w_out.shape[1], res=x, name="a_out")

    def last_ps(row0, nb, nt):
        def last(a, lo, hi):
            if nb * nt <= 1024:
                return a[row0:row0 + nb * nt, lo:hi].reshape(nb, nt, hi - lo)[:, nt - 1]
            return jnp.concatenate([a[row0 + (b + 1) * nt - 1:row0 + (b + 1) * nt, lo:hi]
                                    for b in range(nb)], axis=0)
        return jnp.concatenate([last(p, 4 * ret_w, n_main), last(tail, 0, n_shift - 3 * rw_w)],
                               axis=-1)

    return x, (ret_p, rwkv_p, last_ps(0, pb, pt)), (ret_s, rwkv_s, last_ps(m_p, sb, st))


def _odd_layer(x, h, i, groups, cache_k, cache_v, wts):
    c_w_in, c_w_out, c_q_norm_g, c_k_norm_g, c_sinks = [w[i] for w in wts]
    win, n_kv, hd = cache_k.shape[2], cache_k.shape[3], cache_k.shape[4]
    n_heads = c_sinks.shape[0]
    group = n_heads // n_kv
    mix_c = n_heads * hd
    kv_w = n_kv * hd
    (pb, pt), (sb, st) = groups
    m_p = pb * pt
    assert pt % win == 0 and mix_c % kv_w == 0
    p = _matmul([h], c_w_in, mix_c + 2 * kv_w, name="c_in")
    kcb, vcb = mix_c // kv_w, mix_c // kv_w + 1

    nb_p = m_p // win
    maps = (lambda n: (n, 0), lambda n: (n, kcb), lambda n: (n, vcb),
            lambda n: (jnp.maximum(n - 1, 0), kcb), lambda n: (jnp.maximum(n - 1, 0), vcb))
    o_p, kn_p = _swa(p, p, p, p, p, maps, nb_p, (win, mix_c), win, False, win, c_q_norm_g,
                     c_k_norm_g, c_sinks, n_kv, group, hd, True, False, pt // win)

    tq = 16
    ps = p[m_p:].reshape(sb, st, -1)
    q_s = jnp.pad(ps[:, :, :mix_c].reshape(sb, st, n_kv, group, hd),
                  ((0, 0), (0, tq - st), (0, 0), (0, 0), (0, 0)))
    q_s = q_s.transpose(0, 3, 1, 2, 4).reshape(sb * group * tq, kv_w)
    kv_pad = lambda a: jnp.pad(a, ((0, 0), (0, win - st), (0, 0))).reshape(sb * win, kv_w)
    k_s = kv_pad(ps[:, :, mix_c:mix_c + kv_w])
    v_s = kv_pad(ps[:, :, mix_c + kv_w:])
    kc0 = cache_k[i].reshape(sb * win, kv_w)
    vc0 = cache_v[i].reshape(sb * win, kv_w)
    same = lambda n: (n, 0)
    o_s, kn_s = _swa(q_s, k_s, v_s, kc0, vc0, (same,) * 5, sb, (group * tq, kv_w), tq, True, win,
                     c_q_norm_g, c_k_norm_g, c_sinks, n_kv, group, hd, False, True, 1)
    o_s = o_s.reshape(sb, group, tq, n_kv, hd)[:, :, :st].transpose(0, 2, 3, 1, 4).reshape(
        sb * st, mix_c)

    x = _matmul([jnp.concatenate([o_p, o_s], axis=0)], c_w_out, c_w_out.shape[1], res=x, name="c_out")

    new_k_p = kn_p.reshape(pb, pt, n_kv, hd)[:, -win:]
    new_v_p = p[:m_p, mix_c + kv_w:].reshape(pb, pt, n_kv, hd)[:, -win:]
    kn_s = kn_s.reshape(sb, win, n_kv, hd)[:, :st]
    new_k_s = jnp.concatenate([cache_k[i], kn_s], axis=1)[:, -win:]
    new_v_s = jnp.concatenate([cache_v[i], ps[:, :, mix_c + kv_w:].reshape(sb, st, n_kv, hd)],
                              axis=1)[:, -win:]
    return x, (new_k_p, new_v_p), (new_k_s, new_v_s)


def kernel(x_prompt, x_sample, state_ret, state_rwkv, state_rwkv_shift, cache_swa_k, cache_swa_v, norm_mix_g, norm_ffn_g, a_w_in, a_w_out, ret_norm_g, rwkv_mu, rwkv_w0, rwkv_w2, rwkv_a0, rwkv_a2, rwkv_g2, rwkv_kk_scale, rwkv_ka, rwkv_rk, rwkv_lnx_g, rwkv_lnx_b, c_w_in, c_w_out, c_q_norm_g, c_k_norm_g, c_sinks, moe_w_group, moe_b_group, moe_w_expert, moe_b_expert, moe_w_gate, moe_w_up, moe_w_down):
    pb, pt, d = x_prompt.shape
    sb, st, _ = x_sample.shape
    groups = ((pb, pt), (sb, st))
    depth = norm_mix_g.shape[0]
    a_wts = (a_w_in, a_w_out, ret_norm_g, rwkv_mu, rwkv_w0, rwkv_w2, rwkv_a0, rwkv_a2, rwkv_g2,
             rwkv_kk_scale, rwkv_ka, rwkv_rk, rwkv_lnx_g, rwkv_lnx_b)
    c_wts = (c_w_in, c_w_out, c_q_norm_g, c_k_norm_g, c_sinks)
    st_p = [[] for _ in range(5)]
    st_s = [[] for _ in range(5)]
    for l in range(depth):
        i = l // 2
        if l == 0:
            h, x = _rms_norm_first(x_prompt.reshape(pb * pt, d), x_sample.reshape(sb * st, d),
                                   norm_mix_g[l])
        else:
            h = h_next
        if l % 2 == 0:
            x, sp, ss = _even_layer(x, h, i, groups, state_ret, state_rwkv, state_rwkv_shift, a_wts)
            for j in range(3):
                st_p[j].append(sp[j])
                st_s[j].append(ss[j])
        else:
            x, sp, ss = _odd_layer(x, h, i, groups, cache_swa_k, cache_swa_v, c_wts)
            for j in range(2):
                st_p[3 + j].append(sp[j])
                st_s[3 + j].append(ss[j])
        n_router = moe_w_group.shape[2] + moe_w_expert.shape[2]
        w_router = jnp.pad(jnp.concatenate([moe_w_group[l], moe_w_expert[l]], axis=1),
                           ((0, 0), (0, LANE - n_router)))
        h, logits = _rms_norm_router(x, norm_ffn_g[l], w_router)
        moe_args = (x, h, logits, moe_b_group[l], moe_b_expert[l], moe_w_gate, moe_w_up, moe_w_down, l)
        if l < depth - 1:
            x, h_next = _moe(*moe_args, next_gain=norm_mix_g[l + 1])
        else:
            y_p, y_s = _moe(*moe_args, split=pb * pt)
    y_prompt = y_p.reshape(pb, pt, d)
    y_sample = y_s.reshape(sb, st, d)
    stack = lambda s: s[0][None] if len(s) == 1 else jnp.stack(s)
    return (y_prompt, y_sample) + tuple(stack(s) for s in st_p) + tuple(stack(s) for s in st_s)
```
